```python
import math
import jax, jax.numpy as jnp
from jax import lax
import numpy as np

D_MODEL = 2048
BATCH = 4
SEQ = 2048
DEPTH = 4
DEC_BATCH = 8
DEC_SEQ = 8
PAST_LEN = 16384
PAGE_SIZE = 128

N_MIXERS = 2
N_GDN = (DEPTH + 1) // 2
N_DSA = DEPTH // 2
GDN_DK = 128
GDN_DV = 128
GDN_HK = D_MODEL // GDN_DK
GDN_HV = 2 * GDN_HK
GDN_KD = GDN_HK * GDN_DK
GDN_VD = GDN_HV * GDN_DV
GDN_CONV_DIM = 2 * GDN_KD + GDN_VD
GDN_IN = GDN_CONV_DIM + GDN_VD + 2 * GDN_HV
CONV_W = 4
GDN_CHUNK = 64
HEAD_DIM = 128
N_HEADS = D_MODEL // HEAD_DIM
N_KV_HEADS = 4
IDX_HEADS = 16
IDX_DIM = 128
DSA_QD = N_HEADS * HEAD_DIM
DSA_KVD = N_KV_HEADS * HEAD_DIM
DSA_IN = DSA_QD + 2 * DSA_KVD + IDX_HEADS * IDX_DIM + IDX_DIM + IDX_HEADS
TOPK_MAX = 256
Q_BLOCK = 128
ROPE_THETA = 10000.0
N_GROUPS = 4
EXPERTS_PER_GROUP = 8
N_EXPERTS = N_GROUPS * EXPERTS_PER_GROUP
TOPK_EXPERTS = 2
D_EXPERT = D_MODEL // 4
ALPHA = (2 * DEPTH) ** 0.25
BETA = (8 * DEPTH) ** -0.25
LN_EPS = 1e-5
NORM_EPS = 1e-6

kernel_name = "hybrid_gdn_dsa_hiermoe_deepnorm_step"

F32 = jnp.float32


def layer_norm(x, g, b):
    xf = x.astype(F32)
    mu = jnp.mean(xf, -1, keepdims=True)
    xc = xf - mu
    var = jnp.mean(xc * xc, -1, keepdims=True)
    return (xc * lax.rsqrt(var + LN_EPS) * g.astype(F32) + b.astype(F32)).astype(x.dtype)


def l2norm(x):
    return x * lax.rsqrt(jnp.sum(x * x, -1, keepdims=True) + NORM_EPS)


def rope(x, pos):
    half = x.shape[-1] // 2
    inv = jnp.power(ROPE_THETA, -jnp.arange(half, dtype=F32) / half)
    ang = pos.astype(F32)[:, None] * inv[None, :]
    cos = jnp.cos(ang)[None, :, None, :]
    sin = jnp.sin(ang)[None, :, None, :]
    x1 = x[..., :half].astype(F32)
    x2 = x[..., half:].astype(F32)
    return jnp.concatenate([x1 * cos - x2 * sin, x2 * cos + x1 * sin], -1).astype(x.dtype)


def gdn_chunked(q, k, v, g, beta, s0):
    B, L, H, _ = q.shape
    DV = v.shape[-1]
    C = min(GDN_CHUNK, L)
    pad = (-L) % C
    if pad:
        pw = ((0, 0), (0, pad), (0, 0), (0, 0))
        q, k, v = jnp.pad(q, pw), jnp.pad(k, pw), jnp.pad(v, pw)
        g, beta = jnp.pad(g, pw[:3]), jnp.pad(beta, pw[:3])
    N = (L + pad) // C

    def chunks(t):
        t = t.reshape((B, N, C, H) + t.shape[3:])
        return jnp.moveaxis(t, (1, 3), (0, 2))

    q, k, v, g, beta = chunks(q), chunks(k), chunks(v), chunks(g), chunks(beta)
    gc = jnp.cumsum(g, axis=-1)
    incl = jnp.tril(jnp.ones((C, C), bool))
    strict = jnp.tril(jnp.ones((C, C), bool), -1)
    decay = jnp.exp(jnp.where(incl, gc[..., :, None] - gc[..., None, :], -jnp.inf))
    kb = k * beta[..., None]
    vb = v * beta[..., None]
    lmat = jnp.where(strict, jnp.einsum('nbhcd,nbhed->nbhce', kb, k) * decay, 0.0)
    eye = jnp.eye(C, dtype=lmat.dtype)
    tmat = lax.linalg.triangular_solve(lmat + eye, jnp.broadcast_to(eye, lmat.shape),
                                       left_side=True, lower=True, unit_diagonal=True)
    u = jnp.einsum('nbhce,nbhed->nbhcd', tmat, vb)
    w = jnp.einsum('nbhce,nbhed->nbhcd', tmat, kb * jnp.exp(gc)[..., None])
    qk = jnp.einsum('nbhcd,nbhed->nbhce', q, k) * decay

    def step(s, xs):
        q_i, k_i, u_i, w_i, gc_i, qk_i = xs
        v_new = u_i - jnp.einsum('bhcd,bhde->bhce', w_i, s)
        o = (jnp.einsum('bhcd,bhde->bhce', q_i * jnp.exp(gc_i)[..., None], s)
             + jnp.einsum('bhcs,bhse->bhce', qk_i, v_new))
        g_last = gc_i[..., -1:]
        s = (s * jnp.exp(g_last)[..., None]
             + jnp.einsum('bhcd,bhce->bhde', k_i * jnp.exp(g_last - gc_i)[..., None], v_new))
        return s, o

    s, o = lax.scan(step, s0, (q, k, u, w, gc, qk))
    o = jnp.moveaxis(o, (0, 2), (1, 3)).reshape(B, N * C, H, DV)[:, :L]
    return o, s


def gdn_mixer(x, s0, conv0, w_in, conv_w, a_log, dt_bias, norm_w, w_out):
    B, L, _ = x.shape
    proj = x @ w_in
    qkv = proj[..., :GDN_CONV_DIM]
    z = proj[..., GDN_CONV_DIM:GDN_CONV_DIM + GDN_VD]
    b = proj[..., GDN_CONV_DIM + GDN_VD:GDN_CONV_DIM + GDN_VD + GDN_HV]
    a = proj[..., GDN_CONV_DIM + GDN_VD + GDN_HV:]
    xc = jnp.concatenate([conv0.astype(qkv.dtype), qkv], axis=1)
    conv = xc[:, 0:L] * conv_w[0]
    for j in range(1, CONV_W):
        conv = conv + xc[:, j:j + L] * conv_w[j]
    new_conv = xc[:, L:]
    conv = jax.nn.silu(conv).astype(F32)
    q = conv[..., :GDN_KD].reshape(B, L, GDN_HK, GDN_DK)
    k = conv[..., GDN_KD:2 * GDN_KD].reshape(B, L, GDN_HK, GDN_DK)
    v = conv[..., 2 * GDN_KD:].reshape(B, L, GDN_HV, GDN_DV)
    rep = GDN_HV // GDN_HK
    q = jnp.repeat(l2norm(q) * (GDN_DK ** -0.5), rep, axis=2)
    k = jnp.repeat(l2norm(k), rep, axis=2)
    beta = jax.nn.sigmoid(b.astype(F32))
    g = -jnp.exp(a_log.astype(F32)) * jax.nn.softplus(a.astype(F32) + dt_bias.astype(F32))
    o, s = gdn_chunked(q, k, v, g, beta, s0.astype(F32))
    zf = z.astype(F32).reshape(B, L, GDN_HV, GDN_DV)
    o = o * lax.rsqrt(jnp.mean(o * o, -1, keepdims=True) + NORM_EPS) * norm_w.astype(F32) * jax.nn.silu(zf)
    y = o.reshape(B, L, GDN_VD).astype(x.dtype) @ w_out
    return y, s.astype(x.dtype), new_conv


def dsa_project(x, pos, w_in):
    B, L, _ = x.shape
    p = x @ w_in
    o1 = DSA_QD
    o2 = o1 + DSA_KVD
    o3 = o2 + DSA_KVD
    o4 = o3 + IDX_HEADS * IDX_DIM
    o5 = o4 + IDX_DIM
    q = rope(p[..., :o1].reshape(B, L, N_HEADS, HEAD_DIM), pos)
    k = rope(p[..., o1:o2].reshape(B, L, N_KV_HEADS, HEAD_DIM), pos)
    v = p[..., o2:o3].reshape(B, L, N_KV_HEADS, HEAD_DIM)
    qi = rope(p[..., o3:o4].reshape(B, L, IDX_HEADS, IDX_DIM), pos)
    ki = rope(p[..., o4:o5].reshape(B, L, 1, IDX_DIM), pos)[:, :, 0]
    wi = p[..., o5:] * (IDX_HEADS ** -0.5 * IDX_DIM ** -0.5)
    return q, k, v, qi, ki, wi


def index_topk(qi, wi, ki, qpos, topk):
    s = jnp.einsum('bqhd,bsd->bqhs', qi.astype(F32), ki.astype(F32))
    score = jnp.einsum('bqhs,bqh->bqs', jax.nn.relu(s), wi.astype(F32))
    valid = jnp.arange(ki.shape[1])[None, :] <= qpos[:, None]
    score = jnp.where(valid[None], score, -jnp.inf)
    _, idx = lax.top_k(score, topk)
    return idx


def sparse_attend(q, k_sel, v_sel, sel_valid):
    B, Q = q.shape[:2]
    qg = q.reshape(B, Q, N_KV_HEADS, N_HEADS // N_KV_HEADS, HEAD_DIM)
    s = jnp.einsum('bqngd,bqknd->bqngk', qg, k_sel).astype(F32) * (HEAD_DIM ** -0.5)
    s = jnp.where(sel_valid[:, :, None, None, :], s, -jnp.inf)
    p = jax.nn.softmax(s, axis=-1).astype(v_sel.dtype)
    o = jnp.einsum('bqngk,bqknd->bqngd', p, v_sel)
    return o.reshape(B, Q, DSA_QD)


def gather_rows(t, idx):
    return jax.vmap(lambda tb, ib: tb[ib])(t, idx)


def dsa_prompt(x, w_in, w_out):
    B, L, _ = x.shape
    pos = jnp.arange(L)
    q, k, v, qi, ki, wi = dsa_project(x, pos, w_in)
    topk = min(TOPK_MAX, L // 4)
    qb = min(Q_BLOCK, L)

    def block(i):
        start = i * qb
        q_b = lax.dynamic_slice_in_dim(q, start, qb, 1)
        qi_b = lax.dynamic_slice_in_dim(qi, start, qb, 1)
        wi_b = lax.dynamic_slice_in_dim(wi, start, qb, 1)
        qpos = start + jnp.arange(qb)
        idx = index_topk(qi_b, wi_b, ki, qpos, topk)
        return sparse_attend(q_b, gather_rows(k, idx), gather_rows(v, idx), idx <= qpos[None, :, None])

    o = lax.map(block, jnp.arange(L // qb))
    o = jnp.moveaxis(o, 0, 1).reshape(B, L, DSA_QD)
    return o @ w_out, k, v, ki


def dsa_sample(x, ck, cv, cki, page_table, w_in, w_out):
    B, L, _ = x.shape
    past = page_table.shape[1] * PAGE_SIZE
    pos = past + jnp.arange(L)
    q, k, v, qi, ki, wi = dsa_project(x, pos, w_in)
    ki_past = cki[page_table].reshape(B, past, IDX_DIM)
    ki_all = jnp.concatenate([ki_past.astype(ki.dtype), ki], axis=1)
    topk = min(TOPK_MAX, (past + L) // 4)
    idx = index_topk(qi, wi, ki_all, pos, topk)
    from_past = (idx < past)[..., None, None]
    pidx = jnp.minimum(idx, past - 1)
    phys_page = jnp.take_along_axis(page_table, (pidx // PAGE_SIZE).reshape(B, -1), axis=1).reshape(pidx.shape)
    phys = phys_page * PAGE_SIZE + pidx % PAGE_SIZE
    nidx = jnp.clip(idx - past, 0, L - 1)
    ck_flat = ck.reshape(-1, N_KV_HEADS, HEAD_DIM)
    cv_flat = cv.reshape(-1, N_KV_HEADS, HEAD_DIM)
    k_sel = jnp.where(from_past, ck_flat[phys].astype(k.dtype), gather_rows(k, nidx))
    v_sel = jnp.where(from_past, cv_flat[phys].astype(v.dtype), gather_rows(v, nidx))
    o = sparse_attend(q, k_sel, v_sel, idx <= pos[None, :, None])
    return o @ w_out, k, v, ki


def grouped_experts(xt, eidx, gate, w1, w3, w2):
    T, D = xt.shape
    K = eidx.shape[1]
    A = T * K
    E = w1.shape[0]
    blk = max(8, min(128, A // E))
    nb = A // blk + E
    flat_e = eidx.reshape(-1)
    order = jnp.argsort(flat_e)
    se = flat_e[order]
    tok = order // K
    counts = jnp.bincount(flat_e, length=E)
    padded = (counts + blk - 1) // blk * blk
    pend = jnp.cumsum(padded)
    pstart = pend - padded
    cstart = jnp.cumsum(counts) - counts
    dest = pstart[se] + jnp.arange(A) - cstart[se]
    slot_tok = jnp.full((nb * blk,), T, jnp.int32).at[dest].set(tok.astype(jnp.int32))
    blk_e = jnp.minimum(jnp.searchsorted(pend, jnp.arange(nb) * blk, side='right'), E - 1)
    xpad = jnp.concatenate([xt, jnp.zeros((1, D), xt.dtype)], axis=0)
    xb = xpad[slot_tok].reshape(nb, blk, D)

    def run(args):
        xb_i, e = args
        h = jax.nn.silu(xb_i @ w1[e]) * (xb_i @ w3[e])
        return h @ w2[e]

    yb = lax.map(run, (xb, blk_e)).reshape(nb * blk, D)
    y_assign = yb[dest] * gate.reshape(-1)[order][:, None].astype(yb.dtype)
    return jnp.zeros((T, D), yb.dtype).at[tok].add(y_assign)


def hier_moe(x, wg, bg, we, be, w1, w3, w2):
    B, L, D = x.shape
    xt = x.reshape(-1, D)
    T = xt.shape[0]
    lg = (xt @ wg).astype(F32) + bg.astype(F32)
    pg = jax.nn.softmax(lg, axis=-1)
    gsel = jnp.argmax(lg, axis=-1)
    le = ((xt @ we).astype(F32) + be.astype(F32)).reshape(T, N_GROUPS, EXPERTS_PER_GROUP)
    le_g = jnp.take_along_axis(le, gsel[:, None, None], axis=1)[:, 0]
    pe = jax.nn.softmax(le_g, axis=-1)
    top_p, top_i = lax.top_k(pe, TOPK_EXPERTS)
    gate = top_p / jnp.sum(top_p, -1, keepdims=True) * jnp.take_along_axis(pg, gsel[:, None], axis=1)
    eidx = gsel[:, None] * EXPERTS_PER_GROUP + top_i
    return grouped_experts(xt, eidx, gate, w1, w3, w2).reshape(B, L, D)


def setup_inputs(seed: int = 0) -> dict:
    key = jax.random.key(seed)
    ks = jax.random.split(key, 32)
    n_pages = PAST_LEN // PAGE_SIZE
    n_used = DEC_BATCH * n_pages
    n_pool = (n_used * 5) // 4

    def nrm(k, shape, scale):
        return jax.random.normal(k, shape, F32) * scale

    perm = jax.random.permutation(ks[7], n_pool)
    page_table = perm[:n_used].reshape(DEC_BATCH, n_pages).astype(jnp.int32)
    dt = jnp.exp(jax.random.uniform(ks[11], (N_GDN, GDN_HV), F32, math.log(1e-3), math.log(1e-1)))
    return {
        "x_prompt": nrm(ks[0], (BATCH, SEQ, D_MODEL), 1.0),
        "x_sample": nrm(ks[1], (DEC_BATCH, DEC_SEQ, D_MODEL), 1.0),
        "state_gdn_s": nrm(ks[2], (N_GDN, DEC_BATCH, GDN_HV, GDN_DK, GDN_DV), 0.1),
        "state_gdn_conv": nrm(ks[3], (N_GDN, DEC_BATCH, CONV_W - 1, GDN_CONV_DIM), 1.0),
        "cache_k": nrm(ks[4], (N_DSA, n_pool, PAGE_SIZE, N_KV_HEADS, HEAD_DIM), 1.0),
        "cache_v": nrm(ks[5], (N_DSA, n_pool, PAGE_SIZE, N_KV_HEADS, HEAD_DIM), 1.0),
        "cache_kidx": nrm(ks[6], (N_DSA, n_pool, PAGE_SIZE, IDX_DIM), 1.0),
        "page_table": page_table,
        "gdn_w_in": nrm(ks[8], (N_GDN, D_MODEL, GDN_IN), D_MODEL ** -0.5),
        "gdn_conv_w": nrm(ks[9], (N_GDN, CONV_W, GDN_CONV_DIM), CONV_W ** -0.5),
        "gdn_a_log": jnp.log(jax.random.uniform(ks[10], (N_GDN, GDN_HV), F32, 1.0, 16.0)),
        "gdn_dt_bias": dt + jnp.log(-jnp.expm1(-dt)),
        "gdn_norm_w": 1.0 + nrm(ks[12], (N_GDN, GDN_DV), 0.02),
        "gdn_w_out": nrm(ks[13], (N_GDN, GDN_VD, D_MODEL), GDN_VD ** -0.5 * BETA),
        "dsa_w_in": nrm(ks[14], (N_DSA, D_MODEL, DSA_IN), D_MODEL ** -0.5),
        "dsa_w_out": nrm(ks[15], (N_DSA, DSA_QD, D_MODEL), DSA_QD ** -0.5 * BETA),
        "ln1_g": 1.0 + nrm(ks[16], (DEPTH, D_MODEL), 0.02),
        "ln1_b": nrm(ks[17], (DEPTH, D_MODEL), 0.02),
        "ln2_g": 1.0 + nrm(ks[18], (DEPTH, D_MODEL), 0.02),
        "ln2_b": nrm(ks[19], (DEPTH, D_MODEL), 0.02),
        "moe_wg": nrm(ks[20], (DEPTH, D_MODEL, N_GROUPS), D_MODEL ** -0.5),
        "moe_bg": nrm(ks[21], (DEPTH, N_GROUPS), 0.01),
        "moe_we": nrm(ks[22], (DEPTH, D_MODEL, N_EXPERTS), D_MODEL ** -0.5),
        "moe_be": nrm(ks[23], (DEPTH, N_EXPERTS), 0.01),
        "moe_w1": nrm(ks[24], (DEPTH, N_EXPERTS, D_MODEL, D_EXPERT), D_MODEL ** -0.5),
        "moe_w3": nrm(ks[25], (DEPTH, N_EXPERTS, D_MODEL, D_EXPERT), D_MODEL ** -0.5),
        "moe_w2": nrm(ks[26], (DEPTH, N_EXPERTS, D_EXPERT, D_MODEL), D_EXPERT ** -0.5 * BETA),
    }


def reference(x_prompt, x_sample, state_gdn_s, state_gdn_conv, cache_k, cache_v, cache_kidx, page_table,
              gdn_w_in, gdn_conv_w, gdn_a_log, gdn_dt_bias, gdn_norm_w, gdn_w_out,
              dsa_w_in, dsa_w_out, ln1_g, ln1_b, ln2_g, ln2_b,
              moe_wg, moe_bg, moe_we, moe_be, moe_w1, moe_w3, moe_w2):
    xp, xs = x_prompt, x_sample
    p_s, p_c, s_s, s_c = [], [], [], []
    p_k, p_v, p_ki, s_k, s_v, s_ki = [], [], [], [], [], []
    for i in range(DEPTH):
        j = i // N_MIXERS
        if i % N_MIXERS == 0:
            gp = (gdn_w_in[j], gdn_conv_w[j], gdn_a_log[j], gdn_dt_bias[j], gdn_norm_w[j], gdn_w_out[j])
            s0 = jnp.zeros((xp.shape[0], GDN_HV, GDN_DK, GDN_DV), F32)
            c0 = jnp.zeros((xp.shape[0], CONV_W - 1, GDN_CONV_DIM), xp.dtype)
            yp, sp, cp = gdn_mixer(xp, s0, c0, *gp)
            ys, ss, cs = gdn_mixer(xs, state_gdn_s[j], state_gdn_conv[j], *gp)
            p_s.append(sp); p_c.append(cp); s_s.append(ss); s_c.append(cs)
        else:
            yp, kp, vp, kip = dsa_prompt(xp, dsa_w_in[j], dsa_w_out[j])
            ys, kn, vn, kin = dsa_sample(xs, cache_k[j], cache_v[j], cache_kidx[j], page_table,
                                         dsa_w_in[j], dsa_w_out[j])
            p_k.append(kp); p_v.append(vp); p_ki.append(kip)
            s_k.append(kn); s_v.append(vn); s_ki.append(kin)
        xp = layer_norm(ALPHA * xp + yp, ln1_g[i], ln1_b[i])
        xs = layer_norm(ALPHA * xs + ys, ln1_g[i], ln1_b[i])
        mp = (moe_wg[i], moe_bg[i], moe_we[i], moe_be[i], moe_w1[i], moe_w3[i], moe_w2[i])
        xp = layer_norm(ALPHA * xp + hier_moe(xp, *mp), ln2_g[i], ln2_b[i])
        xs = layer_norm(ALPHA * xs + hier_moe(xs, *mp), ln2_g[i], ln2_b[i])
    p_gdn_s = jnp.stack(p_s)
    p_gdn_conv = jnp.stack(p_c)
    p_k_rows = jnp.stack(p_k)
    p_v_rows = jnp.stack(p_v)
    p_kidx_rows = jnp.stack(p_ki)
    s_gdn_s = jnp.stack(s_s)
    s_gdn_conv = jnp.stack(s_c)
    s_k_rows = jnp.stack(s_k)
    s_v_rows = jnp.stack(s_v)
    s_kidx_rows = jnp.stack(s_ki)
    return (xp, xs, p_gdn_s, p_gdn_conv, p_k_rows, p_v_rows, p_kidx_rows,
            s_gdn_s, s_gdn_conv, s_k_rows, s_v_rows, s_kidx_rows)
```

```python
import functools
import math

import jax
import jax.numpy as jnp
from jax import lax
from jax.experimental import pallas as pl
from jax.experimental.pallas import tpu as pltpu

D_MODEL = 2048
DEPTH = 4
PAGE_SIZE = 128
N_MIXERS = 2
GDN_DK = 128
GDN_DV = 128
GDN_HK = D_MODEL // GDN_DK
GDN_HV = 2 * GDN_HK
GDN_KD = GDN_HK * GDN_DK
GDN_VD = GDN_HV * GDN_DV
GDN_CONV_DIM = 2 * GDN_KD + GDN_VD
GDN_IN = GDN_CONV_DIM + GDN_VD + 2 * GDN_HV
CONV_W = 4
GDN_CHUNK = 64
HEAD_DIM = 128
N_HEADS = D_MODEL // HEAD_DIM
N_KV_HEADS = 4
IDX_HEADS = 16
IDX_DIM = 128
DSA_QD = N_HEADS * HEAD_DIM
DSA_KVD = N_KV_HEADS * HEAD_DIM
DSA_IN = DSA_QD + 2 * DSA_KVD + IDX_HEADS * IDX_DIM + IDX_DIM + IDX_HEADS
TOPK_MAX = 256
Q_BLOCK = 128
ROPE_THETA = 10000.0
N_GROUPS = 4
EXPERTS_PER_GROUP = 8
N_EXPERTS = N_GROUPS * EXPERTS_PER_GROUP
TOPK_EXPERTS = 2
D_EXPERT = D_MODEL // 4
ALPHA = (2 * DEPTH) ** 0.25
LN_EPS = 1e-5
NORM_EPS = 1e-6

F32 = jnp.float32
BF16 = jnp.bfloat16
VMEM_LIMIT = 56 * 1024 * 1024


def _matmul_kernel(x_ref, w_ref, o_ref):
    o_ref[...] = jnp.dot(x_ref[...].astype(BF16), w_ref[...].astype(BF16),
                         preferred_element_type=F32)


def matmul(x, w, n_out=None, col_block0=0, tm=512, tn=512):
    M, K = x.shape
    N = w.shape[1] if n_out is None else n_out
    tm = min(tm, M)
    tn = min(tn, N)
    assert M % tm == 0 and N % tn == 0
    return pl.pallas_call(
        _matmul_kernel,
        grid=(M // tm, N // tn),
        in_specs=[pl.BlockSpec((tm, K), lambda i, j: (i, 0)),
                  pl.BlockSpec((K, tn), lambda i, j: (0, j + col_block0))],
        out_specs=pl.BlockSpec((tm, tn), lambda i, j: (i, j)),
        out_shape=jax.ShapeDtypeStruct((M, N), F32),
        compiler_params=pltpu.CompilerParams(
            dimension_semantics=("parallel", "arbitrary"), vmem_limit_bytes=VMEM_LIMIT),
    )(x, w)


def proj(x, w, n_main, tn_main=512):
    B, L, D = x.shape
    xt = x.reshape(B * L, D)
    N = w.shape[1]
    main = matmul(xt, w, n_out=n_main, tn=tn_main)
    tail_n = N - n_main
    assert 0 < tail_n <= 128 and n_main % 128 == 0
    tail = matmul(xt, w, n_out=128, col_block0=n_main // 128, tn=128)[:, :tail_n]
    return main.reshape(B, L, n_main), tail.reshape(B, L, tail_n)


def mm3(x, w):
    B, L, D = x.shape
    return matmul(x.reshape(B * L, D), w).reshape(B, L, w.shape[1])


def layer_norm(x, g, b):
    mu = jnp.mean(x, -1, keepdims=True)
    xc = x - mu
    var = jnp.mean(xc * xc, -1, keepdims=True)
    return xc * lax.rsqrt(var + LN_EPS) * g + b


def l2norm(x):
    return x * lax.rsqrt(jnp.sum(x * x, -1, keepdims=True) + NORM_EPS)


def rope(x, pos):
    half = x.shape[-1] // 2
    inv = jnp.power(ROPE_THETA, -jnp.arange(half, dtype=F32) / half)
    ang = pos.astype(F32)[:, None] * inv[None, :]
    cos = jnp.cos(ang)[None, :, None, :]
    sin = jnp.sin(ang)[None, :, None, :]
    x1 = x[..., :half]
    x2 = x[..., half:]
    return jnp.concatenate([x1 * cos - x2 * sin, x2 * cos + x1 * sin], -1)


def gdn_chunked(q, k, v, g, beta, s0):
    B, L, H, _ = q.shape
    DV = v.shape[-1]
    C = min(GDN_CHUNK, L)
    pad = (-L) % C
    if pad:
        pw = ((0, 0), (0, pad), (0, 0), (0, 0))
        q, k, v = jnp.pad(q, pw), jnp.pad(k, pw), jnp.pad(v, pw)
        g, beta = jnp.pad(g, pw[:3]), jnp.pad(beta, pw[:3])
    N = (L + pad) // C

    def chunks(t):
        t = t.reshape((B, N, C, H) + t.shape[3:])
        return jnp.moveaxis(t, (1, 3), (0, 2))

    q, k, v, g, beta = chunks(q), chunks(k), chunks(v), chunks(g), chunks(beta)
    gc = jnp.cumsum(g, axis=-1)
    incl = jnp.tril(jnp.ones((C, C), bool))
    strict = jnp.tril(jnp.ones((C, C), bool), -1)
    decay = jnp.exp(jnp.where(incl, gc[..., :, None] - gc[..., None, :], -jnp.inf))
    kb = k * beta[..., None]
    vb = v * beta[..., None]
    lmat = jnp.where(strict, jnp.einsum('nbhcd,nbhed->nbhce', kb, k) * decay, 0.0)
    eye = jnp.eye(C, dtype=lmat.dtype)
    tmat = lax.linalg.triangular_solve(lmat + eye, jnp.broadcast_to(eye, lmat.shape),
                                       left_side=True, lower=True, unit_diagonal=True)
    u = jnp.einsum('nbhce,nbhed->nbhcd', tmat, vb)
    w = jnp.einsum('nbhce,nbhed->nbhcd', tmat, kb * jnp.exp(gc)[..., None])
    qk = jnp.einsum('nbhcd,nbhed->nbhce', q, k) * decay

    def step(s, xs):
        q_i, k_i, u_i, w_i, gc_i, qk_i = xs
        v_new = u_i - jnp.einsum('bhcd,bhde->bhce', w_i, s)
        o = (jnp.einsum('bhcd,bhde->bhce', q_i * jnp.exp(gc_i)[..., None], s)
             + jnp.einsum('bhcs,bhse->bhce', qk_i, v_new))
        g_last = gc_i[..., -1:]
        s = (s * jnp.exp(g_last)[..., None]
             + jnp.einsum('bhcd,bhce->bhde', k_i * jnp.exp(g_last - gc_i)[..., None], v_new))
        return s, o

    s, o = lax.scan(step, s0, (q, k, u, w, gc, qk))
    o = jnp.moveaxis(o, (0, 2), (1, 3)).reshape(B, N * C, H, DV)[:, :L]
    return o, s


def gdn_mixer(x, s0, conv0, w_in, conv_w, a_log, dt_bias, norm_w, w_out):
    B, L, _ = x.shape
    main, tail = proj(x, w_in, GDN_CONV_DIM + GDN_VD)
    qkv = main[..., :GDN_CONV_DIM]
    z = main[..., GDN_CONV_DIM:]
    b = tail[..., :GDN_HV]
    a = tail[..., GDN_HV:]
    xc = jnp.concatenate([conv0, qkv], axis=1)
    conv = xc[:, 0:L] * conv_w[0]
    for j in range(1, CONV_W):
        conv = conv + xc[:, j:j + L] * conv_w[j]
    new_conv = xc[:, L:]
    conv = jax.nn.silu(conv)
    q = conv[..., :GDN_KD].reshape(B, L, GDN_HK, GDN_DK)
    k = conv[..., GDN_KD:2 * GDN_KD].reshape(B, L, GDN_HK, GDN_DK)
    v = conv[..., 2 * GDN_KD:].reshape(B, L, GDN_HV, GDN_DV)
    rep = GDN_HV // GDN_HK
    q = jnp.repeat(l2norm(q) * (GDN_DK ** -0.5), rep, axis=2)
    k = jnp.repeat(l2norm(k), rep, axis=2)
    beta = jax.nn.sigmoid(b)
    g = -jnp.exp(a_log) * jax.nn.softplus(a + dt_bias)
    o, s = gdn_chunked(q, k, v, g, beta, s0)
    zf = z.reshape(B, L, GDN_HV, GDN_DV)
    o = o * lax.rsqrt(jnp.mean(o * o, -1, keepdims=True) + NORM_EPS) * norm_w * jax.nn.silu(zf)
    y = mm3(o.reshape(B, L, GDN_VD), w_out)
    return y, s, new_conv


def dsa_project(x, pos, w_in):
    B, L, _ = x.shape
    o1 = DSA_QD
    o2 = o1 + DSA_KVD
    o3 = o2 + DSA_KVD
    o4 = o3 + IDX_HEADS * IDX_DIM
    o5 = o4 + IDX_DIM
    p, tail = proj(x, w_in, o5, tn_main=128)
    q = rope(p[..., :o1].reshape(B, L, N_HEADS, HEAD_DIM), pos)
    k = rope(p[..., o1:o2].reshape(B, L, N_KV_HEADS, HEAD_DIM), pos)
    v = p[..., o2:o3].reshape(B, L, N_KV_HEADS, HEAD_DIM)
    qi = rope(p[..., o3:o4].reshape(B, L, IDX_HEADS, IDX_DIM), pos)
    ki = rope(p[..., o4:o5].reshape(B, L, 1, IDX_DIM), pos)[:, :, 0]
    wi = tail * (IDX_HEADS ** -0.5 * IDX_DIM ** -0.5)
    return q, k, v, qi, ki, wi


def index_topk(qi, wi, ki, qpos, topk):
    s = jnp.einsum('bqhd,bsd->bqhs', qi, ki)
    score = jnp.einsum('bqhs,bqh->bqs', jax.nn.relu(s), wi)
    valid = jnp.arange(ki.shape[1])[None, :] <= qpos[:, None]
    score = jnp.where(valid[None], score, -jnp.inf)
    _, idx = lax.top_k(score, topk)
    return idx


def sparse_attend(q, k_sel, v_sel, sel_valid):
    B, Q = q.shape[:2]
    qg = q.reshape(B, Q, N_KV_HEADS, N_HEADS // N_KV_HEADS, HEAD_DIM)
    s = jnp.einsum('bqngd,bqknd->bqngk', qg, k_sel) * (HEAD_DIM ** -0.5)
    s = jnp.where(sel_valid[:, :, None, None, :], s, -jnp.inf)
    p = jax.nn.softmax(s, axis=-1)
    o = jnp.einsum('bqngk,bqknd->bqngd', p, v_sel)
    return o.reshape(B, Q, DSA_QD)


def gather_rows(t, idx):
    return jax.vmap(lambda tb, ib: tb[ib])(t, idx)


def dsa_prompt(x, w_in, w_out):
    B, L, _ = x.shape
    pos = jnp.arange(L)
    q, k, v, qi, ki, wi = dsa_project(x, pos, w_in)
    topk = min(TOPK_MAX, L // 4)
    qb = min(Q_BLOCK, L)

    def block(i):
        start = i * qb
        q_b = lax.dynamic_slice_in_dim(q, start, qb, 1)
        qi_b = lax.dynamic_slice_in_dim(qi, start, qb, 1)
        wi_b = lax.dynamic_slice_in_dim(wi, start, qb, 1)
        qpos = start + jnp.arange(qb)
        idx = index_topk(qi_b, wi_b, ki, qpos, topk)
        return sparse_attend(q_b, gather_rows(k, idx), gather_rows(v, idx), idx <= qpos[None, :, None])

    o = lax.map(block, jnp.arange(L // qb))
    o = jnp.moveaxis(o, 0, 1).reshape(B, L, DSA_QD)
    return mm3(o, w_out), k, v, ki


def dsa_sample(x, ck, cv, cki, page_table, w_in, w_out):
    B, L, _ = x.shape
    past = page_table.shape[1] * PAGE_SIZE
    pos = past + jnp.arange(L)
    q, k, v, qi, ki, wi = dsa_project(x, pos, w_in)
    ki_past = cki[page_table].reshape(B, past, IDX_DIM)
    ki_all = jnp.concatenate([ki_past, ki], axis=1)
    topk = min(TOPK_MAX, (past + L) // 4)
    idx = index_topk(qi, wi, ki_all, pos, topk)
    from_past = (idx < past)[..., None, None]
    pidx = jnp.minimum(idx, past - 1)
    phys_page = jnp.take_along_axis(page_table, (pidx // PAGE_SIZE).reshape(B, -1), axis=1).reshape(pidx.shape)
    phys = phys_page * PAGE_SIZE + pidx % PAGE_SIZE
    nidx = jnp.clip(idx - past, 0, L - 1)
    ck_flat = ck.reshape(-1, N_KV_HEADS, HEAD_DIM)
    cv_flat = cv.reshape(-1, N_KV_HEADS, HEAD_DIM)
    k_sel = jnp.where(from_past, ck_flat[phys], gather_rows(k, nidx))
    v_sel = jnp.where(from_past, cv_flat[phys], gather_rows(v, nidx))
    o = sparse_attend(q, k_sel, v_sel, idx <= pos[None, :, None])
    return mm3(o, w_out), k, v, ki


def _expert_kernel(be_ref, x_ref, w1_ref, w3_ref, w2_ref, o_ref):
    del be_ref
    x = x_ref[...].astype(BF16)
    h1 = jnp.dot(x, w1_ref[0].astype(BF16), preferred_element_type=F32)
    h3 = jnp.dot(x, w3_ref[0].astype(BF16), preferred_element_type=F32)
    h = (h1 * jax.nn.sigmoid(h1)) * h3
    o_ref[...] = jnp.dot(h.astype(BF16), w2_ref[0].astype(BF16), preferred_element_type=F32)


def grouped_experts(xt, eidx, gate, w1, w3, w2):
    T, D = xt.shape
    K = eidx.shape[1]
    A = T * K
    E = w1.shape[0]
    blk = max(8, min(128, A // E))
    nb = A // blk + E
    flat_e = eidx.reshape(-1)
    order = jnp.argsort(flat_e)
    se = flat_e[order]
    tok = order // K
    counts = jnp.bincount(flat_e, length=E)
    padded = (counts + blk - 1) // blk * blk
    pend = jnp.cumsum(padded)
    pstart = pend - padded
    cstart = jnp.cumsum(counts) - counts
    dest = pstart[se] + jnp.arange(A) - cstart[se]
    slot_tok = jnp.full((nb * blk,), T, jnp.int32).at[dest].set(tok.astype(jnp.int32))
    blk_e = jnp.minimum(jnp.searchsorted(pend, jnp.arange(nb) * blk, side='right'), E - 1).astype(jnp.int32)
    xpad = jnp.concatenate([xt, jnp.zeros((1, D), xt.dtype)], axis=0)
    xb = xpad[slot_tok]

    yb = pl.pallas_call(
        _expert_kernel,
        grid_spec=pltpu.PrefetchScalarGridSpec(
            num_scalar_prefetch=1,
            grid=(nb,),
            in_specs=[pl.BlockSpec((blk, D), lambda i, be: (i, 0)),
                      pl.BlockSpec((1, D, D_EXPERT), lambda i, be: (be[i], 0, 0)),
                      pl.BlockSpec((1, D, D_EXPERT), lambda i, be: (be[i], 0, 0)),
                      pl.BlockSpec((1, D_EXPERT, D), lambda i, be: (be[i], 0, 0))],
            out_specs=pl.BlockSpec((blk, D), lambda i, be: (i, 0))),
        out_shape=jax.ShapeDtypeStruct((nb * blk, D), F32),
        compiler_params=pltpu.CompilerParams(
            dimension_semantics=("arbitrary",), vmem_limit_bytes=VMEM_LIMIT),
    )(blk_e, xb, w1, w3, w2)
    y_assign = yb[dest] * gate.reshape(-1)[order][:, None]
    return jnp.zeros((T, D), yb.dtype).at[tok].add(y_assign)


def hier_moe(x, wg, bg, we, be, w1, w3, w2):
    B, L, D = x.shape
    xt = x.reshape(-1, D)
    T = xt.shape[0]
    lg = jnp.dot(xt, wg, precision=lax.Precision.HIGHEST) + bg
    pg = jax.nn.softmax(lg, axis=-1)
    gsel = jnp.argmax(lg, axis=-1)
    le = (jnp.dot(xt, we, precision=lax.Precision.HIGHEST) + be).reshape(T, N_GROUPS, EXPERTS_PER_GROUP)
    le_g = jnp.take_along_axis(le, gsel[:, None, None], axis=1)[:, 0]
    pe = jax.nn.softmax(le_g, axis=-1)
    top_p, top_i = lax.top_k(pe, TOPK_EXPERTS)
    gate = top_p / jnp.sum(top_p, -1, keepdims=True) * jnp.take_along_axis(pg, gsel[:, None], axis=1)
    eidx = gsel[:, None] * EXPERTS_PER_GROUP + top_i
    return grouped_experts(xt, eidx, gate, w1, w3, w2).reshape(B, L, D)


def kernel(x_prompt, x_sample, state_gdn_s, state_gdn_conv, cache_k, cache_v, cache_kidx, page_table,
           gdn_w_in, gdn_conv_w, gdn_a_log, gdn_dt_bias, gdn_norm_w, gdn_w_out,
           dsa_w_in, dsa_w_out, ln1_g, ln1_b, ln2_g, ln2_b,
           moe_wg, moe_bg, moe_we, moe_be, moe_w1, moe_w3, moe_w2):
    xp, xs = x_prompt, x_sample
    p_s, p_c, s_s, s_c = [], [], [], []
    p_k, p_v, p_ki, s_k, s_v, s_ki = [], [], [], [], [], []
    for i in range(DEPTH):
        j = i // N_MIXERS
        if i % N_MIXERS == 0:
            gp = (gdn_w_in[j], gdn_conv_w[j], gdn_a_log[j], gdn_dt_bias[j], gdn_norm_w[j], gdn_w_out[j])
            s0 = jnp.zeros((xp.shape[0], GDN_HV, GDN_DK, GDN_DV), F32)
            c0 = jnp.zeros((xp.shape[0], CONV_W - 1, GDN_CONV_DIM), xp.dtype)
            yp, sp, cp = gdn_mixer(xp, s0, c0, *gp)
            ys, ss, cs = gdn_mixer(xs, state_gdn_s[j], state_gdn_conv[j], *gp)
            p_s.append(sp); p_c.append(cp); s_s.append(ss); s_c.append(cs)
        else:
            yp, kp, vp, kip = dsa_prompt(xp, dsa_w_in[j], dsa_w_out[j])
            ys, kn, vn, kin = dsa_sample(xs, cache_k[j], cache_v[j], cache_kidx[j], page_table,
                                         dsa_w_in[j], dsa_w_out[j])
            p_k.append(kp); p_v.append(vp); p_ki.append(kip)
            s_k.append(kn); s_v.append(vn); s_ki.append(kin)
        xp = layer_norm(ALPHA * xp + yp, ln1_g[i], ln1_b[i])
        xs = layer_norm(ALPHA * xs + ys, ln1_g[i], ln1_b[i])
        mp = (moe_wg[i], moe_bg[i], moe_we[i], moe_be[i], moe_w1[i], moe_w3[i], moe_w2[i])
        xp = layer_norm(ALPHA * xp + hier_moe(xp, *mp), ln2_g[i], ln2_b[i])
        xs = layer_norm(ALPHA * xs + hier_moe(xs, *mp), ln2_g[i], ln2_b[i])
    return (xp, xs, jnp.stack(p_s), jnp.stack(p_c), jnp.stack(p_k), jnp.stack(p_v), jnp.stack(p_ki),
            jnp.stack(s_s), jnp.stack(s_c), jnp.stack(s_k), jnp.stack(s_v), jnp.stack(s_ki))
```

```python
import functools

import jax
import jax.numpy as jnp
from jax import lax
from jax.experimental import pallas as pl
from jax.experimental.pallas import tpu as pltpu

D_MODEL = 2048
DEPTH = 4
PAGE_SIZE = 128
N_MIXERS = 2
GDN_DK = 128
GDN_DV = 128
GDN_HK = D_MODEL // GDN_DK
GDN_HV = 2 * GDN_HK
GDN_KD = GDN_HK * GDN_DK
GDN_VD = GDN_HV * GDN_DV
GDN_CONV_DIM = 2 * GDN_KD + GDN_VD
GDN_MAIN = GDN_CONV_DIM + GDN_VD
CONV_W = 4
GDN_CHUNK = 64
HEAD_DIM = 128
N_HEADS = D_MODEL // HEAD_DIM
N_KV_HEADS = 4
KV_GROUP = N_HEADS // N_KV_HEADS
IDX_HEADS = 16
IDX_DIM = 128
DSA_QD = N_HEADS * HEAD_DIM
DSA_KVD = N_KV_HEADS * HEAD_DIM
DSA_MAIN = DSA_QD + 2 * DSA_KVD + IDX_HEADS * IDX_DIM + IDX_DIM
TOPK_MAX = 256
ROPE_THETA = 10000.0
N_GROUPS = 4
EXPERTS_PER_GROUP = 8
N_EXPERTS = N_GROUPS * EXPERTS_PER_GROUP
TOPK_EXPERTS = 2
D_EXPERT = D_MODEL // 4
ALPHA = (2 * DEPTH) ** 0.25
LN_EPS = 1e-5
NORM_EPS = 1e-6

LANES = 128
F32 = jnp.float32
BF16 = jnp.bfloat16
I32 = jnp.int32
VMEM_LIMIT = 56 * 1024 * 1024
INT_MIN = -2 ** 31
NEG_INF = float("-inf")


def _cparams(*sem):
    return pltpu.CompilerParams(dimension_semantics=sem, vmem_limit_bytes=VMEM_LIMIT)


def _bdot(a, b):
    return jnp.dot(a.astype(BF16), b.astype(BF16), preferred_element_type=F32)


def _bdot_nt(a, b):
    return lax.dot_general(a.astype(BF16), b.astype(BF16), (((1,), (1,)), ((), ())),
                           preferred_element_type=F32)


def _row_tile(T, cap):
    if T <= cap:
        return T
    best = None
    for t in range(16, cap + 1, 16):
        if T % t == 0:
            best = t
    assert best is not None, T
    return best


def _proj_kernel(x_ref, w_ref, wt_ref, *rest, rope_ranges, tail_scale):
    if rope_ranges:
        cos_ref, sin_ref, o_ref, t_ref = rest
    else:
        o_ref, t_ref = rest
    j = pl.program_id(1)
    xb = x_ref[...].astype(BF16)
    acc = jnp.dot(xb, w_ref[...].astype(BF16), preferred_element_type=F32)
    if rope_ranges:
        roped = acc * cos_ref[...] + pltpu.roll(acc, HEAD_DIM // 2, 1) * sin_ref[...]
        is_rope = (j >= rope_ranges[0][0]) & (j < rope_ranges[0][1])
        for lo, hi in rope_ranges[1:]:
            is_rope = is_rope | ((j >= lo) & (j < hi))
        acc = jnp.where(is_rope, roped, acc)
    o_ref[...] = acc

    @pl.when(j == 0)
    def _():
        t_ref[...] = jnp.dot(xb, wt_ref[...].astype(BF16), preferred_element_type=F32) * tail_scale


def project(x, w, n_main, tn, tm_cap, cos=None, sin=None, rope_ranges=(), tail_scale=1.0):
    T, D = x.shape
    tm = _row_tile(T, tm_cap)
    n_tail = w.shape[1] - n_main
    w_tail = jnp.pad(w[:, n_main:], ((0, 0), (0, LANES - n_tail)))
    in_specs = [pl.BlockSpec((tm, D), lambda i, j: (i, 0)),
                pl.BlockSpec((D, tn), lambda i, j: (0, j)),
                pl.BlockSpec((D, LANES), lambda i, j: (0, 0))]
    args = [x, w, w_tail]
    if rope_ranges:
        assert tn == HEAD_DIM
        in_specs += [pl.BlockSpec((tm, LANES), lambda i, j: (i, 0))] * 2
        args += [cos, sin]
    return pl.pallas_call(
        functools.partial(_proj_kernel, rope_ranges=tuple(rope_ranges), tail_scale=tail_scale),
        grid=(T // tm, n_main // tn),
        in_specs=in_specs,
        out_specs=[pl.BlockSpec((tm, tn), lambda i, j: (i, j)),
                   pl.BlockSpec((tm, LANES), lambda i, j: (i, 0))],
        out_shape=[jax.ShapeDtypeStruct((T, n_main), F32), jax.ShapeDtypeStruct((T, LANES), F32)],
        compiler_params=_cparams("parallel", "arbitrary"),
    )(*args)


def _mm_res_ln_kernel(x_ref, w_ref, r_ref, g_ref, b_ref, o_ref, *, nk):
    k = pl.program_id(1)
    part = jnp.dot(x_ref[...].astype(BF16), w_ref[...].astype(BF16), preferred_element_type=F32)

    @pl.when(k == 0)
    def _():
        o_ref[...] = part

    @pl.when(k > 0)
    def _():
        o_ref[...] += part

    @pl.when(k == nk - 1)
    def _():
        h = ALPHA * r_ref[...] + o_ref[...]
        mu = jnp.mean(h, -1, keepdims=True)
        hc = h - mu
        var = jnp.mean(hc * hc, -1, keepdims=True)
        o_ref[...] = hc * lax.rsqrt(var + LN_EPS) * g_ref[...] + b_ref[...]


def matmul_res_ln(x, w, resid, g, b, tm_cap=688, tk=512):
    T, K = x.shape
    D = w.shape[1]
    tm = _row_tile(T, tm_cap)
    nk = K // tk
    return pl.pallas_call(
        functools.partial(_mm_res_ln_kernel, nk=nk),
        grid=(T // tm, nk),
        in_specs=[pl.BlockSpec((tm, tk), lambda i, k: (i, k)),
                  pl.BlockSpec((tk, D), lambda i, k: (k, 0)),
                  pl.BlockSpec((tm, D), lambda i, k: (i, 0)),
                  pl.BlockSpec((1, D), lambda i, k: (0, 0)),
                  pl.BlockSpec((1, D), lambda i, k: (0, 0))],
        out_specs=pl.BlockSpec((tm, D), lambda i, k: (i, 0)),
        out_shape=jax.ShapeDtypeStruct((T, D), F32),
        compiler_params=_cparams("parallel", "arbitrary"),
    )(x, w, resid, g.reshape(1, D), b.reshape(1, D))


def _dsa_prompt_kernel(q_ref, qi0_ref, qi1_ref, wi_ref, k_ref, v_ref, ki_ref, o_ref,
                       kbf, vbf, kibf, key_s, bias_s, *, tq, seq, topk, s_step):
    i = pl.program_id(1)

    @pl.when(i == 0)
    def _():
        kbf[...] = k_ref[...].astype(BF16)
        vbf[...] = v_ref[...].astype(BF16)
        kibf[...] = ki_ref[...].astype(BF16)

    half = IDX_HEADS // 2
    qi_rows = [qi0_ref[:, h * IDX_DIM:(h + 1) * IDX_DIM].astype(BF16) for h in range(half)]
    qi_rows += [qi1_ref[:, h * IDX_DIM:(h + 1) * IDX_DIM].astype(BF16) for h in range(half)]
    qi_stack = jnp.concatenate(qi_rows, axis=0)
    wib = wi_ref[...].astype(BF16).astype(F32)
    q_rows = [jnp.concatenate([q_ref[:, (n * KV_GROUP + g) * HEAD_DIM:(n * KV_GROUP + g + 1) * HEAD_DIM]
                               for g in range(KV_GROUP)], axis=0).astype(BF16)
              for n in range(N_KV_HEADS)]

    def body(S):
        qpos = i * tq + lax.broadcasted_iota(I32, (tq, 1), 0)
        for c0 in range(0, S, s_step):
            s = _bdot_nt(qi_stack, kibf[c0:c0 + s_step, :])
            r = jnp.maximum(s, 0.0).astype(BF16).astype(F32)
            score = r[0:tq] * wib[:, 0:1]
            for h in range(1, IDX_HEADS):
                score = score + r[h * tq:(h + 1) * tq] * wib[:, h:h + 1]
            score = score + 0.0
            bits = pltpu.bitcast(score, I32)
            key = jnp.where(bits < 0, bits ^ jnp.int32(0x7FFFFFFF), bits)
            spos = c0 + lax.broadcasted_iota(I32, (tq, s_step), 1)
            key_s[:, c0:c0 + s_step] = jnp.where(spos <= qpos, key, jnp.int32(INT_MIN))

        def count_ge(cand):
            return jnp.sum((key_s[:, 0:S] >= cand).astype(I32), axis=1, keepdims=True)

        t0 = jnp.where(count_ge(jnp.zeros((tq, 1), I32)) >= topk, jnp.int32(0), jnp.int32(INT_MIN))
        t0 = jnp.broadcast_to(t0, (tq, 1))

        def bit_step(it, t):
            cand = t | jnp.left_shift(jnp.int32(1), 30 - it)
            return jnp.where(count_ge(cand) >= topk, cand, t)

        thr = lax.fori_loop(0, 31, bit_step, t0)

        keyv = key_s[:, 0:S]
        valid = lax.broadcasted_iota(I32, (tq, S), 1) <= qpos
        ge = keyv >= thr
        n_ge = jnp.sum((ge & valid).astype(I32), axis=1, keepdims=True)
        has_tie = jnp.max(n_ge) > topk
        bias_s[:, 0:S] = jnp.where(ge & valid, 0.0, NEG_INF)

        @pl.when(has_tie)
        def _():
            gt = keyv > thr
            n_gt = jnp.sum((gt & valid).astype(I32), axis=1, keepdims=True)
            room = (topk - n_gt).astype(F32)
            eq = ((keyv == thr) & valid)
            tri = (lax.broadcasted_iota(I32, (LANES, LANES), 0)
                   < lax.broadcasted_iota(I32, (LANES, LANES), 1)).astype(BF16)
            carry = jnp.zeros((tq, 1), F32)
            for c0 in range(0, S, LANES):
                eqc = eq[:, c0:c0 + LANES]
                before = carry + jnp.dot(eqc.astype(BF16), tri, preferred_element_type=F32)
                keep = (gt[:, c0:c0 + LANES] & valid[:, c0:c0 + LANES]) | (eqc & (before < room))
                bias_s[:, c0:c0 + LANES] = jnp.where(keep, 0.0, NEG_INF)
                carry = carry + jnp.sum(eqc.astype(F32), axis=1, keepdims=True)

        bias = bias_s[:, 0:S]
        for n in range(N_KV_HEADS):
            s = _bdot_nt(q_rows[n], kbf[0:S, n * HEAD_DIM:(n + 1) * HEAD_DIM]) * (HEAD_DIM ** -0.5)
            s = s.reshape(KV_GROUP, tq, S) + bias[None]
            m = jnp.max(s, axis=-1, keepdims=True)
            p = jnp.exp(s - m)
            l = jnp.sum(p, axis=-1, keepdims=True)
            o = jnp.dot(p.reshape(KV_GROUP * tq, S).astype(BF16), vbf[0:S, n * HEAD_DIM:(n + 1) * HEAD_DIM],
                        preferred_element_type=F32)
            o = o.reshape(KV_GROUP, tq, HEAD_DIM) / l
            for g in range(KV_GROUP):
                h = n * KV_GROUP + g
                o_ref[:, h * HEAD_DIM:(h + 1) * HEAD_DIM] = o[g].astype(o_ref.dtype)

    n_var = seq // s_step
    per = (seq // tq) // n_var
    for c in range(n_var):
        @pl.when(i // per == c)
        def _(c=c):
            body((c + 1) * s_step)


def dsa_prompt_attend(pp, wi, batch, seq, tq=128, s_step=512):
    topk = min(TOPK_MAX, seq // 4)
    s_step = min(s_step, seq)
    nq = seq // tq
    kcol = DSA_QD // DSA_KVD
    qicol = (DSA_QD + 2 * DSA_KVD) // (IDX_HEADS * IDX_DIM // 2)
    kicol = (DSA_QD + 2 * DSA_KVD + IDX_HEADS * IDX_DIM) // IDX_DIM
    assert (DSA_QD + 2 * DSA_KVD) % (IDX_HEADS * IDX_DIM // 2) == 0
    hq = IDX_HEADS * IDX_DIM // 2
    return pl.pallas_call(
        functools.partial(_dsa_prompt_kernel, tq=tq, seq=seq, topk=topk, s_step=s_step),
        grid=(batch, nq),
        in_specs=[pl.BlockSpec((tq, DSA_QD), lambda b, i: (b * nq + i, 0)),
                  pl.BlockSpec((tq, hq), lambda b, i: (b * nq + i, qicol)),
                  pl.BlockSpec((tq, hq), lambda b, i: (b * nq + i, qicol + 1)),
                  pl.BlockSpec((tq, LANES), lambda b, i: (b * nq + i, 0)),
                  pl.BlockSpec((seq, DSA_KVD), lambda b, i: (b, kcol)),
                  pl.BlockSpec((seq, DSA_KVD), lambda b, i: (b, kcol + 1)),
                  pl.BlockSpec((seq, IDX_DIM), lambda b, i: (b, kicol))],
        out_specs=pl.BlockSpec((tq, DSA_QD), lambda b, i: (b * nq + i, 0)),
        out_shape=jax.ShapeDtypeStruct((batch * seq, DSA_QD), BF16),
        scratch_shapes=[pltpu.VMEM((seq, DSA_KVD), BF16), pltpu.VMEM((seq, DSA_KVD), BF16),
                        pltpu.VMEM((seq, IDX_DIM), BF16),
                        pltpu.VMEM((tq, seq), I32), pltpu.VMEM((tq, seq), F32)],
        compiler_params=_cparams("parallel", "arbitrary"),
    )(pp, pp, pp, wi, pp, pp, pp)


def _split3(a):
    hi = a.astype(BF16)
    lo = (a - hi.astype(F32)).astype(BF16)
    return hi, lo


def _dot3(a, b):
    ah, al = _split3(a)
    bh, bl = _split3(b)
    d = functools.partial(jnp.dot, preferred_element_type=F32)
    return d(ah, bh) + (d(ah, bl) + d(al, bh))


def _tri_inverse(a):
    c = a.shape[0]
    eye = (lax.broadcasted_iota(I32, (c, c), 0) == lax.broadcasted_iota(I32, (c, c), 1)).astype(F32)
    x = -a
    p = eye + x
    steps = max(0, (c - 1).bit_length() - 1)
    for _ in range(steps):
        x = _dot3(x, x)
        p = p + _dot3(p, x)
    return p


def _silu(x):
    return x * jax.nn.sigmoid(x)


def _gdn_kernel(xq_ref, xk_ref, xv_ref, z_ref, tail_ref, cq_ref, ck_ref, cv_ref, wq_ref, wk_ref, wv_ref,
                alog_ref, dtb_ref, nw_ref, s0_ref, o_ref, s_ref,
                pq, pk, pv, beta_s, g_s, u_s, w_s, qk_s, qg_s, kd_s, el_s, *, seq, chunk):
    hk = pl.program_id(1)
    C = chunk
    N = seq // C
    HALO = 8
    pq[0:HALO, :] = cq_ref[0]
    pk[0:HALO, :] = ck_ref[0]
    pv[0:HALO, :] = cv_ref[0]
    pq[HALO:HALO + seq, :] = xq_ref[...]
    pk[HALO:HALO + seq, :] = xk_ref[...]
    pv[HALO:HALO + seq, :] = xv_ref[...]
    tail = tail_ref[...]
    beta_s[...] = jax.nn.sigmoid(tail)
    x = tail + dtb_ref[...]
    softplus = jnp.maximum(x, 0.0) + jnp.log1p(jnp.exp(-jnp.abs(x)))
    g_s[...] = -jnp.exp(alog_ref[...]) * softplus

    row = lax.broadcasted_iota(I32, (C, C), 0)
    col = lax.broadcasted_iota(I32, (C, C), 1)
    incl = row >= col
    strict = row > col
    lane = lax.broadcasted_iota(I32, (C, LANES), 1)
    sub_t = lax.broadcasted_iota(I32, (LANES, C), 0)
    rowc = lax.broadcasted_iota(I32, (C, LANES), 0)

    def conv(pref, wref, r0, width):
        win = pref[pl.ds(r0, C + HALO), :]
        acc = win[HALO - 3:HALO - 3 + C] * wref[0:1, :]
        for j in range(1, CONV_W):
            acc = acc + win[HALO - 3 + j:HALO - 3 + j + C] * wref[j:j + 1, :]
        return _silu(acc)

    def l2n(t):
        return t * lax.rsqrt(jnp.sum(t * t, -1, keepdims=True) + NORM_EPS)

    def prep(c, carry):
        r0 = pl.multiple_of(c * C, C)
        qn = l2n(conv(pq, wq_ref, r0, GDN_DK)) * (GDN_DK ** -0.5)
        kn = l2n(conv(pk, wk_ref, r0, GDN_DK))
        vv = conv(pv, wv_ref, r0, 2 * GDN_DV)
        beta = beta_s[pl.ds(r0, C), :]
        gc = g_s[pl.ds(r0, C), :]
        sh = 1
        while sh < C:
            gc = gc + jnp.where(rowc >= sh, pltpu.roll(gc, sh, 0), 0.0)
            sh *= 2
        gc_t = gc.T
        qk_raw = _bdot_nt(qn, kn)
        for e in range(2):
            hv = 2 * hk + e
            bcol = jnp.sum(jnp.where(lane == hv, beta, 0.0), axis=1, keepdims=True)
            gcol = jnp.sum(jnp.where(lane == GDN_HV + hv, gc, 0.0), axis=1, keepdims=True)
            grow = jnp.sum(jnp.where(sub_t == GDN_HV + hv, gc_t, 0.0), axis=0, keepdims=True)
            glast = grow[:, C - 1:C]
            decay = jnp.exp(jnp.where(incl, gcol - grow, NEG_INF))
            kb = kn * bcol
            a = jnp.where(strict, _bdot_nt(kb, kn) * decay, 0.0)
            tmat = _tri_inverse(a)
            ve = vv[:, e * GDN_DV:(e + 1) * GDN_DV]
            u_s[e, pl.ds(r0, C), :] = _bdot(tmat, ve * bcol)
            w_s[e, pl.ds(r0, C), :] = _bdot(tmat, kb * jnp.exp(gcol))
            qk_s[e, c] = jnp.where(incl, qk_raw * decay, 0.0)
            qg_s[e, pl.ds(r0, C), :] = (qn * jnp.exp(gcol)).astype(BF16)
            kd = kn * jnp.exp(glast - gcol)
            kd_s[e, c] = kd.T.astype(BF16)
            el_s[e, c] = jnp.broadcast_to(jnp.exp(glast), (8, LANES))
        return carry

    lax.fori_loop(0, N, prep, 0)

    nw = nw_ref[...]

    def scan(c, states):
        r0 = pl.multiple_of(c * C, C)
        new_states = []
        for e in range(2):
            st = states[e]
            sb = st.astype(BF16)
            v_new = u_s[e, pl.ds(r0, C), :] - jnp.dot(w_s[e, pl.ds(r0, C), :].astype(BF16), sb,
                                                      preferred_element_type=F32)
            vb = v_new.astype(BF16)
            o = (jnp.dot(qg_s[e, pl.ds(r0, C), :], sb, preferred_element_type=F32)
                 + jnp.dot(qk_s[e, c].astype(BF16), vb, preferred_element_type=F32))
            st = st * el_s[e, c][0:1, :] + jnp.dot(kd_s[e, c], vb, preferred_element_type=F32)
            zf = z_ref[pl.ds(r0, C), e * GDN_DV:(e + 1) * GDN_DV]
            og = o * lax.rsqrt(jnp.mean(o * o, -1, keepdims=True) + NORM_EPS) * nw * _silu(zf)
            o_ref[pl.ds(r0, C), e * GDN_DV:(e + 1) * GDN_DV] = og.astype(o_ref.dtype)
            new_states.append(st)
        return tuple(new_states)

    s_fin = lax.fori_loop(0, N, scan, (s0_ref[0, 0], s0_ref[0, 1]))
    s_ref[0, 0] = s_fin[0]
    s_ref[0, 1] = s_fin[1]


def gdn_prompt_core(main, tail, conv0, conv_w, a_log, dt_bias, norm_w, s0, batch, seq):
    C = min(GDN_CHUNK, seq)
    N = seq // C
    assert seq % C == 0
    conv0p = jnp.pad(conv0, ((0, 0), (8 - (CONV_W - 1), 0), (0, 0)))
    alog = jnp.pad(a_log, (GDN_HV, LANES - 2 * GDN_HV)).reshape(1, LANES)
    dtb = jnp.pad(dt_bias, (GDN_HV, LANES - 2 * GDN_HV)).reshape(1, LANES)
    vblk = GDN_KD * 2 // (2 * GDN_DV)
    zblk = GDN_CONV_DIM // (2 * GDN_DV)
    kblk = GDN_KD // GDN_DK
    return pl.pallas_call(
        functools.partial(_gdn_kernel, seq=seq, chunk=C),
        grid=(batch, GDN_HK),
        in_specs=[pl.BlockSpec((seq, GDN_DK), lambda b, h: (b, h)),
                  pl.BlockSpec((seq, GDN_DK), lambda b, h: (b, kblk + h)),
                  pl.BlockSpec((seq, 2 * GDN_DV), lambda b, h: (b, vblk + h)),
                  pl.BlockSpec((seq, 2 * GDN_DV), lambda b, h: (b, zblk + h)),
                  pl.BlockSpec((seq, LANES), lambda b, h: (b, 0)),
                  pl.BlockSpec((1, 8, GDN_DK), lambda b, h: (b, 0, h)),
                  pl.BlockSpec((1, 8, GDN_DK), lambda b, h: (b, 0, kblk + h)),
                  pl.BlockSpec((1, 8, 2 * GDN_DV), lambda b, h: (b, 0, vblk + h)),
                  pl.BlockSpec((CONV_W, GDN_DK), lambda b, h: (0, h)),
                  pl.BlockSpec((CONV_W, GDN_DK), lambda b, h: (0, kblk + h)),
                  pl.BlockSpec((CONV_W, 2 * GDN_DV), lambda b, h: (0, vblk + h)),
                  pl.BlockSpec((1, LANES), lambda b, h: (0, 0)),
                  pl.BlockSpec((1, LANES), lambda b, h: (0, 0)),
                  pl.BlockSpec((1, GDN_DV), lambda b, h: (0, 0)),
                  pl.BlockSpec((1, 2, GDN_DK, GDN_DV), lambda b, h: (b, h, 0, 0))],
        out_specs=[pl.BlockSpec((seq, 2 * GDN_DV), lambda b, h: (b, h)),
                   pl.BlockSpec((1, 2, GDN_DK, GDN_DV), lambda b, h: (b, h, 0, 0))],
        out_shape=[jax.ShapeDtypeStruct((batch * seq, GDN_VD), BF16),
                   jax.ShapeDtypeStruct((batch, GDN_HV, GDN_DK, GDN_DV), F32)],
        scratch_shapes=[pltpu.VMEM((seq + 8, GDN_DK), F32), pltpu.VMEM((seq + 8, GDN_DK), F32),
                        pltpu.VMEM((seq + 8, 2 * GDN_DV), F32),
                        pltpu.VMEM((seq, LANES), F32), pltpu.VMEM((seq, LANES), F32),
                        pltpu.VMEM((2, seq, GDN_DV), F32), pltpu.VMEM((2, seq, GDN_DK), F32),
                        pltpu.VMEM((2, N, C, C), F32), pltpu.VMEM((2, seq, GDN_DK), BF16),
                        pltpu.VMEM((2, N, GDN_DK, C), BF16), pltpu.VMEM((2, N, 8, LANES), F32)],
        compiler_params=_cparams("parallel", "parallel"),
    )(main, main, main, main, tail, conv0p, conv0p, conv0p, conv_w, conv_w, conv_w,
      alog, dtb, norm_w.reshape(1, GDN_DV), s0)


def l2norm(x):
    return x * lax.rsqrt(jnp.sum(x * x, -1, keepdims=True) + NORM_EPS)


def gdn_chunked(q, k, v, g, beta, s0):
    B, L, H, _ = q.shape
    DV = v.shape[-1]
    C = min(GDN_CHUNK, L)
    N = L // C

    def chunks(t):
        t = t.reshape((B, N, C, H) + t.shape[3:])
        return jnp.moveaxis(t, (1, 3), (0, 2))

    q, k, v, g, beta = chunks(q), chunks(k), chunks(v), chunks(g), chunks(beta)
    gc = jnp.cumsum(g, axis=-1)
    incl = jnp.tril(jnp.ones((C, C), bool))
    strict = jnp.tril(jnp.ones((C, C), bool), -1)
    decay = jnp.exp(jnp.where(incl, gc[..., :, None] - gc[..., None, :], -jnp.inf))
    kb = k * beta[..., None]
    vb = v * beta[..., None]
    lmat = jnp.where(strict, jnp.einsum('nbhcd,nbhed->nbhce', kb, k) * decay, 0.0)
    eye = jnp.eye(C, dtype=lmat.dtype)
    tmat = lax.linalg.triangular_solve(lmat + eye, jnp.broadcast_to(eye, lmat.shape),
                                       left_side=True, lower=True, unit_diagonal=True)
    u = jnp.einsum('nbhce,nbhed->nbhcd', tmat, vb)
    w = jnp.einsum('nbhce,nbhed->nbhcd', tmat, kb * jnp.exp(gc)[..., None])
    qk = jnp.einsum('nbhcd,nbhed->nbhce', q, k) * decay

    def step(s, xs):
        q_i, k_i, u_i, w_i, gc_i, qk_i = xs
        v_new = u_i - jnp.einsum('bhcd,bhde->bhce', w_i, s)
        o = (jnp.einsum('bhcd,bhde->bhce', q_i * jnp.exp(gc_i)[..., None], s)
             + jnp.einsum('bhcs,bhse->bhce', qk_i, v_new))
        g_last = gc_i[..., -1:]
        s = (s * jnp.exp(g_last)[..., None]
             + jnp.einsum('bhcd,bhce->bhde', k_i * jnp.exp(g_last - gc_i)[..., None], v_new))
        return s, o

    s, o = lax.scan(step, s0, (q, k, u, w, gc, qk))
    o = jnp.moveaxis(o, (0, 2), (1, 3)).reshape(B, N * C, H, DV)
    return o, s


def gdn_sample(main, tail, s0, conv0, conv_w, a_log, dt_bias, norm_w):
    B, L = conv0.shape[0], main.shape[0] // conv0.shape[0]
    main = main.reshape(B, L, GDN_MAIN)
    tail = tail.reshape(B, L, LANES)
    qkv = main[..., :GDN_CONV_DIM]
    z = main[..., GDN_CONV_DIM:]
    b = tail[..., :GDN_HV]
    a = tail[..., GDN_HV:2 * GDN_HV]
    xc = jnp.concatenate([conv0, qkv], axis=1)
    conv = xc[:, 0:L] * conv_w[0]
    for j in range(1, CONV_W):
        conv = conv + xc[:, j:j + L] * conv_w[j]
    new_conv = xc[:, L:]
    conv = jax.nn.silu(conv)
    q = conv[..., :GDN_KD].reshape(B, L, GDN_HK, GDN_DK)
    k = conv[..., GDN_KD:2 * GDN_KD].reshape(B, L, GDN_HK, GDN_DK)
    v = conv[..., 2 * GDN_KD:].reshape(B, L, GDN_HV, GDN_DV)
    rep = GDN_HV // GDN_HK
    q = jnp.repeat(l2norm(q) * (GDN_DK ** -0.5), rep, axis=2)
    k = jnp.repeat(l2norm(k), rep, axis=2)
    beta = jax.nn.sigmoid(b)
    g = -jnp.exp(a_log) * jax.nn.softplus(a + dt_bias)
    o, s = gdn_chunked(q, k, v, g, beta, s0)
    zf = z.reshape(B, L, GDN_HV, GDN_DV)
    o = o * lax.rsqrt(jnp.mean(o * o, -1, keepdims=True) + NORM_EPS) * norm_w * jax.nn.silu(zf)
    return o.reshape(B * L, GDN_VD).astype(BF16), s, new_conv


def index_topk(qi, wi, ki, qpos, topk):
    s = jnp.einsum('bqhd,bsd->bqhs', qi, ki)
    score = jnp.einsum('bqhs,bqh->bqs', jax.nn.relu(s), wi)
    valid = jnp.arange(ki.shape[1])[None, :] <= qpos[:, None]
    score = jnp.where(valid[None], score, -jnp.inf)
    _, idx = lax.top_k(score, topk)
    return idx


def sparse_attend(q, k_sel, v_sel, sel_valid):
    B, Q = q.shape[:2]
    qg = q.reshape(B, Q, N_KV_HEADS, KV_GROUP, HEAD_DIM)
    s = jnp.einsum('bqngd,bqknd->bqngk', qg, k_sel) * (HEAD_DIM ** -0.5)
    s = jnp.where(sel_valid[:, :, None, None, :], s, -jnp.inf)
    p = jax.nn.softmax(s, axis=-1)
    o = jnp.einsum('bqngk,bqknd->bqngd', p, v_sel)
    return o.reshape(B, Q, DSA_QD)


def gather_rows(t, idx):
    return jax.vmap(lambda tb, ib: tb[ib])(t, idx)


def dsa_sample(pp, wi, ck, cv, cki, page_table):
    B = page_table.shape[0]
    L = pp.shape[0] // B
    past = page_table.shape[1] * PAGE_SIZE
    pos = past + jnp.arange(L)
    o1, o2, o3, o4 = DSA_QD, DSA_QD + DSA_KVD, DSA_QD + 2 * DSA_KVD, DSA_QD + 2 * DSA_KVD + IDX_HEADS * IDX_DIM
    pp = pp.reshape(B, L, DSA_MAIN)
    q = pp[..., :o1].reshape(B, L, N_HEADS, HEAD_DIM)
    k = pp[..., o1:o2].reshape(B, L, N_KV_HEADS, HEAD_DIM)
    v = pp[..., o2:o3].reshape(B, L, N_KV_HEADS, HEAD_DIM)
    qi = pp[..., o3:o4].reshape(B, L, IDX_HEADS, IDX_DIM)
    ki = pp[..., o4:]
    wi = wi.reshape(B, L, LANES)[..., :IDX_HEADS]
    ki_past = cki[page_table].reshape(B, past, IDX_DIM)
    ki_all = jnp.concatenate([ki_past, ki], axis=1)
    topk = min(TOPK_MAX, (past + L) // 4)
    idx = index_topk(qi, wi, ki_all, pos, topk)
    from_past = (idx < past)[..., None, None]
    pidx = jnp.minimum(idx, past - 1)
    phys_page = jnp.take_along_axis(page_table, (pidx // PAGE_SIZE).reshape(B, -1), axis=1).reshape(pidx.shape)
    phys = phys_page * PAGE_SIZE + pidx % PAGE_SIZE
    nidx = jnp.clip(idx - past, 0, L - 1)
    ck_flat = ck.reshape(-1, N_KV_HEADS, HEAD_DIM)
    cv_flat = cv.reshape(-1, N_KV_HEADS, HEAD_DIM)
    k_sel = jnp.where(from_past, ck_flat[phys], gather_rows(k, nidx))
    v_sel = jnp.where(from_past, cv_flat[phys], gather_rows(v, nidx))
    o = sparse_attend(q, k_sel, v_sel, idx <= pos[None, :, None])
    return o.reshape(B * L, DSA_QD).astype(BF16), k, v, ki


def _expert_kernel(be_ref, x_ref, w1_ref, w3_ref, w2_ref, o_ref):
    del be_ref
    x = x_ref[...].astype(BF16)
    h1 = jnp.dot(x, w1_ref[0].astype(BF16), preferred_element_type=F32)
    h3 = jnp.dot(x, w3_ref[0].astype(BF16), preferred_element_type=F32)
    h = _silu(h1) * h3
    o_ref[...] = jnp.dot(h.astype(BF16), w2_ref[0].astype(BF16), preferred_element_type=F32)


def _combine_ln_kernel(x_ref, y0_ref, y1_ref, gate_ref, g_ref, b_ref, o_ref):
    gate = gate_ref[...]
    h = ALPHA * x_ref[...] + (y0_ref[...] * gate[:, 0:1] + y1_ref[...] * gate[:, 1:2])
    mu = jnp.mean(h, -1, keepdims=True)
    hc = h - mu
    var = jnp.mean(hc * hc, -1, keepdims=True)
    o_ref[...] = hc * lax.rsqrt(var + LN_EPS) * g_ref[...] + b_ref[...]


def moe_layer(xt, wg, bg, we, be, w1, w3, w2, ln_g, ln_b, blk=128):
    T, D = xt.shape
    E = N_EXPERTS
    K = TOPK_EXPERTS
    xb16 = xt.astype(BF16)
    lg = jnp.dot(xb16, wg.astype(BF16), preferred_element_type=F32) + bg
    pg = jax.nn.softmax(lg, axis=-1)
    gsel = jnp.argmax(lg, axis=-1)
    le = (jnp.dot(xb16, we.astype(BF16), preferred_element_type=F32) + be).reshape(T, N_GROUPS, EXPERTS_PER_GROUP)
    le_g = jnp.take_along_axis(le, gsel[:, None, None], axis=1)[:, 0]
    pe = jax.nn.softmax(le_g, axis=-1)
    top_p, top_i = lax.top_k(pe, K)
    gate = top_p / jnp.sum(top_p, -1, keepdims=True) * jnp.take_along_axis(pg, gsel[:, None], axis=1)
    eidx = (gsel[:, None] * EXPERTS_PER_GROUP + top_i).astype(I32)

    A = T * K
    nb = A // blk + E
    cnt = jnp.sum((eidx[:, :, None] == jnp.arange(E, dtype=I32)).astype(I32), axis=1)
    cum = jnp.cumsum(cnt, axis=0) - cnt
    counts = jnp.sum(cnt, axis=0)
    padded = (counts + blk - 1) // blk * blk
    pend = jnp.cumsum(padded)
    pstart = pend - padded
    dest = pstart[eidx] + jnp.take_along_axis(cum, eidx, axis=1)
    tok = jnp.broadcast_to(jnp.arange(T, dtype=I32)[:, None], (T, K))
    slot_tok = jnp.full((nb * blk,), T, I32).at[dest.reshape(-1)].set(tok.reshape(-1))
    blk_e = jnp.minimum(jnp.searchsorted(pend, jnp.arange(nb, dtype=I32) * blk, side='right'), E - 1).astype(I32)
    xpad = jnp.concatenate([xt, jnp.zeros((1, D), xt.dtype)], axis=0)
    xb = xpad[slot_tok]

    yb = pl.pallas_call(
        _expert_kernel,
        grid_spec=pltpu.PrefetchScalarGridSpec(
            num_scalar_prefetch=1,
            grid=(nb,),
            in_specs=[pl.BlockSpec((blk, D), lambda i, be_: (i, 0)),
                      pl.BlockSpec((1, D, D_EXPERT), lambda i, be_: (be_[i], 0, 0)),
                      pl.BlockSpec((1, D, D_EXPERT), lambda i, be_: (be_[i], 0, 0)),
                      pl.BlockSpec((1, D_EXPERT, D), lambda i, be_: (be_[i], 0, 0))],
            out_specs=pl.BlockSpec((blk, D), lambda i, be_: (i, 0))),
        out_shape=jax.ShapeDtypeStruct((nb * blk, D), F32),
        compiler_params=_cparams("arbitrary"),
    )(blk_e, xb, w1, w3, w2)

    y0 = yb[dest[:, 0]]
    y1 = yb[dest[:, 1]]
    tm = _row_tile(T, 344)
    return pl.pallas_call(
        _combine_ln_kernel,
        grid=(T // tm,),
        in_specs=[pl.BlockSpec((tm, D), lambda i: (i, 0)),
                  pl.BlockSpec((tm, D), lambda i: (i, 0)),
                  pl.BlockSpec((tm, D), lambda i: (i, 0)),
                  pl.BlockSpec((tm, K), lambda i: (i, 0)),
                  pl.BlockSpec((1, D), lambda i: (0, 0)),
                  pl.BlockSpec((1, D), lambda i: (0, 0))],
        out_specs=pl.BlockSpec((tm, D), lambda i: (i, 0)),
        out_shape=jax.ShapeDtypeStruct((T, D), F32),
        compiler_params=_cparams("parallel"),
    )(xt, y0, y1, gate, ln_g.reshape(1, D), ln_b.reshape(1, D))


def _rope_tables(pos):
    half = HEAD_DIM // 2
    inv = jnp.power(ROPE_THETA, -jnp.arange(half, dtype=F32) / half)
    ang = pos.astype(F32)[:, None] * inv[None, :]
    cos = jnp.cos(ang)
    sin = jnp.sin(ang)
    return jnp.concatenate([cos, cos], -1), jnp.concatenate([-sin, sin], -1)


def kernel(x_prompt, x_sample, state_gdn_s, state_gdn_conv, cache_k, cache_v, cache_kidx, page_table,
           gdn_w_in, gdn_conv_w, gdn_a_log, gdn_dt_bias, gdn_norm_w, gdn_w_out,
           dsa_w_in, dsa_w_out, ln1_g, ln1_b, ln2_g, ln2_b,
           moe_wg, moe_bg, moe_we, moe_be, moe_w1, moe_w3, moe_w2):
    B, L, D = x_prompt.shape
    BS, LS, _ = x_sample.shape
    TP = B * L
    past = page_table.shape[1] * PAGE_SIZE
    x = jnp.concatenate([x_prompt.reshape(TP, D), x_sample.reshape(BS * LS, D)], axis=0)
    pos = jnp.concatenate([jnp.tile(jnp.arange(L), B), jnp.tile(past + jnp.arange(LS), BS)])
    cos, sin = _rope_tables(pos)
    nq = DSA_QD // HEAD_DIM
    nkv = DSA_KVD // HEAD_DIM
    rope_ranges = ((0, nq + nkv), (nq + 2 * nkv, nq + 2 * nkv + IDX_HEADS + 1))
    wi_scale = IDX_HEADS ** -0.5 * IDX_DIM ** -0.5

    p_s, p_c, s_s, s_c = [], [], [], []
    p_k, p_v, p_ki, s_k, s_v, s_ki = [], [], [], [], [], []
    for i in range(DEPTH):
        j = i // N_MIXERS
        if i % N_MIXERS == 0:
            main, tail = project(x, gdn_w_in[j], GDN_MAIN, tn=512, tm_cap=688)
            c0 = jnp.zeros((B, CONV_W - 1, GDN_CONV_DIM), F32)
            s0 = jnp.zeros((B, GDN_HV, GDN_DK, GDN_DV), F32)
            op, sp = gdn_prompt_core(main, tail, c0, gdn_conv_w[j], gdn_a_log[j], gdn_dt_bias[j],
                                     gdn_norm_w[j], s0, B, L)
            cp = main[:TP, :GDN_CONV_DIM].reshape(B, L, GDN_CONV_DIM)[:, L - (CONV_W - 1):]
            os_, ss, cs = gdn_sample(main[TP:], tail[TP:], state_gdn_s[j], state_gdn_conv[j],
                                     gdn_conv_w[j], gdn_a_log[j], gdn_dt_bias[j], gdn_norm_w[j])
            p_s.append(sp); p_c.append(cp); s_s.append(ss); s_c.append(cs)
            w_out = gdn_w_out[j]
        else:
            pp, wi = project(x, dsa_w_in[j], DSA_MAIN, tn=HEAD_DIM, tm_cap=1376, cos=cos, sin=sin,
                             rope_ranges=rope_ranges, tail_scale=wi_scale)
            op = dsa_prompt_attend(pp, wi, B, L)
            os_, kn, vn, kin = dsa_sample(pp[TP:], wi[TP:], cache_k[j], cache_v[j], cache_kidx[j], page_table)
            ppp = pp[:TP].reshape(B, L, DSA_MAIN)
            p_k.append(ppp[..., DSA_QD:DSA_QD + DSA_KVD].reshape(B, L, N_KV_HEADS, HEAD_DIM))
            p_v.append(ppp[..., DSA_QD + DSA_KVD:DSA_QD + 2 * DSA_KVD].reshape(B, L, N_KV_HEADS, HEAD_DIM))
            p_ki.append(ppp[..., DSA_MAIN - IDX_DIM:])
            s_k.append(kn); s_v.append(vn); s_ki.append(kin)
            w_out = dsa_w_out[j]
        o_all = jnp.concatenate([op, os_], axis=0)
        x = matmul_res_ln(o_all, w_out, x, ln1_g[i], ln1_b[i])
        x = moe_layer(x, moe_wg[i], moe_bg[i], moe_we[i], moe_be[i], moe_w1[i], moe_w3[i], moe_w2[i],
                      ln2_g[i], ln2_b[i])
    xp = x[:TP].reshape(B, L, D)
    xs = x[TP:].reshape(BS, LS, D)
    return (xp, xs, jnp.stack(p_s), jnp.stack(p_c), jnp.stack(p_k), jnp.stack(p_v), jnp.stack(p_ki),
            jnp.stack(s_s), jnp.stack(s_c), jnp.stack(s_k), jnp.stack(s_v), jnp.stack(s_ki))
```

```python
import functools

import jax
import jax.numpy as jnp
from jax import lax
from jax.experimental import pallas as pl
from jax.experimental.pallas import tpu as pltpu

D_MODEL = 2048
DEPTH = 4
PAGE_SIZE = 128
N_MIXERS = 2
GDN_DK = 128
GDN_DV = 128
GDN_HK = D_MODEL // GDN_DK
GDN_HV = 2 * GDN_HK
GDN_KD = GDN_HK * GDN_DK
GDN_VD = GDN_HV * GDN_DV
GDN_CONV_DIM = 2 * GDN_KD + GDN_VD
GDN_MAIN = GDN_CONV_DIM + GDN_VD
CONV_W = 4
GDN_CHUNK = 64
HEAD_DIM = 128
N_HEADS = D_MODEL // HEAD_DIM
N_KV_HEADS = 4
KV_GROUP = N_HEADS // N_KV_HEADS
IDX_HEADS = 16
IDX_DIM = 128
DSA_QD = N_HEADS * HEAD_DIM
DSA_KVD = N_KV_HEADS * HEAD_DIM
DSA_MAIN = DSA_QD + 2 * DSA_KVD + IDX_HEADS * IDX_DIM + IDX_DIM
TOPK_MAX = 256
ROPE_THETA = 10000.0
N_GROUPS = 4
EXPERTS_PER_GROUP = 8
N_EXPERTS = N_GROUPS * EXPERTS_PER_GROUP
TOPK_EXPERTS = 2
D_EXPERT = D_MODEL // 4
ALPHA = (2 * DEPTH) ** 0.25
LN_EPS = 1e-5
NORM_EPS = 1e-6

LANES = 128
F32 = jnp.float32
BF16 = jnp.bfloat16
I32 = jnp.int32
VMEM_LIMIT = 56 * 1024 * 1024
INT_MIN = -2 ** 31
NEG_INF = float("-inf")


def _cparams(*sem):
    return pltpu.CompilerParams(dimension_semantics=sem, vmem_limit_bytes=VMEM_LIMIT)


def _bdot(a, b):
    return jnp.dot(a.astype(BF16), b.astype(BF16), preferred_element_type=F32)


def _bdot_nt(a, b):
    return lax.dot_general(a.astype(BF16), b.astype(BF16), (((1,), (1,)), ((), ())),
                           preferred_element_type=F32)


def _row_tile(T, cap):
    if T <= cap:
        return T
    best = None
    for t in range(16, cap + 1, 16):
        if T % t == 0:
            best = t
    assert best is not None, T
    return best


def _proj_kernel(x_ref, w_ref, wt_ref, *rest, rope_ranges, tail_scale):
    if rope_ranges:
        cos_ref, sin_ref, o_ref, t_ref = rest
    else:
        o_ref, t_ref = rest
    j = pl.program_id(1)
    xb = x_ref[...].astype(BF16)
    acc = jnp.dot(xb, w_ref[...].astype(BF16), preferred_element_type=F32)
    if rope_ranges:
        roped = acc * cos_ref[...] + pltpu.roll(acc, HEAD_DIM // 2, 1) * sin_ref[...]
        is_rope = (j >= rope_ranges[0][0]) & (j < rope_ranges[0][1])
        for lo, hi in rope_ranges[1:]:
            is_rope = is_rope | ((j >= lo) & (j < hi))
        acc = jnp.where(is_rope, roped, acc)
    o_ref[...] = acc

    @pl.when(j == 0)
    def _():
        t_ref[...] = jnp.dot(xb, wt_ref[...].astype(BF16), preferred_element_type=F32) * tail_scale


def project(x, w, layer, n_main, tn, tm_cap, cos=None, sin=None, rope_ranges=(), tail_scale=1.0):
    T, D = x.shape
    tm = _row_tile(T, tm_cap)
    n_tail = w.shape[2] - n_main
    w_tail = jnp.pad(w[layer, :, n_main:], ((0, 0), (0, LANES - n_tail)))
    in_specs = [pl.BlockSpec((tm, D), lambda i, j: (i, 0)),
                pl.BlockSpec((None, D, tn), lambda i, j: (layer, 0, j)),
                pl.BlockSpec((D, LANES), lambda i, j: (0, 0))]
    args = [x, w, w_tail]
    if rope_ranges:
        assert tn == HEAD_DIM
        in_specs += [pl.BlockSpec((tm, LANES), lambda i, j: (i, 0))] * 2
        args += [cos, sin]
    return pl.pallas_call(
        functools.partial(_proj_kernel, rope_ranges=tuple(rope_ranges), tail_scale=tail_scale),
        grid=(T // tm, n_main // tn),
        in_specs=in_specs,
        out_specs=[pl.BlockSpec((tm, tn), lambda i, j: (i, j)),
                   pl.BlockSpec((tm, LANES), lambda i, j: (i, 0))],
        out_shape=[jax.ShapeDtypeStruct((T, n_main), F32), jax.ShapeDtypeStruct((T, LANES), F32)],
        compiler_params=_cparams("parallel", "arbitrary"),
    )(*args)


def _mm_res_ln_kernel(x_ref, w_ref, r_ref, g_ref, b_ref, o_ref, *, nk):
    k = pl.program_id(1)
    part = jnp.dot(x_ref[...].astype(BF16), w_ref[...].astype(BF16), preferred_element_type=F32)

    @pl.when(k == 0)
    def _():
        o_ref[...] = part

    @pl.when(k > 0)
    def _():
        o_ref[...] += part

    @pl.when(k == nk - 1)
    def _():
        h = ALPHA * r_ref[...] + o_ref[...]
        mu = jnp.mean(h, -1, keepdims=True)
        hc = h - mu
        var = jnp.mean(hc * hc, -1, keepdims=True)
        o_ref[...] = hc * lax.rsqrt(var + LN_EPS) * g_ref[...] + b_ref[...]


def matmul_res_ln(x, w, layer, resid, g, b, tm_cap=688, tk=512):
    T, K = x.shape
    D = w.shape[2]
    tm = _row_tile(T, tm_cap)
    nk = K // tk
    return pl.pallas_call(
        functools.partial(_mm_res_ln_kernel, nk=nk),
        grid=(T // tm, nk),
        in_specs=[pl.BlockSpec((tm, tk), lambda i, k: (i, k)),
                  pl.BlockSpec((None, tk, D), lambda i, k: (layer, k, 0)),
                  pl.BlockSpec((tm, D), lambda i, k: (i, 0)),
                  pl.BlockSpec((1, D), lambda i, k: (0, 0)),
                  pl.BlockSpec((1, D), lambda i, k: (0, 0))],
        out_specs=pl.BlockSpec((tm, D), lambda i, k: (i, 0)),
        out_shape=jax.ShapeDtypeStruct((T, D), F32),
        compiler_params=_cparams("parallel", "arbitrary"),
    )(x, w, resid, g.reshape(1, D), b.reshape(1, D))


def _dsa_prompt_kernel(q_ref, qi0_ref, qi1_ref, wi_ref, k_ref, v_ref, ki_ref, o_ref,
                       kbf, vbf, kibf, key_s, bias_s, *, tq, seq, topk, s_step):
    i = pl.program_id(1)

    @pl.when(i == 0)
    def _():
        kbf[...] = k_ref[...].astype(BF16)
        vbf[...] = v_ref[...].astype(BF16)
        kibf[...] = ki_ref[...].astype(BF16)

    half = IDX_HEADS // 2
    qi_rows = [qi0_ref[:, h * IDX_DIM:(h + 1) * IDX_DIM].astype(BF16) for h in range(half)]
    qi_rows += [qi1_ref[:, h * IDX_DIM:(h + 1) * IDX_DIM].astype(BF16) for h in range(half)]
    qi_stack = jnp.concatenate(qi_rows, axis=0)
    wib = wi_ref[...].astype(BF16).astype(F32)
    q_rows = [jnp.concatenate([q_ref[:, (n * KV_GROUP + g) * HEAD_DIM:(n * KV_GROUP + g + 1) * HEAD_DIM]
                               for g in range(KV_GROUP)], axis=0).astype(BF16)
              for n in range(N_KV_HEADS)]

    def body(S):
        qpos = i * tq + lax.broadcasted_iota(I32, (tq, 1), 0)
        for c0 in range(0, S, s_step):
            s = _bdot_nt(qi_stack, kibf[c0:c0 + s_step, :])
            r = jnp.maximum(s, 0.0).astype(BF16).astype(F32)
            score = r[0:tq] * wib[:, 0:1]
            for h in range(1, IDX_HEADS):
                score = score + r[h * tq:(h + 1) * tq] * wib[:, h:h + 1]
            score = score + 0.0
            bits = pltpu.bitcast(score, I32)
            key = jnp.where(bits < 0, bits ^ jnp.int32(0x7FFFFFFF), bits)
            spos = c0 + lax.broadcasted_iota(I32, (tq, s_step), 1)
            key_s[:, c0:c0 + s_step] = jnp.where(spos <= qpos, key, jnp.int32(INT_MIN))

        def count_ge(cand):
            return jnp.sum((key_s[:, 0:S] >= cand).astype(I32), axis=1, keepdims=True)

        t0 = jnp.where(count_ge(jnp.zeros((tq, 1), I32)) >= topk, jnp.int32(0), jnp.int32(INT_MIN))
        t0 = jnp.broadcast_to(t0, (tq, 1))

        def bit_step(it, t):
            cand = t | jnp.left_shift(jnp.int32(1), 30 - it)
            return jnp.where(count_ge(cand) >= topk, cand, t)

        thr = lax.fori_loop(0, 31, bit_step, t0)

        keyv = key_s[:, 0:S]
        valid = lax.broadcasted_iota(I32, (tq, S), 1) <= qpos
        ge = keyv >= thr
        n_ge = jnp.sum((ge & valid).astype(I32), axis=1, keepdims=True)
        has_tie = jnp.max(n_ge) > topk
        bias_s[:, 0:S] = jnp.where(ge & valid, 0.0, NEG_INF)

        @pl.when(has_tie)
        def _():
            gt = keyv > thr
            n_gt = jnp.sum((gt & valid).astype(I32), axis=1, keepdims=True)
            room = (topk - n_gt).astype(F32)
            eq = ((keyv == thr) & valid)
            tri = (lax.broadcasted_iota(I32, (LANES, LANES), 0)
                   < lax.broadcasted_iota(I32, (LANES, LANES), 1)).astype(BF16)
            carry = jnp.zeros((tq, 1), F32)
            for c0 in range(0, S, LANES):
                eqc = eq[:, c0:c0 + LANES]
                before = carry + jnp.dot(eqc.astype(BF16), tri, preferred_element_type=F32)
                keep = (gt[:, c0:c0 + LANES] & valid[:, c0:c0 + LANES]) | (eqc & (before < room))
                bias_s[:, c0:c0 + LANES] = jnp.where(keep, 0.0, NEG_INF)
                carry = carry + jnp.sum(eqc.astype(F32), axis=1, keepdims=True)

        bias = bias_s[:, 0:S]
        for n in range(N_KV_HEADS):
            s = _bdot_nt(q_rows[n], kbf[0:S, n * HEAD_DIM:(n + 1) * HEAD_DIM]) * (HEAD_DIM ** -0.5)
            s = s.reshape(KV_GROUP, tq, S) + bias[None]
            m = jnp.max(s, axis=-1, keepdims=True)
            p = jnp.exp(s - m)
            l = jnp.sum(p, axis=-1, keepdims=True)
            o = jnp.dot(p.reshape(KV_GROUP * tq, S).astype(BF16), vbf[0:S, n * HEAD_DIM:(n + 1) * HEAD_DIM],
                        preferred_element_type=F32)
            o = o.reshape(KV_GROUP, tq, HEAD_DIM) / l
            for g in range(KV_GROUP):
                h = n * KV_GROUP + g
                o_ref[:, h * HEAD_DIM:(h + 1) * HEAD_DIM] = o[g].astype(o_ref.dtype)

    n_var = seq // s_step
    per = (seq // tq) // n_var
    for c in range(n_var):
        @pl.when(i // per == c)
        def _(c=c):
            body((c + 1) * s_step)


def dsa_prompt_attend(pp, wi, batch, seq, tq=128, s_step=512):
    topk = min(TOPK_MAX, seq // 4)
    s_step = min(s_step, seq)
    nq = seq // tq
    kcol = DSA_QD // DSA_KVD
    qicol = (DSA_QD + 2 * DSA_KVD) // (IDX_HEADS * IDX_DIM // 2)
    kicol = (DSA_QD + 2 * DSA_KVD + IDX_HEADS * IDX_DIM) // IDX_DIM
    assert (DSA_QD + 2 * DSA_KVD) % (IDX_HEADS * IDX_DIM // 2) == 0
    hq = IDX_HEADS * IDX_DIM // 2
    return pl.pallas_call(
        functools.partial(_dsa_prompt_kernel, tq=tq, seq=seq, topk=topk, s_step=s_step),
        grid=(batch, nq),
        in_specs=[pl.BlockSpec((tq, DSA_QD), lambda b, i: (b * nq + i, 0)),
                  pl.BlockSpec((tq, hq), lambda b, i: (b * nq + i, qicol)),
                  pl.BlockSpec((tq, hq), lambda b, i: (b * nq + i, qicol + 1)),
                  pl.BlockSpec((tq, LANES), lambda b, i: (b * nq + i, 0)),
                  pl.BlockSpec((seq, DSA_KVD), lambda b, i: (b, kcol)),
                  pl.BlockSpec((seq, DSA_KVD), lambda b, i: (b, kcol + 1)),
                  pl.BlockSpec((seq, IDX_DIM), lambda b, i: (b, kicol))],
        out_specs=pl.BlockSpec((tq, DSA_QD), lambda b, i: (b * nq + i, 0)),
        out_shape=jax.ShapeDtypeStruct((batch * seq, DSA_QD), BF16),
        scratch_shapes=[pltpu.VMEM((seq, DSA_KVD), BF16), pltpu.VMEM((seq, DSA_KVD), BF16),
                        pltpu.VMEM((seq, IDX_DIM), BF16),
                        pltpu.VMEM((tq, seq), I32), pltpu.VMEM((tq, seq), F32)],
        compiler_params=_cparams("parallel", "arbitrary"),
    )(pp, pp, pp, wi, pp, pp, pp)


def _split3(a):
    hi = a.astype(BF16)
    lo = (a - hi.astype(F32)).astype(BF16)
    return hi, lo


def _dot3(a_parts, b_parts):
    ah, al = a_parts
    bh, bl = b_parts
    d = functools.partial(jnp.dot, preferred_element_type=F32)
    return d(ah, bh) + (d(ah, bl) + d(al, bh))


def _tri_inverse(a, order):
    n = a.shape[0]
    eye = (lax.broadcasted_iota(I32, (n, n), 0) == lax.broadcasted_iota(I32, (n, n), 1)).astype(F32)
    x = -a
    p = eye + x
    steps = max(0, (order - 1).bit_length() - 1)
    xs = _split3(x)
    for _ in range(steps):
        xs = _split3(_dot3(xs, xs))
        p = p + _dot3(_split3(p), xs)
    return p


def _silu(x):
    return x * jax.nn.sigmoid(x)


def _gdn_kernel(xq_ref, xk_ref, xv_ref, z_ref, tail_ref, cq_ref, ck_ref, cv_ref, wq_ref, wk_ref, wv_ref,
                alog_ref, dtb_ref, nw_ref, s0_ref, o_ref, s_ref,
                beta_s, g_s, u_s, w_s, qk_s, qg_s, kd_s, el_s, *, seq, chunk, valid, hpb, unroll):
    hb = pl.program_id(1)
    C = chunk
    N = seq // C
    HALO = 8
    NV = 2 * hpb
    tail = tail_ref[...]
    beta = jax.nn.sigmoid(tail)
    x = tail + dtb_ref[...]
    softplus = jnp.maximum(x, 0.0) + jnp.log1p(jnp.exp(-jnp.abs(x)))
    g = -jnp.exp(alog_ref[...]) * softplus
    if valid < seq:
        is_real = lax.broadcasted_iota(I32, (seq, LANES), 0) < valid
        beta = jnp.where(is_real, beta, 0.0)
        g = jnp.where(is_real, g, 0.0)
    beta_s[...] = beta
    g_s[...] = g

    R = NV * C
    row = lax.broadcasted_iota(I32, (R, R), 0)
    col = lax.broadcasted_iota(I32, (R, R), 1)
    log2c = C.bit_length() - 1
    same_head = lax.shift_right_logical(row, log2c) == lax.shift_right_logical(col, log2c)
    incl = same_head & (row >= col)
    strict = same_head & (row > col)
    lane = lax.broadcasted_iota(I32, (C, LANES), 1)
    sub_t = lax.broadcasted_iota(I32, (LANES, C), 0)
    rowc = lax.broadcasted_iota(I32, (C, LANES), 0)

    def conv(xref, cref, wref, r0, c):
        prev = xref[pl.ds(pl.multiple_of(jnp.maximum(r0 - HALO, 0), HALO), HALO), :]
        win = jnp.concatenate([jnp.where(c == 0, cref[0], prev), xref[pl.ds(r0, C), :]], axis=0)
        acc = win[HALO - 3:HALO - 3 + C] * wref[0:1, :]
        for j in range(1, CONV_W):
            acc = acc + win[HALO - 3 + j:HALO - 3 + j + C] * wref[j:j + 1, :]
        return _silu(acc)

    def l2n(t):
        return t * lax.rsqrt(jnp.sum(t * t, -1, keepdims=True) + NORM_EPS)

    def prep_chunk(c):
        r0 = pl.multiple_of(c * C, C)
        qc = conv(xq_ref, cq_ref, wq_ref, r0, c)
        kc = conv(xk_ref, ck_ref, wk_ref, r0, c)
        vv = conv(xv_ref, cv_ref, wv_ref, r0, c)
        beta = beta_s[pl.ds(r0, C), :]
        gc = g_s[pl.ds(r0, C), :]
        sh = 1
        while sh < C:
            gc = gc + jnp.where(rowc >= sh, pltpu.roll(gc, sh, 0), 0.0)
            sh *= 2
        gc_t = gc.T
        qn = [l2n(qc[:, hl * GDN_DK:(hl + 1) * GDN_DK]) * (GDN_DK ** -0.5) for hl in range(hpb)]
        kn = [l2n(kc[:, hl * GDN_DK:(hl + 1) * GDN_DK]) for hl in range(hpb)]
        bcols, gcols, grows = [], [], []
        for e in range(NV):
            hv = NV * hb + e
            bcols.append(jnp.sum(jnp.where(lane == hv, beta, 0.0), axis=1, keepdims=True))
            gcols.append(jnp.sum(jnp.where(lane == GDN_HV + hv, gc, 0.0), axis=1, keepdims=True))
            grows.append(jnp.sum(jnp.where(sub_t == GDN_HV + hv, gc_t, 0.0), axis=0, keepdims=True))
        bcol = jnp.concatenate(bcols, axis=0)
        gcol = jnp.concatenate(gcols, axis=0)
        grow = jnp.concatenate(grows, axis=1)
        kn_st = jnp.concatenate([kn[e // 2] for e in range(NV)], axis=0)
        qn_st = jnp.concatenate([qn[e // 2] for e in range(NV)], axis=0)
        v_st = jnp.concatenate([vv[:, e * GDN_DV:(e + 1) * GDN_DV] for e in range(NV)], axis=0)
        decay = jnp.exp(jnp.where(incl, gcol - grow, NEG_INF))
        kb = kn_st * bcol
        a = jnp.where(strict, _bdot_nt(kb, kn_st) * decay, 0.0)
        tmat = _tri_inverse(a, C)
        u_s[c] = _bdot(tmat, v_st * bcol)
        w_s[c] = _bdot(tmat, kb * jnp.exp(gcol)).astype(BF16)
        qk_s[c] = jnp.where(incl, _bdot_nt(qn_st, kn_st) * decay, 0.0).astype(BF16)
        qg_s[c] = (qn_st * jnp.exp(gcol)).astype(BF16)
        for e in range(NV):
            glast = grows[e][:, C - 1:C]
            kd = kn[e // 2] * jnp.exp(glast - gcols[e])
            kd_s[e, c] = kd.T.astype(BF16)
            el_s[e, c] = jnp.broadcast_to(jnp.exp(glast), (8, LANES))

    def prep(it, carry):
        for u in range(unroll):
            prep_chunk(it * unroll + u)
        return carry

    lax.fori_loop(0, N // unroll, prep, 0)

    nw = nw_ref[...]
    s_ref[...] = s0_ref[...]

    def scan(c, carry):
        r0 = pl.multiple_of(c * C, C)
        d = functools.partial(jnp.dot, preferred_element_type=F32)
        u = u_s[c]
        w = w_s[c]
        qg = qg_s[c]
        sts = [s_ref[0, e] for e in range(NV)]
        sbs = [st.astype(BF16) for st in sts]
        vb = jnp.concatenate([u[e * C:(e + 1) * C] - d(w[e * C:(e + 1) * C], sbs[e]) for e in range(NV)],
                             axis=0).astype(BF16)
        o_intra = d(qk_s[c], vb)
        for e in range(NV):
            o = d(qg[e * C:(e + 1) * C], sbs[e]) + o_intra[e * C:(e + 1) * C]
            s_ref[0, e] = sts[e] * el_s[e, c][0:1, :] + d(kd_s[e, c], vb[e * C:(e + 1) * C])
            zf = z_ref[pl.ds(r0, C), e * GDN_DV:(e + 1) * GDN_DV]
            og = o * lax.rsqrt(jnp.mean(o * o, -1, keepdims=True) + NORM_EPS) * nw * _silu(zf)
            o_ref[pl.ds(r0, C), e * GDN_DV:(e + 1) * GDN_DV] = og.astype(o_ref.dtype)
        return carry

    lax.fori_loop(0, N, scan, 0)


def gdn_core(main, tail, conv0, conv_w, a_log, dt_bias, norm_w, s0, batch, seq, valid=None, hpb=2, unroll=2):
    valid = seq if valid is None else valid
    C = min(GDN_CHUNK, seq)
    N = seq // C
    assert seq % C == 0
    unroll = unroll if N % unroll == 0 else 1
    conv0p = jnp.pad(conv0, ((0, 0), (8 - (CONV_W - 1), 0), (0, 0)))
    alog = jnp.pad(a_log, (GDN_HV, LANES - 2 * GDN_HV)).reshape(1, LANES)
    dtb = jnp.pad(dt_bias, (GDN_HV, LANES - 2 * GDN_HV)).reshape(1, LANES)
    qw = GDN_DK * hpb
    vw = 2 * GDN_DV * hpb
    kblk = GDN_KD // qw
    vblk = 2 * GDN_KD // vw
    zblk = GDN_CONV_DIM // vw
    nv = 2 * hpb
    return pl.pallas_call(
        functools.partial(_gdn_kernel, seq=seq, chunk=C, valid=valid, hpb=hpb, unroll=unroll),
        grid=(batch, GDN_HK // hpb),
        in_specs=[pl.BlockSpec((seq, qw), lambda b, h: (b, h)),
                  pl.BlockSpec((seq, qw), lambda b, h: (b, kblk + h)),
                  pl.BlockSpec((seq, vw), lambda b, h: (b, vblk + h)),
                  pl.BlockSpec((seq, vw), lambda b, h: (b, zblk + h)),
                  pl.BlockSpec((seq, LANES), lambda b, h: (b, 0)),
                  pl.BlockSpec((1, 8, qw), lambda b, h: (b, 0, h)),
                  pl.BlockSpec((1, 8, qw), lambda b, h: (b, 0, kblk + h)),
                  pl.BlockSpec((1, 8, vw), lambda b, h: (b, 0, vblk + h)),
                  pl.BlockSpec((CONV_W, qw), lambda b, h: (0, h)),
                  pl.BlockSpec((CONV_W, qw), lambda b, h: (0, kblk + h)),
                  pl.BlockSpec((CONV_W, vw), lambda b, h: (0, vblk + h)),
                  pl.BlockSpec((1, LANES), lambda b, h: (0, 0)),
                  pl.BlockSpec((1, LANES), lambda b, h: (0, 0)),
                  pl.BlockSpec((1, GDN_DV), lambda b, h: (0, 0)),
                  pl.BlockSpec((1, nv, GDN_DK, GDN_DV), lambda b, h: (b, h, 0, 0))],
        out_specs=[pl.BlockSpec((seq, vw), lambda b, h: (b, h)),
                   pl.BlockSpec((1, nv, GDN_DK, GDN_DV), lambda b, h: (b, h, 0, 0))],
        out_shape=[jax.ShapeDtypeStruct((batch * seq, GDN_VD), BF16),
                   jax.ShapeDtypeStruct((batch, GDN_HV, GDN_DK, GDN_DV), F32)],
        scratch_shapes=[pltpu.VMEM((seq, LANES), F32), pltpu.VMEM((seq, LANES), F32),
                        pltpu.VMEM((N, nv * C, GDN_DV), F32), pltpu.VMEM((N, nv * C, GDN_DK), BF16),
                        pltpu.VMEM((N, nv * C, nv * C), BF16), pltpu.VMEM((N, nv * C, GDN_DK), BF16),
                        pltpu.VMEM((nv, N, GDN_DK, C), BF16), pltpu.VMEM((nv, N, 8, LANES), F32)],
        compiler_params=_cparams("parallel", "parallel"),
    )(main, main, main, main, tail, conv0p, conv0p, conv0p, conv_w, conv_w, conv_w,
      alog, dtb, norm_w.reshape(1, GDN_DV), s0)


def l2norm(x):
    return x * lax.rsqrt(jnp.sum(x * x, -1, keepdims=True) + NORM_EPS)


def gdn_chunked(q, k, v, g, beta, s0):
    B, L, H, _ = q.shape
    DV = v.shape[-1]
    C = min(GDN_CHUNK, L)
    N = L // C

    def chunks(t):
        t = t.reshape((B, N, C, H) + t.shape[3:])
        return jnp.moveaxis(t, (1, 3), (0, 2))

    q, k, v, g, beta = chunks(q), chunks(k), chunks(v), chunks(g), chunks(beta)
    gc = jnp.cumsum(g, axis=-1)
    incl = jnp.tril(jnp.ones((C, C), bool))
    strict = jnp.tril(jnp.ones((C, C), bool), -1)
    decay = jnp.exp(jnp.where(incl, gc[..., :, None] - gc[..., None, :], -jnp.inf))
    kb = k * beta[..., None]
    vb = v * beta[..., None]
    lmat = jnp.where(strict, jnp.einsum('nbhcd,nbhed->nbhce', kb, k) * decay, 0.0)
    eye = jnp.eye(C, dtype=lmat.dtype)
    tmat = lax.linalg.triangular_solve(lmat + eye, jnp.broadcast_to(eye, lmat.shape),
                                       left_side=True, lower=True, unit_diagonal=True)
    u = jnp.einsum('nbhce,nbhed->nbhcd', tmat, vb)
    w = jnp.einsum('nbhce,nbhed->nbhcd', tmat, kb * jnp.exp(gc)[..., None])
    qk = jnp.einsum('nbhcd,nbhed->nbhce', q, k) * decay

    def step(s, xs):
        q_i, k_i, u_i, w_i, gc_i, qk_i = xs
        v_new = u_i - jnp.einsum('bhcd,bhde->bhce', w_i, s)
        o = (jnp.einsum('bhcd,bhde->bhce', q_i * jnp.exp(gc_i)[..., None], s)
             + jnp.einsum('bhcs,bhse->bhce', qk_i, v_new))
        g_last = gc_i[..., -1:]
        s = (s * jnp.exp(g_last)[..., None]
             + jnp.einsum('bhcd,bhce->bhde', k_i * jnp.exp(g_last - gc_i)[..., None], v_new))
        return s, o

    s, o = lax.scan(step, s0, (q, k, u, w, gc, qk))
    o = jnp.moveaxis(o, (0, 2), (1, 3)).reshape(B, N * C, H, DV)
    return o, s


def gdn_sample(main, tail, s0, conv0, conv_w, a_log, dt_bias, norm_w):
    B, L = conv0.shape[0], main.shape[0] // conv0.shape[0]
    main = main.reshape(B, L, GDN_MAIN)
    tail = tail.reshape(B, L, LANES)
    qkv = main[..., :GDN_CONV_DIM]
    z = main[..., GDN_CONV_DIM:]
    b = tail[..., :GDN_HV]
    a = tail[..., GDN_HV:2 * GDN_HV]
    xc = jnp.concatenate([conv0, qkv], axis=1)
    conv = xc[:, 0:L] * conv_w[0]
    for j in range(1, CONV_W):
        conv = conv + xc[:, j:j + L] * conv_w[j]
    new_conv = xc[:, L:]
    conv = jax.nn.silu(conv)
    q = conv[..., :GDN_KD].reshape(B, L, GDN_HK, GDN_DK)
    k = conv[..., GDN_KD:2 * GDN_KD].reshape(B, L, GDN_HK, GDN_DK)
    v = conv[..., 2 * GDN_KD:].reshape(B, L, GDN_HV, GDN_DV)
    rep = GDN_HV // GDN_HK
    q = jnp.repeat(l2norm(q) * (GDN_DK ** -0.5), rep, axis=2)
    k = jnp.repeat(l2norm(k), rep, axis=2)
    beta = jax.nn.sigmoid(b)
    g = -jnp.exp(a_log) * jax.nn.softplus(a + dt_bias)
    o, s = gdn_chunked(q, k, v, g, beta, s0)
    zf = z.reshape(B, L, GDN_HV, GDN_DV)
    o = o * lax.rsqrt(jnp.mean(o * o, -1, keepdims=True) + NORM_EPS) * norm_w * jax.nn.silu(zf)
    return o.reshape(B * L, GDN_VD).astype(BF16), s, new_conv


def index_topk(qi, wi, ki, qpos, topk):
    s = jnp.einsum('bqhd,bsd->bqhs', qi, ki)
    score = jnp.einsum('bqhs,bqh->bqs', jax.nn.relu(s), wi)
    valid = jnp.arange(ki.shape[1])[None, :] <= qpos[:, None]
    score = jnp.where(valid[None], score, -jnp.inf)
    _, idx = lax.top_k(score, topk)
    return idx


def sparse_attend(q, k_sel, v_sel, sel_valid):
    B, Q = q.shape[:2]
    qg = q.reshape(B, Q, N_KV_HEADS, KV_GROUP, HEAD_DIM)
    s = jnp.einsum('bqngd,bqknd->bqngk', qg, k_sel) * (HEAD_DIM ** -0.5)
    s = jnp.where(sel_valid[:, :, None, None, :], s, -jnp.inf)
    p = jax.nn.softmax(s, axis=-1)
    o = jnp.einsum('bqngk,bqknd->bqngd', p, v_sel)
    return o.reshape(B, Q, DSA_QD)


def gather_rows(t, idx):
    return jax.vmap(lambda tb, ib: tb[ib])(t, idx)


def dsa_sample(pp, wi, ck, cv, cki, page_table):
    B = page_table.shape[0]
    L = pp.shape[0] // B
    past = page_table.shape[1] * PAGE_SIZE
    pos = past + jnp.arange(L)
    o1, o2, o3, o4 = DSA_QD, DSA_QD + DSA_KVD, DSA_QD + 2 * DSA_KVD, DSA_QD + 2 * DSA_KVD + IDX_HEADS * IDX_DIM
    pp = pp.reshape(B, L, DSA_MAIN)
    q = pp[..., :o1].reshape(B, L, N_HEADS, HEAD_DIM)
    k = pp[..., o1:o2].reshape(B, L, N_KV_HEADS, HEAD_DIM)
    v = pp[..., o2:o3].reshape(B, L, N_KV_HEADS, HEAD_DIM)
    qi = pp[..., o3:o4].reshape(B, L, IDX_HEADS, IDX_DIM)
    ki = pp[..., o4:]
    wi = wi.reshape(B, L, LANES)[..., :IDX_HEADS]
    ki_past = cki[page_table].reshape(B, past, IDX_DIM)
    ki_all = jnp.concatenate([ki_past, ki], axis=1)
    topk = min(TOPK_MAX, (past + L) // 4)
    idx = index_topk(qi, wi, ki_all, pos, topk)
    from_past = (idx < past)[..., None, None]
    pidx = jnp.minimum(idx, past - 1)
    phys_page = jnp.take_along_axis(page_table, (pidx // PAGE_SIZE).reshape(B, -1), axis=1).reshape(pidx.shape)
    phys = phys_page * PAGE_SIZE + pidx % PAGE_SIZE
    nidx = jnp.clip(idx - past, 0, L - 1)
    ck_flat = ck.reshape(-1, N_KV_HEADS, HEAD_DIM)
    cv_flat = cv.reshape(-1, N_KV_HEADS, HEAD_DIM)
    k_sel = jnp.where(from_past, ck_flat[phys], gather_rows(k, nidx))
    v_sel = jnp.where(from_past, cv_flat[phys], gather_rows(v, nidx))
    o = sparse_attend(q, k_sel, v_sel, idx <= pos[None, :, None])
    return o.reshape(B * L, DSA_QD).astype(BF16), k, v, ki


def _expert_kernel(be_ref, x_ref, w1_ref, w3_ref, w2_ref, o_ref):
    del be_ref
    x = x_ref[...].astype(BF16)
    h1 = jnp.dot(x, w1_ref[...].astype(BF16), preferred_element_type=F32)
    h3 = jnp.dot(x, w3_ref[...].astype(BF16), preferred_element_type=F32)
    h = _silu(h1) * h3
    o_ref[...] = jnp.dot(h.astype(BF16), w2_ref[...].astype(BF16), preferred_element_type=F32)


def _combine_ln_kernel(x_ref, y0_ref, y1_ref, gate_ref, g_ref, b_ref, o_ref):
    gate = gate_ref[...]
    h = ALPHA * x_ref[...] + (y0_ref[...] * gate[:, 0:1] + y1_ref[...] * gate[:, 1:2])
    mu = jnp.mean(h, -1, keepdims=True)
    hc = h - mu
    var = jnp.mean(hc * hc, -1, keepdims=True)
    o_ref[...] = hc * lax.rsqrt(var + LN_EPS) * g_ref[...] + b_ref[...]


def moe_layer(xt, wg, bg, we, be, w1, w3, w2, layer, ln_g, ln_b, blk=128):
    T, D = xt.shape
    E = N_EXPERTS
    K = TOPK_EXPERTS
    xb16 = xt.astype(BF16)
    lg = jnp.dot(xb16, wg.astype(BF16), preferred_element_type=F32) + bg
    pg = jax.nn.softmax(lg, axis=-1)
    gsel = jnp.argmax(lg, axis=-1)
    le = (jnp.dot(xb16, we.astype(BF16), preferred_element_type=F32) + be).reshape(T, N_GROUPS, EXPERTS_PER_GROUP)
    le_g = jnp.take_along_axis(le, gsel[:, None, None], axis=1)[:, 0]
    pe = jax.nn.softmax(le_g, axis=-1)
    top_p, top_i = lax.top_k(pe, K)
    gate = top_p / jnp.sum(top_p, -1, keepdims=True) * jnp.take_along_axis(pg, gsel[:, None], axis=1)
    eidx = (gsel[:, None] * EXPERTS_PER_GROUP + top_i).astype(I32)

    A = T * K
    nb = A // blk + E
    cnt = jnp.sum((eidx[:, :, None] == jnp.arange(E, dtype=I32)).astype(I32), axis=1)
    cum = jnp.cumsum(cnt, axis=0) - cnt
    counts = jnp.sum(cnt, axis=0)
    padded = (counts + blk - 1) // blk * blk
    pend = jnp.cumsum(padded)
    pstart = pend - padded
    dest = pstart[eidx] + jnp.take_along_axis(cum, eidx, axis=1)
    tok = jnp.broadcast_to(jnp.arange(T, dtype=I32)[:, None], (T, K))
    slot_tok = jnp.full((nb * blk,), T, I32).at[dest.reshape(-1)].set(tok.reshape(-1))
    blk_e = jnp.minimum(jnp.searchsorted(pend, jnp.arange(nb, dtype=I32) * blk, side='right'), E - 1).astype(I32)
    xpad = jnp.concatenate([xt, jnp.zeros((1, D), xt.dtype)], axis=0)
    xb = xpad[slot_tok]

    yb = pl.pallas_call(
        _expert_kernel,
        grid_spec=pltpu.PrefetchScalarGridSpec(
            num_scalar_prefetch=1,
            grid=(nb,),
            in_specs=[pl.BlockSpec((blk, D), lambda i, be_: (i, 0)),
                      pl.BlockSpec((None, None, D, D_EXPERT), lambda i, be_: (layer, be_[i], 0, 0)),
                      pl.BlockSpec((None, None, D, D_EXPERT), lambda i, be_: (layer, be_[i], 0, 0)),
                      pl.BlockSpec((None, None, D_EXPERT, D), lambda i, be_: (layer, be_[i], 0, 0))],
            out_specs=pl.BlockSpec((blk, D), lambda i, be_: (i, 0))),
        out_shape=jax.ShapeDtypeStruct((nb * blk, D), F32),
        compiler_params=_cparams("arbitrary"),
    )(blk_e, xb, w1, w3, w2)

    y0 = yb[dest[:, 0]]
    y1 = yb[dest[:, 1]]
    tm = _row_tile(T, 344)
    return pl.pallas_call(
        _combine_ln_kernel,
        grid=(T // tm,),
        in_specs=[pl.BlockSpec((tm, D), lambda i: (i, 0)),
                  pl.BlockSpec((tm, D), lambda i: (i, 0)),
                  pl.BlockSpec((tm, D), lambda i: (i, 0)),
                  pl.BlockSpec((tm, K), lambda i: (i, 0)),
                  pl.BlockSpec((1, D), lambda i: (0, 0)),
                  pl.BlockSpec((1, D), lambda i: (0, 0))],
        out_specs=pl.BlockSpec((tm, D), lambda i: (i, 0)),
        out_shape=jax.ShapeDtypeStruct((T, D), F32),
        compiler_params=_cparams("parallel"),
    )(xt, y0, y1, gate, ln_g.reshape(1, D), ln_b.reshape(1, D))


def _rope_tables(pos):
    half = HEAD_DIM // 2
    inv = jnp.power(ROPE_THETA, -jnp.arange(half, dtype=F32) / half)
    ang = pos.astype(F32)[:, None] * inv[None, :]
    cos = jnp.cos(ang)
    sin = jnp.sin(ang)
    return jnp.concatenate([cos, cos], -1), jnp.concatenate([-sin, sin], -1)


def kernel(x_prompt, x_sample, state_gdn_s, state_gdn_conv, cache_k, cache_v, cache_kidx, page_table,
           gdn_w_in, gdn_conv_w, gdn_a_log, gdn_dt_bias, gdn_norm_w, gdn_w_out,
           dsa_w_in, dsa_w_out, ln1_g, ln1_b, ln2_g, ln2_b,
           moe_wg, moe_bg, moe_we, moe_be, moe_w1, moe_w3, moe_w2):
    B, L, D = x_prompt.shape
    BS, LS, _ = x_sample.shape
    TP = B * L
    past = page_table.shape[1] * PAGE_SIZE
    x = jnp.concatenate([x_prompt.reshape(TP, D), x_sample.reshape(BS * LS, D)], axis=0)
    pos = jnp.concatenate([jnp.tile(jnp.arange(L), B), jnp.tile(past + jnp.arange(LS), BS)])
    cos, sin = _rope_tables(pos)
    nq = DSA_QD // HEAD_DIM
    nkv = DSA_KVD // HEAD_DIM
    rope_ranges = ((0, nq + nkv), (nq + 2 * nkv, nq + 2 * nkv + IDX_HEADS + 1))
    wi_scale = IDX_HEADS ** -0.5 * IDX_DIM ** -0.5

    p_s, p_c, s_s, s_c = [], [], [], []
    p_k, p_v, p_ki, s_k, s_v, s_ki = [], [], [], [], [], []
    for i in range(DEPTH):
        j = i // N_MIXERS
        if i % N_MIXERS == 0:
            main, tail = project(x, gdn_w_in, j, GDN_MAIN, tn=512, tm_cap=688)
            gp = (gdn_conv_w[j], gdn_a_log[j], gdn_dt_bias[j], gdn_norm_w[j])
            c0 = jnp.zeros((B, CONV_W - 1, GDN_CONV_DIM), F32)
            s0 = jnp.zeros((B, GDN_HV, GDN_DK, GDN_DV), F32)
            op, sp = gdn_core(main, tail, c0, *gp, s0, B, L)
            cp = jnp.stack([main[b * L + L - (CONV_W - 1):(b + 1) * L, :GDN_CONV_DIM] for b in range(B)])
            pad_rows = ((0, 0), (0, GDN_CHUNK - LS), (0, 0))
            main_s = jnp.pad(main[TP:].reshape(BS, LS, GDN_MAIN), pad_rows).reshape(BS * GDN_CHUNK, GDN_MAIN)
            tail_s = jnp.pad(tail[TP:].reshape(BS, LS, LANES), pad_rows).reshape(BS * GDN_CHUNK, LANES)
            osp, ss = gdn_core(main_s, tail_s, state_gdn_conv[j], *gp, state_gdn_s[j], BS, GDN_CHUNK, valid=LS)
            os_ = osp.reshape(BS, GDN_CHUNK, GDN_VD)[:, :LS].reshape(BS * LS, GDN_VD)
            cs = main[TP:, :GDN_CONV_DIM].reshape(BS, LS, GDN_CONV_DIM)[:, LS - (CONV_W - 1):]
            p_s.append(sp); p_c.append(cp); s_s.append(ss); s_c.append(cs)
            w_out = gdn_w_out
        else:
            pp, wi = project(x, dsa_w_in, j, DSA_MAIN, tn=HEAD_DIM, tm_cap=1376, cos=cos, sin=sin,
                             rope_ranges=rope_ranges, tail_scale=wi_scale)
            op = dsa_prompt_attend(pp, wi, B, L)
            os_, kn, vn, kin = dsa_sample(pp[TP:], wi[TP:], cache_k[j], cache_v[j], cache_kidx[j], page_table)
            p_k.append(pp[:TP, DSA_QD:DSA_QD + DSA_KVD].reshape(B, L, N_KV_HEADS, HEAD_DIM))
            p_v.append(pp[:TP, DSA_QD + DSA_KVD:DSA_QD + 2 * DSA_KVD].reshape(B, L, N_KV_HEADS, HEAD_DIM))
            p_ki.append(pp[:TP, DSA_MAIN - IDX_DIM:].reshape(B, L, IDX_DIM))
            s_k.append(kn); s_v.append(vn); s_ki.append(kin)
            w_out = dsa_w_out
        o_all = jnp.concatenate([op, os_], axis=0)
        x = matmul_res_ln(o_all, w_out, j, x, ln1_g[i], ln1_b[i])
        x = moe_layer(x, moe_wg[i], moe_bg[i], moe_we[i], moe_be[i], moe_w1, moe_w3, moe_w2, i,
                      ln2_g[i], ln2_b[i])
    xp = x[:TP].reshape(B, L, D)
    xs = x[TP:].reshape(BS, LS, D)
    return (xp, xs, jnp.stack(p_s), jnp.stack(p_c), jnp.stack(p_k), jnp.stack(p_v), jnp.stack(p_ki),
            jnp.stack(s_s), jnp.stack(s_c), jnp.stack(s_k), jnp.stack(s_v), jnp.stack(s_ki))
```

```python
import functools

import jax
import jax.numpy as jnp
from jax import lax
from jax.experimental import pallas as pl
from jax.experimental.pallas import tpu as pltpu

D_MODEL = 2048
DEPTH = 4
PAGE_SIZE = 128
N_MIXERS = 2
GDN_DK = 128
GDN_DV = 128
GDN_HK = D_MODEL // GDN_DK
GDN_HV = 2 * GDN_HK
GDN_KD = GDN_HK * GDN_DK
GDN_VD = GDN_HV * GDN_DV
GDN_CONV_DIM = 2 * GDN_KD + GDN_VD
GDN_MAIN = GDN_CONV_DIM + GDN_VD
CONV_W = 4
GDN_CHUNK = 64
HEAD_DIM = 128
N_HEADS = D_MODEL // HEAD_DIM
N_KV_HEADS = 4
KV_GROUP = N_HEADS // N_KV_HEADS
IDX_HEADS = 16
IDX_DIM = 128
DSA_QD = N_HEADS * HEAD_DIM
DSA_KVD = N_KV_HEADS * HEAD_DIM
DSA_MAIN = DSA_QD + 2 * DSA_KVD + IDX_HEADS * IDX_DIM + IDX_DIM
TOPK_MAX = 256
ROPE_THETA = 10000.0
N_GROUPS = 4
EXPERTS_PER_GROUP = 8
N_EXPERTS = N_GROUPS * EXPERTS_PER_GROUP
TOPK_EXPERTS = 2
D_EXPERT = D_MODEL // 4
ALPHA = (2 * DEPTH) ** 0.25
LN_EPS = 1e-5
NORM_EPS = 1e-6

LANES = 128
F32 = jnp.float32
BF16 = jnp.bfloat16
I32 = jnp.int32
VMEM_LIMIT = 56 * 1024 * 1024
INT_MIN = -2 ** 31
NEG_INF = float("-inf")


def _cparams(*sem):
    return pltpu.CompilerParams(dimension_semantics=sem, vmem_limit_bytes=VMEM_LIMIT)


def _bdot(a, b):
    return jnp.dot(a.astype(BF16), b.astype(BF16), preferred_element_type=F32)


def _bdot_nt(a, b):
    return lax.dot_general(a.astype(BF16), b.astype(BF16), (((1,), (1,)), ((), ())),
                           preferred_element_type=F32)


def _row_tile(T, cap):
    if T <= cap:
        return T
    best = None
    for t in range(16, cap + 1, 16):
        if T % t == 0:
            best = t
    assert best is not None, T
    return best


def _proj_kernel(x_ref, w_ref, wt_ref, *rest, rope_ranges, tail_scale):
    if rope_ranges:
        cos_ref, sin_ref, o_ref, t_ref = rest
    else:
        o_ref, t_ref = rest
    j = pl.program_id(1)
    xb = x_ref[...].astype(BF16)
    acc = jnp.dot(xb, w_ref[...].astype(BF16), preferred_element_type=F32)
    if rope_ranges:
        roped = acc * cos_ref[...] + pltpu.roll(acc, HEAD_DIM // 2, 1) * sin_ref[...]
        is_rope = (j >= rope_ranges[0][0]) & (j < rope_ranges[0][1])
        for lo, hi in rope_ranges[1:]:
            is_rope = is_rope | ((j >= lo) & (j < hi))
        acc = jnp.where(is_rope, roped, acc)
    o_ref[...] = acc

    @pl.when(j == 0)
    def _():
        t_ref[...] = jnp.dot(xb, wt_ref[...].astype(BF16), preferred_element_type=F32) * tail_scale


def project(x, w, layer, n_main, tn, tm_cap, cos=None, sin=None, rope_ranges=(), tail_scale=1.0):
    T, D = x.shape
    tm = _row_tile(T, tm_cap)
    n_tail = w.shape[2] - n_main
    w_tail = jnp.pad(w[layer, :, n_main:], ((0, 0), (0, LANES - n_tail)))
    in_specs = [pl.BlockSpec((tm, D), lambda i, j: (i, 0)),
                pl.BlockSpec((None, D, tn), lambda i, j: (layer, 0, j)),
                pl.BlockSpec((D, LANES), lambda i, j: (0, 0))]
    args = [x, w, w_tail]
    if rope_ranges:
        assert tn == HEAD_DIM
        in_specs += [pl.BlockSpec((tm, LANES), lambda i, j: (i, 0))] * 2
        args += [cos, sin]
    return pl.pallas_call(
        functools.partial(_proj_kernel, rope_ranges=tuple(rope_ranges), tail_scale=tail_scale),
        grid=(T // tm, n_main // tn),
        in_specs=in_specs,
        out_specs=[pl.BlockSpec((tm, tn), lambda i, j: (i, j)),
                   pl.BlockSpec((tm, LANES), lambda i, j: (i, 0))],
        out_shape=[jax.ShapeDtypeStruct((T, n_main), F32), jax.ShapeDtypeStruct((T, LANES), F32)],
        compiler_params=_cparams("parallel", "arbitrary"),
    )(*args)


def _mm_res_ln_kernel(x_ref, w_ref, r_ref, g_ref, b_ref, o_ref, *, nk):
    k = pl.program_id(1)
    part = jnp.dot(x_ref[...].astype(BF16), w_ref[...].astype(BF16), preferred_element_type=F32)

    @pl.when(k == 0)
    def _():
        o_ref[...] = part

    @pl.when(k > 0)
    def _():
        o_ref[...] += part

    @pl.when(k == nk - 1)
    def _():
        h = ALPHA * r_ref[...] + o_ref[...]
        mu = jnp.mean(h, -1, keepdims=True)
        hc = h - mu
        var = jnp.mean(hc * hc, -1, keepdims=True)
        o_ref[...] = hc * lax.rsqrt(var + LN_EPS) * g_ref[...] + b_ref[...]


def matmul_res_ln(x, w, layer, resid, g, b, tm_cap=688, tk=512):
    T, K = x.shape
    D = w.shape[2]
    tm = _row_tile(T, tm_cap)
    nk = K // tk
    return pl.pallas_call(
        functools.partial(_mm_res_ln_kernel, nk=nk),
        grid=(T // tm, nk),
        in_specs=[pl.BlockSpec((tm, tk), lambda i, k: (i, k)),
                  pl.BlockSpec((None, tk, D), lambda i, k: (layer, k, 0)),
                  pl.BlockSpec((tm, D), lambda i, k: (i, 0)),
                  pl.BlockSpec((1, D), lambda i, k: (0, 0)),
                  pl.BlockSpec((1, D), lambda i, k: (0, 0))],
        out_specs=pl.BlockSpec((tm, D), lambda i, k: (i, 0)),
        out_shape=jax.ShapeDtypeStruct((T, D), F32),
        compiler_params=_cparams("parallel", "arbitrary"),
    )(x, w, resid, g.reshape(1, D), b.reshape(1, D))


def _dsa_prompt_kernel(q_ref, qi0_ref, qi1_ref, wi_ref, k_ref, v_ref, ki_ref, o_ref,
                       kbf, vbf, kibf, key_s, bias_s, *, tq, seq, topk, s_step):
    i = pl.program_id(1)

    @pl.when(i == 0)
    def _():
        kbf[...] = k_ref[...].astype(BF16)
        vbf[...] = v_ref[...].astype(BF16)
        kibf[...] = ki_ref[...].astype(BF16)

    half = IDX_HEADS // 2
    qi_rows = [qi0_ref[:, h * IDX_DIM:(h + 1) * IDX_DIM].astype(BF16) for h in range(half)]
    qi_rows += [qi1_ref[:, h * IDX_DIM:(h + 1) * IDX_DIM].astype(BF16) for h in range(half)]
    qi_stack = jnp.concatenate(qi_rows, axis=0)
    wib = wi_ref[...].astype(BF16).astype(F32)
    q_rows = [jnp.concatenate([q_ref[:, (n * KV_GROUP + g) * HEAD_DIM:(n * KV_GROUP + g + 1) * HEAD_DIM]
                               for g in range(KV_GROUP)], axis=0).astype(BF16)
              for n in range(N_KV_HEADS)]

    def body(S):
        qpos = i * tq + lax.broadcasted_iota(I32, (tq, 1), 0)
        for c0 in range(0, S, s_step):
            s = _bdot_nt(qi_stack, kibf[c0:c0 + s_step, :])
            r = jnp.maximum(s, 0.0).astype(BF16).astype(F32)
            score = r[0:tq] * wib[:, 0:1]
            for h in range(1, IDX_HEADS):
                score = score + r[h * tq:(h + 1) * tq] * wib[:, h:h + 1]
            score = score + 0.0
            bits = pltpu.bitcast(score, I32)
            key = jnp.where(bits < 0, bits ^ jnp.int32(0x7FFFFFFF), bits)
            spos = c0 + lax.broadcasted_iota(I32, (tq, s_step), 1)
            key_s[:, c0:c0 + s_step] = jnp.where(spos <= qpos, key, jnp.int32(INT_MIN))

        def count_ge(cand):
            return jnp.sum((key_s[:, 0:S] >= cand).astype(I32), axis=1, keepdims=True)

        t0 = jnp.where(count_ge(jnp.zeros((tq, 1), I32)) >= topk, jnp.int32(0), jnp.int32(INT_MIN))
        t0 = jnp.broadcast_to(t0, (tq, 1))

        def bit_step(it, t):
            cand = t | jnp.left_shift(jnp.int32(1), 30 - it)
            return jnp.where(count_ge(cand) >= topk, cand, t)

        thr = lax.fori_loop(0, 31, bit_step, t0)

        keyv = key_s[:, 0:S]
        valid = lax.broadcasted_iota(I32, (tq, S), 1) <= qpos
        ge = keyv >= thr
        n_ge = jnp.sum((ge & valid).astype(I32), axis=1, keepdims=True)
        has_tie = jnp.max(n_ge) > topk
        bias_s[:, 0:S] = jnp.where(ge & valid, 0.0, NEG_INF)

        @pl.when(has_tie)
        def _():
            gt = keyv > thr
            n_gt = jnp.sum((gt & valid).astype(I32), axis=1, keepdims=True)
            room = (topk - n_gt).astype(F32)
            eq = ((keyv == thr) & valid)
            tri = (lax.broadcasted_iota(I32, (LANES, LANES), 0)
                   < lax.broadcasted_iota(I32, (LANES, LANES), 1)).astype(BF16)
            carry = jnp.zeros((tq, 1), F32)
            for c0 in range(0, S, LANES):
                eqc = eq[:, c0:c0 + LANES]
                before = carry + jnp.dot(eqc.astype(BF16), tri, preferred_element_type=F32)
                keep = (gt[:, c0:c0 + LANES] & valid[:, c0:c0 + LANES]) | (eqc & (before < room))
                bias_s[:, c0:c0 + LANES] = jnp.where(keep, 0.0, NEG_INF)
                carry = carry + jnp.sum(eqc.astype(F32), axis=1, keepdims=True)

        bias = bias_s[:, 0:S]
        for n in range(N_KV_HEADS):
            s = _bdot_nt(q_rows[n], kbf[0:S, n * HEAD_DIM:(n + 1) * HEAD_DIM]) * (HEAD_DIM ** -0.5)
            s = s.reshape(KV_GROUP, tq, S) + bias[None]
            m = jnp.max(s, axis=-1, keepdims=True)
            p = jnp.exp(s - m)
            l = jnp.sum(p, axis=-1, keepdims=True)
            o = jnp.dot(p.reshape(KV_GROUP * tq, S).astype(BF16), vbf[0:S, n * HEAD_DIM:(n + 1) * HEAD_DIM],
                        preferred_element_type=F32)
            o = o.reshape(KV_GROUP, tq, HEAD_DIM) / l
            for g in range(KV_GROUP):
                h = n * KV_GROUP + g
                o_ref[:, h * HEAD_DIM:(h + 1) * HEAD_DIM] = o[g].astype(o_ref.dtype)

    n_var = seq // s_step
    per = (seq // tq) // n_var
    for c in range(n_var):
        @pl.when(i // per == c)
        def _(c=c):
            body((c + 1) * s_step)


def dsa_prompt_attend(pp, wi, batch, seq, tq=128, s_step=512):
    topk = min(TOPK_MAX, seq // 4)
    s_step = min(s_step, seq)
    nq = seq // tq
    kcol = DSA_QD // DSA_KVD
    qicol = (DSA_QD + 2 * DSA_KVD) // (IDX_HEADS * IDX_DIM // 2)
    kicol = (DSA_QD + 2 * DSA_KVD + IDX_HEADS * IDX_DIM) // IDX_DIM
    assert (DSA_QD + 2 * DSA_KVD) % (IDX_HEADS * IDX_DIM // 2) == 0
    hq = IDX_HEADS * IDX_DIM // 2
    return pl.pallas_call(
        functools.partial(_dsa_prompt_kernel, tq=tq, seq=seq, topk=topk, s_step=s_step),
        grid=(batch, nq),
        in_specs=[pl.BlockSpec((tq, DSA_QD), lambda b, i: (b * nq + i, 0)),
                  pl.BlockSpec((tq, hq), lambda b, i: (b * nq + i, qicol)),
                  pl.BlockSpec((tq, hq), lambda b, i: (b * nq + i, qicol + 1)),
                  pl.BlockSpec((tq, LANES), lambda b, i: (b * nq + i, 0)),
                  pl.BlockSpec((seq, DSA_KVD), lambda b, i: (b, kcol)),
                  pl.BlockSpec((seq, DSA_KVD), lambda b, i: (b, kcol + 1)),
                  pl.BlockSpec((seq, IDX_DIM), lambda b, i: (b, kicol))],
        out_specs=pl.BlockSpec((tq, DSA_QD), lambda b, i: (b * nq + i, 0)),
        out_shape=jax.ShapeDtypeStruct((batch * seq, DSA_QD), BF16),
        scratch_shapes=[pltpu.VMEM((seq, DSA_KVD), BF16), pltpu.VMEM((seq, DSA_KVD), BF16),
                        pltpu.VMEM((seq, IDX_DIM), BF16),
                        pltpu.VMEM((tq, seq), I32), pltpu.VMEM((tq, seq), F32)],
        compiler_params=_cparams("parallel", "arbitrary"),
    )(pp, pp, pp, wi, pp, pp, pp)


def _order_key(score):
    bits = pltpu.bitcast(score + 0.0, I32)
    return jnp.where(bits < 0, bits ^ jnp.int32(0x7FFFFFFF), bits)


def _dsa_sample_kernel(pt_ref, q_ref, qi0_ref, qi1_ref, wi_ref, kn_ref, vn_ref, kin_ref, ck_hbm, cv_hbm, cki_hbm,
                       o_ref, kibuf, kvbuf, key_s, bias_s, sc_s, p_s, sem_ki, sem_kv,
                       *, layer, ls, n_pages, ppc, topk):
    b = pl.program_id(0)
    past = n_pages * PAGE_SIZE
    S = past + LANES
    ck = ppc * PAGE_SIZE
    n_chunks = n_pages // ppc
    gk = 4 * PAGE_SIZE
    rows = KV_GROUP * ls

    def ki_copy(p):
        return pltpu.make_async_copy(cki_hbm.at[layer, pt_ref[b, p]], kibuf.at[p], sem_ki)

    def kv_copy(src, c, slot, i):
        return pltpu.make_async_copy(src.at[layer, pt_ref[b, c * ppc + i]],
                                     kvbuf.at[slot, pl.ds(i * PAGE_SIZE, PAGE_SIZE)], sem_kv.at[slot])

    def start_chunk(src, c, slot):
        for i in range(ppc):
            kv_copy(src, c, slot, i).start()

    def wait_chunk(src, c, slot):
        for i in range(ppc):
            kv_copy(src, c, slot, i).wait()

    def ki_start(p, carry):
        ki_copy(p).start()
        return carry

    def ki_wait(p, carry):
        ki_copy(p).wait()
        return carry

    lax.fori_loop(0, n_pages, ki_start, 0)
    start_chunk(ck_hbm, 0, 0)

    half = IDX_HEADS // 2
    qi_stack = jnp.concatenate([qi0_ref[:, h * IDX_DIM:(h + 1) * IDX_DIM] for h in range(half)]
                               + [qi1_ref[:, h * IDX_DIM:(h + 1) * IDX_DIM] for h in range(half)],
                               axis=0).astype(BF16)
    wib = wi_ref[...].astype(BF16).astype(F32)
    q_rows = [jnp.concatenate([q_ref[:, (n * KV_GROUP + g) * HEAD_DIM:(n * KV_GROUP + g + 1) * HEAD_DIM]
                               for g in range(KV_GROUP)], axis=0).astype(BF16)
              for n in range(N_KV_HEADS)]
    qpos = past + lax.broadcasted_iota(I32, (ls, 1), 0)
    zpad = jnp.zeros((LANES - ls, DSA_KVD), F32)

    def index_keys(ki_rows):
        s = _bdot_nt(qi_stack, ki_rows)
        r = jnp.maximum(s, 0.0).astype(BF16).astype(F32)
        score = r[0:ls] * wib[:, 0:1]
        for h in range(1, IDX_HEADS):
            score = score + r[h * ls:(h + 1) * ls] * wib[:, h:h + 1]
        return _order_key(score)

    lax.fori_loop(0, n_pages, ki_wait, 0)

    def index_step(g, carry):
        kic = kibuf[pl.ds(g * (gk // PAGE_SIZE), gk // PAGE_SIZE)].reshape(gk, IDX_DIM)
        key_s[:, pl.ds(pl.multiple_of(g * gk, gk), gk)] = index_keys(kic)
        return carry

    lax.fori_loop(0, past // gk, index_step, 0)
    kin_pad = jnp.concatenate([kin_ref[...], zpad[:, 0:IDX_DIM]], axis=0)
    new_pos = past + lax.broadcasted_iota(I32, (ls, LANES), 1)
    key_s[:, past:S] = jnp.where(new_pos <= qpos, index_keys(kin_pad), jnp.int32(INT_MIN))

    def count_ge(cand):
        return jnp.sum((key_s[...] >= cand).astype(I32), axis=1, keepdims=True)

    t0 = jnp.where(count_ge(jnp.zeros((ls, 1), I32)) >= topk, jnp.int32(0), jnp.int32(INT_MIN))

    def bit_step(it, t):
        cand = t | jnp.left_shift(jnp.int32(1), 30 - it)
        return jnp.where(count_ge(cand) >= topk, cand, t)

    thr = lax.fori_loop(0, 31, bit_step, t0)
    keyv = key_s[...]
    valid = lax.broadcasted_iota(I32, (ls, S), 1) <= qpos
    ge = (keyv >= thr) & valid
    n_ge = jnp.sum(ge.astype(I32), axis=1, keepdims=True)
    bias_s[...] = jnp.where(ge, 0.0, NEG_INF)

    @pl.when(jnp.max(n_ge) > topk)
    def _():
        n_gt = jnp.sum(((keyv > thr) & valid).astype(I32), axis=1, keepdims=True)
        room = (topk - n_gt).astype(F32)
        tri = (lax.broadcasted_iota(I32, (LANES, LANES), 0)
               < lax.broadcasted_iota(I32, (LANES, LANES), 1)).astype(BF16)

        def tie_step(c, carry):
            off = pl.multiple_of(c * LANES, LANES)
            kc = key_s[:, pl.ds(off, LANES)]
            ok = (off + lax.broadcasted_iota(I32, (ls, LANES), 1)) <= qpos
            eq = (kc == thr) & ok
            before = carry + jnp.dot(eq.astype(BF16), tri, preferred_element_type=F32)
            keep = ((kc > thr) & ok) | (eq & (before < room))
            bias_s[:, pl.ds(off, LANES)] = jnp.where(keep, 0.0, NEG_INF)
            return carry + jnp.sum(eq.astype(F32), axis=1, keepdims=True)

        lax.fori_loop(0, S // LANES, tie_step, jnp.zeros((ls, 1), F32))

    def masked_scores(n, k_rows, bias):
        s = _bdot_nt(q_rows[n], k_rows) * (HEAD_DIM ** -0.5)
        return (s.reshape(KV_GROUP, ls, s.shape[1]) + bias[None]).reshape(rows, s.shape[1])

    def k_step(c, carry):
        slot = c % 2
        wait_chunk(ck_hbm, c, slot)

        @pl.when(c + 1 < n_chunks)
        def _():
            start_chunk(ck_hbm, c + 1, 1 - slot)

        @pl.when(c + 1 == n_chunks)
        def _():
            start_chunk(cv_hbm, 0, 1 - slot)

        kc = kvbuf[slot].astype(BF16)
        off = pl.multiple_of(c * ck, ck)
        bias = bias_s[:, pl.ds(off, ck)]
        for n in range(N_KV_HEADS):
            sc_s[n * rows:(n + 1) * rows, pl.ds(off, ck)] = masked_scores(
                n, kc[:, n * HEAD_DIM:(n + 1) * HEAD_DIM], bias)
        return carry

    lax.fori_loop(0, n_chunks, k_step, 0)
    kn_pad = jnp.concatenate([kn_ref[...], zpad], axis=0).astype(BF16)
    for n in range(N_KV_HEADS):
        sc_s[n * rows:(n + 1) * rows, past:S] = masked_scores(
            n, kn_pad[:, n * HEAD_DIM:(n + 1) * HEAD_DIM], bias_s[:, past:S])

    sc = sc_s[...]
    m = jnp.max(sc, axis=1, keepdims=True)
    p = jnp.exp(sc - m)
    l = jnp.sum(p, axis=1, keepdims=True)
    p_s[...] = p.astype(BF16)

    def v_step(v, acc):
        slot = (n_chunks + v) % 2
        wait_chunk(cv_hbm, v, slot)

        @pl.when(v + 1 < n_chunks)
        def _():
            start_chunk(cv_hbm, v + 1, 1 - slot)

        vc = kvbuf[slot].astype(BF16)
        off = pl.multiple_of(v * ck, ck)
        return tuple(acc[n] + jnp.dot(p_s[n * rows:(n + 1) * rows, pl.ds(off, ck)],
                                      vc[:, n * HEAD_DIM:(n + 1) * HEAD_DIM], preferred_element_type=F32)
                     for n in range(N_KV_HEADS))

    acc = lax.fori_loop(0, n_chunks, v_step, tuple(jnp.zeros((rows, HEAD_DIM), F32) for _ in range(N_KV_HEADS)))
    vn_pad = jnp.concatenate([vn_ref[...], zpad], axis=0).astype(BF16)
    for n in range(N_KV_HEADS):
        o = acc[n] + jnp.dot(p_s[n * rows:(n + 1) * rows, past:S], vn_pad[:, n * HEAD_DIM:(n + 1) * HEAD_DIM],
                             preferred_element_type=F32)
        o = o / l[n * rows:(n + 1) * rows]
        for g in range(KV_GROUP):
            h = n * KV_GROUP + g
            o_ref[:, h * HEAD_DIM:(h + 1) * HEAD_DIM] = o[g * ls:(g + 1) * ls]


def dsa_sample_attend(pp, wi, cache_k, cache_v, cache_kidx, page_table, layer, row0, batch, ls, ppc=16):
    n_pool = cache_k.shape[1]
    n_pages = page_table.shape[1]
    past = n_pages * PAGE_SIZE
    topk = min(TOPK_MAX, (past + ls) // 4)
    assert row0 % ls == 0 and n_pages % ppc == 0 and n_pages % 4 == 0 and ls % 8 == 0
    rb = row0 // ls
    ck = cache_k.reshape(cache_k.shape[0], n_pool, PAGE_SIZE, DSA_KVD)
    cv = cache_v.reshape(cache_v.shape[0], n_pool, PAGE_SIZE, DSA_KVD)
    kcol = DSA_QD // DSA_KVD
    hq = IDX_HEADS * IDX_DIM // 2
    qicol = (DSA_QD + 2 * DSA_KVD) // hq
    kicol = (DSA_QD + 2 * DSA_KVD + IDX_HEADS * IDX_DIM) // IDX_DIM
    S = past + LANES
    return pl.pallas_call(
        functools.partial(_dsa_sample_kernel, layer=layer, ls=ls, n_pages=n_pages, ppc=ppc, topk=topk),
        grid_spec=pltpu.PrefetchScalarGridSpec(
            num_scalar_prefetch=1,
            grid=(batch,),
            in_specs=[pl.BlockSpec((ls, DSA_QD), lambda b, pt: (rb + b, 0)),
                      pl.BlockSpec((ls, hq), lambda b, pt: (rb + b, qicol)),
                      pl.BlockSpec((ls, hq), lambda b, pt: (rb + b, qicol + 1)),
                      pl.BlockSpec((ls, LANES), lambda b, pt: (rb + b, 0)),
                      pl.BlockSpec((ls, DSA_KVD), lambda b, pt: (rb + b, kcol)),
                      pl.BlockSpec((ls, DSA_KVD), lambda b, pt: (rb + b, kcol + 1)),
                      pl.BlockSpec((ls, IDX_DIM), lambda b, pt: (rb + b, kicol)),
                      pl.BlockSpec(memory_space=pl.ANY),
                      pl.BlockSpec(memory_space=pl.ANY),
                      pl.BlockSpec(memory_space=pl.ANY)],
            out_specs=pl.BlockSpec((ls, DSA_QD), lambda b, pt: (b, 0)),
            scratch_shapes=[pltpu.VMEM((n_pages, PAGE_SIZE, IDX_DIM), F32),
                            pltpu.VMEM((2, ppc * PAGE_SIZE, DSA_KVD), F32),
                            pltpu.VMEM((ls, S), I32), pltpu.VMEM((ls, S), F32),
                            pltpu.VMEM((N_HEADS * ls, S), F32), pltpu.VMEM((N_HEADS * ls, S), BF16),
                            pltpu.SemaphoreType.DMA(()), pltpu.SemaphoreType.DMA((2,))]),
        out_shape=jax.ShapeDtypeStruct((batch * ls, DSA_QD), F32),
        compiler_params=_cparams("arbitrary"),
    )(page_table, pp, pp, pp, wi, pp, pp, pp, ck, cv, cache_kidx)


def _split3(a):
    hi = a.astype(BF16)
    lo = (a - hi.astype(F32)).astype(BF16)
    return hi, lo


def _dot3(a_parts, b_parts):
    ah, al = a_parts
    bh, bl = b_parts
    d = functools.partial(jnp.dot, preferred_element_type=F32)
    return d(ah, bh) + (d(ah, bl) + d(al, bh))


def _tri_inverse(mats, order):
    n = mats[0].shape[0]
    eye = (lax.broadcasted_iota(I32, (n, n), 0) == lax.broadcasted_iota(I32, (n, n), 1)).astype(F32)
    ps = [eye - a for a in mats]
    xss = [_split3(-a) for a in mats]
    steps = max(0, (order - 1).bit_length() - 1)
    for _ in range(steps):
        xss = [_split3(_dot3(xs, xs)) for xs in xss]
        ps = [p + _dot3(_split3(p), xs) for p, xs in zip(ps, xss)]
    return ps


def _silu(x):
    return x * jax.nn.sigmoid(x)


def _gdn_kernel(xq_ref, xk_ref, xv_ref, z_ref, tail_ref, cq_ref, ck_ref, cv_ref, wq_ref, wk_ref, wv_ref,
                alog_ref, dtb_ref, nw_ref, s0_ref, o_ref, s_ref,
                beta_s, g_s, u_s, w_s, qk_s, qg_s, kd_s, el_s, *, seq, chunk, valid, hpb, unroll):
    hb = pl.program_id(1)
    C = chunk
    N = seq // C
    HALO = 8
    NV = 2 * hpb
    tail = tail_ref[...]
    beta = jax.nn.sigmoid(tail)
    x = tail + dtb_ref[...]
    softplus = jnp.maximum(x, 0.0) + jnp.log1p(jnp.exp(-jnp.abs(x)))
    g = -jnp.exp(alog_ref[...]) * softplus
    if valid < seq:
        is_real = lax.broadcasted_iota(I32, (seq, LANES), 0) < valid
        beta = jnp.where(is_real, beta, 0.0)
        g = jnp.where(is_real, g, 0.0)
    beta_s[...] = beta
    g_s[...] = g

    R = NV * C
    row = lax.broadcasted_iota(I32, (R, R), 0)
    col = lax.broadcasted_iota(I32, (R, R), 1)
    log2c = C.bit_length() - 1
    same_head = lax.shift_right_logical(row, log2c) == lax.shift_right_logical(col, log2c)
    incl = same_head & (row >= col)
    strict = same_head & (row > col)
    lane = lax.broadcasted_iota(I32, (C, LANES), 1)
    sub_t = lax.broadcasted_iota(I32, (LANES, C), 0)
    rowc = lax.broadcasted_iota(I32, (C, LANES), 0)

    def conv(xref, cref, wref, r0, c):
        prev = xref[pl.ds(pl.multiple_of(jnp.maximum(r0 - HALO, 0), HALO), HALO), :]
        win = jnp.concatenate([jnp.where(c == 0, cref[0], prev), xref[pl.ds(r0, C), :]], axis=0)
        acc = win[HALO - 3:HALO - 3 + C] * wref[0:1, :]
        for j in range(1, CONV_W):
            acc = acc + win[HALO - 3 + j:HALO - 3 + j + C] * wref[j:j + 1, :]
        return _silu(acc)

    def l2n(t):
        return t * lax.rsqrt(jnp.sum(t * t, -1, keepdims=True) + NORM_EPS)

    def prep_inputs(c):
        r0 = pl.multiple_of(c * C, C)
        qc = conv(xq_ref, cq_ref, wq_ref, r0, c)
        kc = conv(xk_ref, ck_ref, wk_ref, r0, c)
        vv = conv(xv_ref, cv_ref, wv_ref, r0, c)
        beta = beta_s[pl.ds(r0, C), :]
        gc = g_s[pl.ds(r0, C), :]
        sh = 1
        while sh < C:
            gc = gc + jnp.where(rowc >= sh, pltpu.roll(gc, sh, 0), 0.0)
            sh *= 2
        gc_t = gc.T
        qn = [l2n(qc[:, hl * GDN_DK:(hl + 1) * GDN_DK]) * (GDN_DK ** -0.5) for hl in range(hpb)]
        kn = [l2n(kc[:, hl * GDN_DK:(hl + 1) * GDN_DK]) for hl in range(hpb)]
        bcols, gcols, grows = [], [], []
        for e in range(NV):
            hv = NV * hb + e
            bcols.append(jnp.sum(jnp.where(lane == hv, beta, 0.0), axis=1, keepdims=True))
            gcols.append(jnp.sum(jnp.where(lane == GDN_HV + hv, gc, 0.0), axis=1, keepdims=True))
            grows.append(jnp.sum(jnp.where(sub_t == GDN_HV + hv, gc_t, 0.0), axis=0, keepdims=True))
        bcol = jnp.concatenate(bcols, axis=0)
        gcol = jnp.concatenate(gcols, axis=0)
        grow = jnp.concatenate(grows, axis=1)
        kn_st = jnp.concatenate([kn[e // 2] for e in range(NV)], axis=0)
        qn_st = jnp.concatenate([qn[e // 2] for e in range(NV)], axis=0)
        v_st = jnp.concatenate([vv[:, e * GDN_DV:(e + 1) * GDN_DV] for e in range(NV)], axis=0)
        decay = jnp.exp(jnp.where(incl, gcol - grow, NEG_INF))
        kb = kn_st * bcol
        a = jnp.where(strict, _bdot_nt(kb, kn_st) * decay, 0.0)
        qk_s[c] = jnp.where(incl, _bdot_nt(qn_st, kn_st) * decay, 0.0).astype(BF16)
        qg_s[c] = (qn_st * jnp.exp(gcol)).astype(BF16)
        for e in range(NV):
            glast = grows[e][:, C - 1:C]
            kd = kn[e // 2] * jnp.exp(glast - gcols[e])
            kd_s[e, c] = kd.T.astype(BF16)
            el_s[e, c] = jnp.broadcast_to(jnp.exp(glast), (8, LANES))
        return a, v_st * bcol, kb * jnp.exp(gcol)

    def prep(it, carry):
        cs = [it * unroll + u for u in range(unroll)]
        ins = [prep_inputs(c) for c in cs]
        tmats = _tri_inverse([a for a, _, _ in ins], C)
        for c, tmat, (_, vb, kbg) in zip(cs, tmats, ins):
            u_s[c] = _bdot(tmat, vb)
            w_s[c] = _bdot(tmat, kbg).astype(BF16)
        return carry

    lax.fori_loop(0, N // unroll, prep, 0)

    nw = nw_ref[...]
    s_ref[...] = s0_ref[...]

    def scan(c, carry):
        r0 = pl.multiple_of(c * C, C)
        d = functools.partial(jnp.dot, preferred_element_type=F32)
        u = u_s[c]
        w = w_s[c]
        qg = qg_s[c]
        sts = [s_ref[0, e] for e in range(NV)]
        sbs = [st.astype(BF16) for st in sts]
        vb = jnp.concatenate([u[e * C:(e + 1) * C] - d(w[e * C:(e + 1) * C], sbs[e]) for e in range(NV)],
                             axis=0).astype(BF16)
        o_intra = d(qk_s[c], vb)
        for e in range(NV):
            o = d(qg[e * C:(e + 1) * C], sbs[e]) + o_intra[e * C:(e + 1) * C]
            s_ref[0, e] = sts[e] * el_s[e, c][0:1, :] + d(kd_s[e, c], vb[e * C:(e + 1) * C])
            zf = z_ref[pl.ds(r0, C), e * GDN_DV:(e + 1) * GDN_DV]
            og = o * lax.rsqrt(jnp.mean(o * o, -1, keepdims=True) + NORM_EPS) * nw * _silu(zf)
            o_ref[pl.ds(r0, C), e * GDN_DV:(e + 1) * GDN_DV] = og.astype(o_ref.dtype)
        return carry

    lax.fori_loop(0, N, scan, 0)


def gdn_core(main, tail, conv0, conv_w, a_log, dt_bias, norm_w, s0, batch, seq, valid=None, hpb=2, unroll=2):
    valid = seq if valid is None else valid
    C = min(GDN_CHUNK, seq)
    N = seq // C
    assert seq % C == 0
    unroll = unroll if N % unroll == 0 else 1
    conv0p = jnp.pad(conv0, ((0, 0), (8 - (CONV_W - 1), 0), (0, 0)))
    alog = jnp.pad(a_log, (GDN_HV, LANES - 2 * GDN_HV)).reshape(1, LANES)
    dtb = jnp.pad(dt_bias, (GDN_HV, LANES - 2 * GDN_HV)).reshape(1, LANES)
    qw = GDN_DK * hpb
    vw = 2 * GDN_DV * hpb
    kblk = GDN_KD // qw
    vblk = 2 * GDN_KD // vw
    zblk = GDN_CONV_DIM // vw
    nv = 2 * hpb
    return pl.pallas_call(
        functools.partial(_gdn_kernel, seq=seq, chunk=C, valid=valid, hpb=hpb, unroll=unroll),
        grid=(batch, GDN_HK // hpb),
        in_specs=[pl.BlockSpec((seq, qw), lambda b, h: (b, h)),
                  pl.BlockSpec((seq, qw), lambda b, h: (b, kblk + h)),
                  pl.BlockSpec((seq, vw), lambda b, h: (b, vblk + h)),
                  pl.BlockSpec((seq, vw), lambda b, h: (b, zblk + h)),
                  pl.BlockSpec((seq, LANES), lambda b, h: (b, 0)),
                  pl.BlockSpec((1, 8, qw), lambda b, h: (b, 0, h)),
                  pl.BlockSpec((1, 8, qw), lambda b, h: (b, 0, kblk + h)),
                  pl.BlockSpec((1, 8, vw), lambda b, h: (b, 0, vblk + h)),
                  pl.BlockSpec((CONV_W, qw), lambda b, h: (0, h)),
                  pl.BlockSpec((CONV_W, qw), lambda b, h: (0, kblk + h)),
                  pl.BlockSpec((CONV_W, vw), lambda b, h: (0, vblk + h)),
                  pl.BlockSpec((1, LANES), lambda b, h: (0, 0)),
                  pl.BlockSpec((1, LANES), lambda b, h: (0, 0)),
                  pl.BlockSpec((1, GDN_DV), lambda b, h: (0, 0)),
                  pl.BlockSpec((1, nv, GDN_DK, GDN_DV), lambda b, h: (b, h, 0, 0))],
        out_specs=[pl.BlockSpec((seq, vw), lambda b, h: (b, h)),
                   pl.BlockSpec((1, nv, GDN_DK, GDN_DV), lambda b, h: (b, h, 0, 0))],
        out_shape=[jax.ShapeDtypeStruct((batch * seq, GDN_VD), BF16),
                   jax.ShapeDtypeStruct((batch, GDN_HV, GDN_DK, GDN_DV), F32)],
        scratch_shapes=[pltpu.VMEM((seq, LANES), F32), pltpu.VMEM((seq, LANES), F32),
                        pltpu.VMEM((N, nv * C, GDN_DV), F32), pltpu.VMEM((N, nv * C, GDN_DK), BF16),
                        pltpu.VMEM((N, nv * C, nv * C), BF16), pltpu.VMEM((N, nv * C, GDN_DK), BF16),
                        pltpu.VMEM((nv, N, GDN_DK, C), BF16), pltpu.VMEM((nv, N, 8, LANES), F32)],
        compiler_params=_cparams("parallel", "parallel"),
    )(main, main, main, main, tail, conv0p, conv0p, conv0p, conv_w, conv_w, conv_w,
      alog, dtb, norm_w.reshape(1, GDN_DV), s0)


def l2norm(x):
    return x * lax.rsqrt(jnp.sum(x * x, -1, keepdims=True) + NORM_EPS)


def gdn_chunked(q, k, v, g, beta, s0):
    B, L, H, _ = q.shape
    DV = v.shape[-1]
    C = min(GDN_CHUNK, L)
    N = L // C

    def chunks(t):
        t = t.reshape((B, N, C, H) + t.shape[3:])
        return jnp.moveaxis(t, (1, 3), (0, 2))

    q, k, v, g, beta = chunks(q), chunks(k), chunks(v), chunks(g), chunks(beta)
    gc = jnp.cumsum(g, axis=-1)
    incl = jnp.tril(jnp.ones((C, C), bool))
    strict = jnp.tril(jnp.ones((C, C), bool), -1)
    decay = jnp.exp(jnp.where(incl, gc[..., :, None] - gc[..., None, :], -jnp.inf))
    kb = k * beta[..., None]
    vb = v * beta[..., None]
    lmat = jnp.where(strict, jnp.einsum('nbhcd,nbhed->nbhce', kb, k) * decay, 0.0)
    eye = jnp.eye(C, dtype=lmat.dtype)
    tmat = lax.linalg.triangular_solve(lmat + eye, jnp.broadcast_to(eye, lmat.shape),
                                       left_side=True, lower=True, unit_diagonal=True)
    u = jnp.einsum('nbhce,nbhed->nbhcd', tmat, vb)
    w = jnp.einsum('nbhce,nbhed->nbhcd', tmat, kb * jnp.exp(gc)[..., None])
    qk = jnp.einsum('nbhcd,nbhed->nbhce', q, k) * decay

    def step(s, xs):
        q_i, k_i, u_i, w_i, gc_i, qk_i = xs
        v_new = u_i - jnp.einsum('bhcd,bhde->bhce', w_i, s)
        o = (jnp.einsum('bhcd,bhde->bhce', q_i * jnp.exp(gc_i)[..., None], s)
             + jnp.einsum('bhcs,bhse->bhce', qk_i, v_new))
        g_last = gc_i[..., -1:]
        s = (s * jnp.exp(g_last)[..., None]
             + jnp.einsum('bhcd,bhce->bhde', k_i * jnp.exp(g_last - gc_i)[..., None], v_new))
        return s, o

    s, o = lax.scan(step, s0, (q, k, u, w, gc, qk))
    o = jnp.moveaxis(o, (0, 2), (1, 3)).reshape(B, N * C, H, DV)
    return o, s


def gdn_sample(main, tail, s0, conv0, conv_w, a_log, dt_bias, norm_w):
    B, L = conv0.shape[0], main.shape[0] // conv0.shape[0]
    main = main.reshape(B, L, GDN_MAIN)
    tail = tail.reshape(B, L, LANES)
    qkv = main[..., :GDN_CONV_DIM]
    z = main[..., GDN_CONV_DIM:]
    b = tail[..., :GDN_HV]
    a = tail[..., GDN_HV:2 * GDN_HV]
    xc = jnp.concatenate([conv0, qkv], axis=1)
    conv = xc[:, 0:L] * conv_w[0]
    for j in range(1, CONV_W):
        conv = conv + xc[:, j:j + L] * conv_w[j]
    new_conv = xc[:, L:]
    conv = jax.nn.silu(conv)
    q = conv[..., :GDN_KD].reshape(B, L, GDN_HK, GDN_DK)
    k = conv[..., GDN_KD:2 * GDN_KD].reshape(B, L, GDN_HK, GDN_DK)
    v = conv[..., 2 * GDN_KD:].reshape(B, L, GDN_HV, GDN_DV)
    rep = GDN_HV // GDN_HK
    q = jnp.repeat(l2norm(q) * (GDN_DK ** -0.5), rep, axis=2)
    k = jnp.repeat(l2norm(k), rep, axis=2)
    beta = jax.nn.sigmoid(b)
    g = -jnp.exp(a_log) * jax.nn.softplus(a + dt_bias)
    o, s = gdn_chunked(q, k, v, g, beta, s0)
    zf = z.reshape(B, L, GDN_HV, GDN_DV)
    o = o * lax.rsqrt(jnp.mean(o * o, -1, keepdims=True) + NORM_EPS) * norm_w * jax.nn.silu(zf)
    return o.reshape(B * L, GDN_VD).astype(BF16), s, new_conv


def index_topk(qi, wi, ki, qpos, topk):
    s = jnp.einsum('bqhd,bsd->bqhs', qi, ki)
    score = jnp.einsum('bqhs,bqh->bqs', jax.nn.relu(s), wi)
    valid = jnp.arange(ki.shape[1])[None, :] <= qpos[:, None]
    score = jnp.where(valid[None], score, -jnp.inf)
    _, idx = lax.top_k(score, topk)
    return idx


def sparse_attend(q, k_sel, v_sel, sel_valid):
    B, Q = q.shape[:2]
    qg = q.reshape(B, Q, N_KV_HEADS, KV_GROUP, HEAD_DIM)
    s = jnp.einsum('bqngd,bqknd->bqngk', qg, k_sel) * (HEAD_DIM ** -0.5)
    s = jnp.where(sel_valid[:, :, None, None, :], s, -jnp.inf)
    p = jax.nn.softmax(s, axis=-1)
    o = jnp.einsum('bqngk,bqknd->bqngd', p, v_sel)
    return o.reshape(B, Q, DSA_QD)


def gather_rows(t, idx):
    return jax.vmap(lambda tb, ib: tb[ib])(t, idx)


def dsa_sample(pp, wi, ck, cv, cki, page_table):
    B = page_table.shape[0]
    L = pp.shape[0] // B
    past = page_table.shape[1] * PAGE_SIZE
    pos = past + jnp.arange(L)
    o1, o2, o3, o4 = DSA_QD, DSA_QD + DSA_KVD, DSA_QD + 2 * DSA_KVD, DSA_QD + 2 * DSA_KVD + IDX_HEADS * IDX_DIM
    pp = pp.reshape(B, L, DSA_MAIN)
    q = pp[..., :o1].reshape(B, L, N_HEADS, HEAD_DIM)
    k = pp[..., o1:o2].reshape(B, L, N_KV_HEADS, HEAD_DIM)
    v = pp[..., o2:o3].reshape(B, L, N_KV_HEADS, HEAD_DIM)
    qi = pp[..., o3:o4].reshape(B, L, IDX_HEADS, IDX_DIM)
    ki = pp[..., o4:]
    wi = wi.reshape(B, L, LANES)[..., :IDX_HEADS]
    ki_past = cki[page_table].reshape(B, past, IDX_DIM)
    ki_all = jnp.concatenate([ki_past, ki], axis=1)
    topk = min(TOPK_MAX, (past + L) // 4)
    idx = index_topk(qi, wi, ki_all, pos, topk)
    from_past = (idx < past)[..., None, None]
    pidx = jnp.minimum(idx, past - 1)
    phys_page = jnp.take_along_axis(page_table, (pidx // PAGE_SIZE).reshape(B, -1), axis=1).reshape(pidx.shape)
    phys = phys_page * PAGE_SIZE + pidx % PAGE_SIZE
    nidx = jnp.clip(idx - past, 0, L - 1)
    ck_flat = ck.reshape(-1, N_KV_HEADS, HEAD_DIM)
    cv_flat = cv.reshape(-1, N_KV_HEADS, HEAD_DIM)
    k_sel = jnp.where(from_past, ck_flat[phys], gather_rows(k, nidx))
    v_sel = jnp.where(from_past, cv_flat[phys], gather_rows(v, nidx))
    o = sparse_attend(q, k_sel, v_sel, idx <= pos[None, :, None])
    return o.reshape(B * L, DSA_QD).astype(BF16), k, v, ki


def _expert_kernel(be_ref, x_ref, w1_ref, w3_ref, w2_ref, o_ref):
    del be_ref
    x = x_ref[...].astype(BF16)
    h1 = jnp.dot(x, w1_ref[...].astype(BF16), preferred_element_type=F32)
    h3 = jnp.dot(x, w3_ref[...].astype(BF16), preferred_element_type=F32)
    h = _silu(h1) * h3
    o_ref[...] = jnp.dot(h.astype(BF16), w2_ref[...].astype(BF16), preferred_element_type=F32)


def _combine_ln_kernel(x_ref, y0_ref, y1_ref, gate_ref, g_ref, b_ref, o_ref):
    gate = gate_ref[...]
    h = ALPHA * x_ref[...] + (y0_ref[...] * gate[:, 0:1] + y1_ref[...] * gate[:, 1:2])
    mu = jnp.mean(h, -1, keepdims=True)
    hc = h - mu
    var = jnp.mean(hc * hc, -1, keepdims=True)
    o_ref[...] = hc * lax.rsqrt(var + LN_EPS) * g_ref[...] + b_ref[...]


def moe_layer(xt, wg, bg, we, be, w1, w3, w2, layer, ln_g, ln_b, blk=128):
    T, D = xt.shape
    E = N_EXPERTS
    K = TOPK_EXPERTS
    xb16 = xt.astype(BF16)
    lg = jnp.dot(xb16, wg.astype(BF16), preferred_element_type=F32) + bg
    pg = jax.nn.softmax(lg, axis=-1)
    gsel = jnp.argmax(lg, axis=-1)
    le = (jnp.dot(xb16, we.astype(BF16), preferred_element_type=F32) + be).reshape(T, N_GROUPS, EXPERTS_PER_GROUP)
    le_g = jnp.take_along_axis(le, gsel[:, None, None], axis=1)[:, 0]
    pe = jax.nn.softmax(le_g, axis=-1)
    top_p, top_i = lax.top_k(pe, K)
    gate = top_p / jnp.sum(top_p, -1, keepdims=True) * jnp.take_along_axis(pg, gsel[:, None], axis=1)
    eidx = (gsel[:, None] * EXPERTS_PER_GROUP + top_i).astype(I32)

    A = T * K
    nb = A // blk + E
    cnt = jnp.sum((eidx[:, :, None] == jnp.arange(E, dtype=I32)).astype(I32), axis=1)
    cum = jnp.cumsum(cnt, axis=0) - cnt
    counts = jnp.sum(cnt, axis=0)
    padded = (counts + blk - 1) // blk * blk
    pend = jnp.cumsum(padded)
    pstart = pend - padded
    dest = pstart[eidx] + jnp.take_along_axis(cum, eidx, axis=1)
    tok = jnp.broadcast_to(jnp.arange(T, dtype=I32)[:, None], (T, K))
    slot_tok = jnp.full((nb * blk,), T, I32).at[dest.reshape(-1)].set(tok.reshape(-1))
    blk_start = jnp.arange(nb, dtype=I32) * blk
    blk_e = jnp.minimum(jnp.sum((pend[None, :] <= blk_start[:, None]).astype(I32), axis=1), E - 1).astype(I32)
    xpad = jnp.concatenate([xt, jnp.zeros((1, D), xt.dtype)], axis=0)
    xb = xpad[slot_tok]

    yb = pl.pallas_call(
        _expert_kernel,
        grid_spec=pltpu.PrefetchScalarGridSpec(
            num_scalar_prefetch=1,
            grid=(nb,),
            in_specs=[pl.BlockSpec((blk, D), lambda i, be_: (i, 0)),
                      pl.BlockSpec((None, None, D, D_EXPERT), lambda i, be_: (layer, be_[i], 0, 0)),
                      pl.BlockSpec((None, None, D, D_EXPERT), lambda i, be_: (layer, be_[i], 0, 0)),
                      pl.BlockSpec((None, None, D_EXPERT, D), lambda i, be_: (layer, be_[i], 0, 0))],
            out_specs=pl.BlockSpec((blk, D), lambda i, be_: (i, 0))),
        out_shape=jax.ShapeDtypeStruct((nb * blk, D), F32),
        compiler_params=_cparams("arbitrary"),
    )(blk_e, xb, w1, w3, w2)

    y0 = yb[dest[:, 0]]
    y1 = yb[dest[:, 1]]
    tm = _row_tile(T, 344)
    return pl.pallas_call(
        _combine_ln_kernel,
        grid=(T // tm,),
        in_specs=[pl.BlockSpec((tm, D), lambda i: (i, 0)),
                  pl.BlockSpec((tm, D), lambda i: (i, 0)),
                  pl.BlockSpec((tm, D), lambda i: (i, 0)),
                  pl.BlockSpec((tm, K), lambda i: (i, 0)),
                  pl.BlockSpec((1, D), lambda i: (0, 0)),
                  pl.BlockSpec((1, D), lambda i: (0, 0))],
        out_specs=pl.BlockSpec((tm, D), lambda i: (i, 0)),
        out_shape=jax.ShapeDtypeStruct((T, D), F32),
        compiler_params=_cparams("parallel"),
    )(xt, y0, y1, gate, ln_g.reshape(1, D), ln_b.reshape(1, D))


def _rope_tables(pos):
    half = HEAD_DIM // 2
    inv = jnp.power(ROPE_THETA, -jnp.arange(half, dtype=F32) / half)
    ang = pos.astype(F32)[:, None] * inv[None, :]
    cos = jnp.cos(ang)
    sin = jnp.sin(ang)
    return jnp.concatenate([cos, cos], -1), jnp.concatenate([-sin, sin], -1)


def kernel(x_prompt, x_sample, state_gdn_s, state_gdn_conv, cache_k, cache_v, cache_kidx, page_table,
           gdn_w_in, gdn_conv_w, gdn_a_log, gdn_dt_bias, gdn_norm_w, gdn_w_out,
           dsa_w_in, dsa_w_out, ln1_g, ln1_b, ln2_g, ln2_b,
           moe_wg, moe_bg, moe_we, moe_be, moe_w1, moe_w3, moe_w2):
    B, L, D = x_prompt.shape
    BS, LS, _ = x_sample.shape
    TP = B * L
    past = page_table.shape[1] * PAGE_SIZE
    x = jnp.concatenate([x_prompt.reshape(TP, D), x_sample.reshape(BS * LS, D)], axis=0)
    pos = jnp.concatenate([jnp.tile(jnp.arange(L), B), jnp.tile(past + jnp.arange(LS), BS)])
    cos, sin = _rope_tables(pos)
    nq = DSA_QD // HEAD_DIM
    nkv = DSA_KVD // HEAD_DIM
    rope_ranges = ((0, nq + nkv), (nq + 2 * nkv, nq + 2 * nkv + IDX_HEADS + 1))
    wi_scale = IDX_HEADS ** -0.5 * IDX_DIM ** -0.5

    p_s, p_c, s_s, s_c = [], [], [], []
    p_k, p_v, p_ki, s_k, s_v, s_ki = [], [], [], [], [], []
    for i in range(DEPTH):
        j = i // N_MIXERS
        if i % N_MIXERS == 0:
            main, tail = project(x, gdn_w_in, j, GDN_MAIN, tn=512, tm_cap=688)
            gp = (gdn_conv_w[j], gdn_a_log[j], gdn_dt_bias[j], gdn_norm_w[j])
            c0 = jnp.zeros((B, CONV_W - 1, GDN_CONV_DIM), F32)
            s0 = jnp.zeros((B, GDN_HV, GDN_DK, GDN_DV), F32)
            op, sp = gdn_core(main, tail, c0, *gp, s0, B, L)
            cp = jnp.stack([main[b * L + L - (CONV_W - 1):(b + 1) * L, :GDN_CONV_DIM] for b in range(B)])
            pad_rows = ((0, 0), (0, GDN_CHUNK - LS), (0, 0))
            main_s = jnp.pad(main[TP:].reshape(BS, LS, GDN_MAIN), pad_rows).reshape(BS * GDN_CHUNK, GDN_MAIN)
            tail_s = jnp.pad(tail[TP:].reshape(BS, LS, LANES), pad_rows).reshape(BS * GDN_CHUNK, LANES)
            osp, ss = gdn_core(main_s, tail_s, state_gdn_conv[j], *gp, state_gdn_s[j], BS, GDN_CHUNK, valid=LS)
            os_ = osp.reshape(BS, GDN_CHUNK, GDN_VD)[:, :LS].reshape(BS * LS, GDN_VD)
            cs = main[TP:, :GDN_CONV_DIM].reshape(BS, LS, GDN_CONV_DIM)[:, LS - (CONV_W - 1):]
            p_s.append(sp); p_c.append(cp); s_s.append(ss); s_c.append(cs)
            w_out = gdn_w_out
        else:
            pp, wi = project(x, dsa_w_in, j, DSA_MAIN, tn=HEAD_DIM, tm_cap=1376, cos=cos, sin=sin,
                             rope_ranges=rope_ranges, tail_scale=wi_scale)
            op = dsa_prompt_attend(pp, wi, B, L)
            os_ = dsa_sample_attend(pp, wi, cache_k, cache_v, cache_kidx, page_table, j, TP, BS, LS).astype(BF16)
            kn = pp[TP:, DSA_QD:DSA_QD + DSA_KVD].reshape(BS, LS, N_KV_HEADS, HEAD_DIM)
            vn = pp[TP:, DSA_QD + DSA_KVD:DSA_QD + 2 * DSA_KVD].reshape(BS, LS, N_KV_HEADS, HEAD_DIM)
            kin = pp[TP:, DSA_MAIN - IDX_DIM:].reshape(BS, LS, IDX_DIM)
            p_k.append(pp[:TP, DSA_QD:DSA_QD + DSA_KVD].reshape(B, L, N_KV_HEADS, HEAD_DIM))
            p_v.append(pp[:TP, DSA_QD + DSA_KVD:DSA_QD + 2 * DSA_KVD].reshape(B, L, N_KV_HEADS, HEAD_DIM))
            p_ki.append(pp[:TP, DSA_MAIN - IDX_DIM:].reshape(B, L, IDX_DIM))
            s_k.append(kn); s_v.append(vn); s_ki.append(kin)
            w_out = dsa_w_out
        o_all = jnp.concatenate([op, os_], axis=0)
        x = matmul_res_ln(o_all, w_out, j, x, ln1_g[i], ln1_b[i])
        x = moe_layer(x, moe_wg[i], moe_bg[i], moe_we[i], moe_be[i], moe_w1, moe_w3, moe_w2, i,
                      ln2_g[i], ln2_b[i])
    xp = x[:TP].reshape(B, L, D)
    xs = x[TP:].reshape(BS, LS, D)
    return (xp, xs, jnp.stack(p_s), jnp.stack(p_c), jnp.stack(p_k), jnp.stack(p_v), jnp.stack(p_ki),
            jnp.stack(s_s), jnp.stack(s_c), jnp.stack(s_k), jnp.stack(s_v), jnp.stack(s_ki))
```

```python
import functools

import jax
import jax.numpy as jnp
from jax import lax
from jax.experimental import pallas as pl
from jax.experimental.pallas import tpu as pltpu

D_MODEL = 2048
DEPTH = 4
PAGE_SIZE = 128
N_MIXERS = 2
GDN_DK = 128
GDN_DV = 128
GDN_HK = D_MODEL // GDN_DK
GDN_HV = 2 * GDN_HK
GDN_KD = GDN_HK * GDN_DK
GDN_VD = GDN_HV * GDN_DV
GDN_CONV_DIM = 2 * GDN_KD + GDN_VD
GDN_MAIN = GDN_CONV_DIM + GDN_VD
CONV_W = 4
GDN_CHUNK = 64
HEAD_DIM = 128
N_HEADS = D_MODEL // HEAD_DIM
N_KV_HEADS = 4
KV_GROUP = N_HEADS // N_KV_HEADS
IDX_HEADS = 16
IDX_DIM = 128
DSA_QD = N_HEADS * HEAD_DIM
DSA_KVD = N_KV_HEADS * HEAD_DIM
DSA_MAIN = DSA_QD + 2 * DSA_KVD + IDX_HEADS * IDX_DIM + IDX_DIM
TOPK_MAX = 256
ROPE_THETA = 10000.0
N_GROUPS = 4
EXPERTS_PER_GROUP = 8
N_EXPERTS = N_GROUPS * EXPERTS_PER_GROUP
TOPK_EXPERTS = 2
D_EXPERT = D_MODEL // 4
ALPHA = (2 * DEPTH) ** 0.25
LN_EPS = 1e-5
NORM_EPS = 1e-6

LANES = 128
F32 = jnp.float32
BF16 = jnp.bfloat16
I32 = jnp.int32
VMEM_LIMIT = 56 * 1024 * 1024
INT_MIN = -2 ** 31
NEG_INF = float("-inf")


def _cparams(*sem):
    return pltpu.CompilerParams(dimension_semantics=sem, vmem_limit_bytes=VMEM_LIMIT)


def _bdot(a, b):
    return jnp.dot(a.astype(BF16), b.astype(BF16), preferred_element_type=F32)


def _bdot_nt(a, b):
    return lax.dot_general(a.astype(BF16), b.astype(BF16), (((1,), (1,)), ((), ())),
                           preferred_element_type=F32)


def _row_tile(T, cap):
    if T <= cap:
        return T
    best = None
    for t in range(16, cap + 1, 16):
        if T % t == 0:
            best = t
    assert best is not None, T
    return best


def _proj_kernel(x_ref, w_ref, wt_ref, *rest, rope_ranges, tail_scale):
    if rope_ranges:
        cos_ref, sin_ref, o_ref, t_ref = rest
    else:
        o_ref, t_ref = rest
    j = pl.program_id(1)
    xb = x_ref[...].astype(BF16)
    acc = jnp.dot(xb, w_ref[...].astype(BF16), preferred_element_type=F32)
    if rope_ranges:
        roped = acc * cos_ref[...] + pltpu.roll(acc, HEAD_DIM // 2, 1) * sin_ref[...]
        is_rope = (j >= rope_ranges[0][0]) & (j < rope_ranges[0][1])
        for lo, hi in rope_ranges[1:]:
            is_rope = is_rope | ((j >= lo) & (j < hi))
        acc = jnp.where(is_rope, roped, acc)
    o_ref[...] = acc

    @pl.when(j == 0)
    def _():
        t_ref[...] = jnp.dot(xb, wt_ref[...].astype(BF16), preferred_element_type=F32) * tail_scale


def project(x, w, layer, n_main, tn, tm_cap, cos=None, sin=None, rope_ranges=(), tail_scale=1.0):
    T, D = x.shape
    tm = _row_tile(T, tm_cap)
    n_tail = w.shape[2] - n_main
    w_tail = jnp.pad(w[layer, :, n_main:], ((0, 0), (0, LANES - n_tail)))
    in_specs = [pl.BlockSpec((tm, D), lambda i, j: (i, 0)),
                pl.BlockSpec((None, D, tn), lambda i, j: (layer, 0, j)),
                pl.BlockSpec((D, LANES), lambda i, j: (0, 0))]
    args = [x, w, w_tail]
    if rope_ranges:
        assert tn == HEAD_DIM
        in_specs += [pl.BlockSpec((tm, LANES), lambda i, j: (i, 0))] * 2
        args += [cos, sin]
    return pl.pallas_call(
        functools.partial(_proj_kernel, rope_ranges=tuple(rope_ranges), tail_scale=tail_scale),
        grid=(T // tm, n_main // tn),
        in_specs=in_specs,
        out_specs=[pl.BlockSpec((tm, tn), lambda i, j: (i, j)),
                   pl.BlockSpec((tm, LANES), lambda i, j: (i, 0))],
        out_shape=[jax.ShapeDtypeStruct((T, n_main), F32), jax.ShapeDtypeStruct((T, LANES), F32)],
        compiler_params=_cparams("parallel", "arbitrary"),
    )(*args)


def _mm_res_ln_kernel(x_ref, w_ref, r_ref, g_ref, b_ref, o_ref, *, nk):
    k = pl.program_id(1)
    part = jnp.dot(x_ref[...].astype(BF16), w_ref[...].astype(BF16), preferred_element_type=F32)

    @pl.when(k == 0)
    def _():
        o_ref[...] = part

    @pl.when(k > 0)
    def _():
        o_ref[...] += part

    @pl.when(k == nk - 1)
    def _():
        h = ALPHA * r_ref[...] + o_ref[...]
        mu = jnp.mean(h, -1, keepdims=True)
        hc = h - mu
        var = jnp.mean(hc * hc, -1, keepdims=True)
        o_ref[...] = hc * lax.rsqrt(var + LN_EPS) * g_ref[...] + b_ref[...]


def matmul_res_ln(x, w, layer, resid, g, b, tm_cap=688, tk=512):
    T, K = x.shape
    D = w.shape[2]
    tm = _row_tile(T, tm_cap)
    nk = K // tk
    return pl.pallas_call(
        functools.partial(_mm_res_ln_kernel, nk=nk),
        grid=(T // tm, nk),
        in_specs=[pl.BlockSpec((tm, tk), lambda i, k: (i, k)),
                  pl.BlockSpec((None, tk, D), lambda i, k: (layer, k, 0)),
                  pl.BlockSpec((tm, D), lambda i, k: (i, 0)),
                  pl.BlockSpec((1, D), lambda i, k: (0, 0)),
                  pl.BlockSpec((1, D), lambda i, k: (0, 0))],
        out_specs=pl.BlockSpec((tm, D), lambda i, k: (i, 0)),
        out_shape=jax.ShapeDtypeStruct((T, D), F32),
        compiler_params=_cparams("parallel", "arbitrary"),
    )(x, w, resid, g.reshape(1, D), b.reshape(1, D))


def _dsa_prompt_kernel(q_ref, qi0_ref, qi1_ref, wi_ref, k_ref, v_ref, ki_ref, o_ref,
                       kbf, vbf, kibf, key_s, bias_s, *, tq, seq, topk, s_step):
    i = pl.program_id(1)

    @pl.when(i == 0)
    def _():
        kbf[...] = k_ref[...].astype(BF16)
        vbf[...] = v_ref[...].astype(BF16)
        kibf[...] = ki_ref[...].astype(BF16)

    half = IDX_HEADS // 2
    qi_rows = [qi0_ref[:, h * IDX_DIM:(h + 1) * IDX_DIM].astype(BF16) for h in range(half)]
    qi_rows += [qi1_ref[:, h * IDX_DIM:(h + 1) * IDX_DIM].astype(BF16) for h in range(half)]
    qi_stack = jnp.concatenate(qi_rows, axis=0)
    wib = wi_ref[...].astype(BF16).astype(F32)
    q_rows = [jnp.concatenate([q_ref[:, (n * KV_GROUP + g) * HEAD_DIM:(n * KV_GROUP + g + 1) * HEAD_DIM]
                               for g in range(KV_GROUP)], axis=0).astype(BF16)
              for n in range(N_KV_HEADS)]

    def body(S):
        qpos = i * tq + lax.broadcasted_iota(I32, (tq, 1), 0)
        for c0 in range(0, S, s_step):
            s = _bdot_nt(qi_stack, kibf[c0:c0 + s_step, :])
            r = jnp.maximum(s, 0.0).astype(BF16).astype(F32)
            score = r[0:tq] * wib[:, 0:1]
            for h in range(1, IDX_HEADS):
                score = score + r[h * tq:(h + 1) * tq] * wib[:, h:h + 1]
            score = score + 0.0
            bits = pltpu.bitcast(score, I32)
            key = jnp.where(bits < 0, bits ^ jnp.int32(0x7FFFFFFF), bits)
            spos = c0 + lax.broadcasted_iota(I32, (tq, s_step), 1)
            key_s[:, c0:c0 + s_step] = jnp.where(spos <= qpos, key, jnp.int32(INT_MIN))

        def count_ge(cand):
            return jnp.sum((key_s[:, 0:S] >= cand).astype(I32), axis=1, keepdims=True)

        t0 = jnp.where(count_ge(jnp.zeros((tq, 1), I32)) >= topk, jnp.int32(0), jnp.int32(INT_MIN))
        t0 = jnp.broadcast_to(t0, (tq, 1))

        def bit_step(it, t):
            cand = t | jnp.left_shift(jnp.int32(1), 30 - it)
            return jnp.where(count_ge(cand) >= topk, cand, t)

        thr = lax.fori_loop(0, 31, bit_step, t0)

        keyv = key_s[:, 0:S]
        valid = lax.broadcasted_iota(I32, (tq, S), 1) <= qpos
        ge = keyv >= thr
        n_ge = jnp.sum((ge & valid).astype(I32), axis=1, keepdims=True)
        has_tie = jnp.max(n_ge) > topk
        bias_s[:, 0:S] = jnp.where(ge & valid, 0.0, NEG_INF)

        @pl.when(has_tie)
        def _():
            gt = keyv > thr
            n_gt = jnp.sum((gt & valid).astype(I32), axis=1, keepdims=True)
            room = (topk - n_gt).astype(F32)
            eq = ((keyv == thr) & valid)
            tri = (lax.broadcasted_iota(I32, (LANES, LANES), 0)
                   < lax.broadcasted_iota(I32, (LANES, LANES), 1)).astype(BF16)
            carry = jnp.zeros((tq, 1), F32)
            for c0 in range(0, S, LANES):
                eqc = eq[:, c0:c0 + LANES]
                before = carry + jnp.dot(eqc.astype(BF16), tri, preferred_element_type=F32)
                keep = (gt[:, c0:c0 + LANES] & valid[:, c0:c0 + LANES]) | (eqc & (before < room))
                bias_s[:, c0:c0 + LANES] = jnp.where(keep, 0.0, NEG_INF)
                carry = carry + jnp.sum(eqc.astype(F32), axis=1, keepdims=True)

        bias = bias_s[:, 0:S]
        for n in range(N_KV_HEADS):
            s = _bdot_nt(q_rows[n], kbf[0:S, n * HEAD_DIM:(n + 1) * HEAD_DIM]) * (HEAD_DIM ** -0.5)
            s = s.reshape(KV_GROUP, tq, S) + bias[None]
            m = jnp.max(s, axis=-1, keepdims=True)
            p = jnp.exp(s - m)
            l = jnp.sum(p, axis=-1, keepdims=True)
            o = jnp.dot(p.reshape(KV_GROUP * tq, S).astype(BF16), vbf[0:S, n * HEAD_DIM:(n + 1) * HEAD_DIM],
                        preferred_element_type=F32)
            o = o.reshape(KV_GROUP, tq, HEAD_DIM) / l
            for g in range(KV_GROUP):
                h = n * KV_GROUP + g
                o_ref[:, h * HEAD_DIM:(h + 1) * HEAD_DIM] = o[g].astype(o_ref.dtype)

    n_var = seq // s_step
    per = (seq // tq) // n_var
    for c in range(n_var):
        @pl.when(i // per == c)
        def _(c=c):
            body((c + 1) * s_step)


def dsa_prompt_attend(pp, wi, batch, seq, tq=128, s_step=512):
    topk = min(TOPK_MAX, seq // 4)
    s_step = min(s_step, seq)
    nq = seq // tq
    kcol = DSA_QD // DSA_KVD
    qicol = (DSA_QD + 2 * DSA_KVD) // (IDX_HEADS * IDX_DIM // 2)
    kicol = (DSA_QD + 2 * DSA_KVD + IDX_HEADS * IDX_DIM) // IDX_DIM
    assert (DSA_QD + 2 * DSA_KVD) % (IDX_HEADS * IDX_DIM // 2) == 0
    hq = IDX_HEADS * IDX_DIM // 2
    return pl.pallas_call(
        functools.partial(_dsa_prompt_kernel, tq=tq, seq=seq, topk=topk, s_step=s_step),
        grid=(batch, nq),
        in_specs=[pl.BlockSpec((tq, DSA_QD), lambda b, i: (b * nq + i, 0)),
                  pl.BlockSpec((tq, hq), lambda b, i: (b * nq + i, qicol)),
                  pl.BlockSpec((tq, hq), lambda b, i: (b * nq + i, qicol + 1)),
                  pl.BlockSpec((tq, LANES), lambda b, i: (b * nq + i, 0)),
                  pl.BlockSpec((seq, DSA_KVD), lambda b, i: (b, kcol)),
                  pl.BlockSpec((seq, DSA_KVD), lambda b, i: (b, kcol + 1)),
                  pl.BlockSpec((seq, IDX_DIM), lambda b, i: (b, kicol))],
        out_specs=pl.BlockSpec((tq, DSA_QD), lambda b, i: (b * nq + i, 0)),
        out_shape=jax.ShapeDtypeStruct((batch * seq, DSA_QD), BF16),
        scratch_shapes=[pltpu.VMEM((seq, DSA_KVD), BF16), pltpu.VMEM((seq, DSA_KVD), BF16),
                        pltpu.VMEM((seq, IDX_DIM), BF16),
                        pltpu.VMEM((tq, seq), I32), pltpu.VMEM((tq, seq), F32)],
        compiler_params=_cparams("parallel", "arbitrary"),
    )(pp, pp, pp, wi, pp, pp, pp)


def _order_key(score):
    bits = pltpu.bitcast(score + 0.0, I32)
    return jnp.where(bits < 0, bits ^ jnp.int32(0x7FFFFFFF), bits)


def _dsa_sample_kernel(pt_ref, q_ref, qi0_ref, qi1_ref, wi_ref, kn_ref, vn_ref, kin_ref, ck_hbm, cv_hbm, cki_hbm,
                       o_ref, kibuf, kvbuf, key_s, bias_s, sc_s, p_s, sem_ki, sem_kv,
                       *, layer, ls, n_pages, ppc, topk):
    b = pl.program_id(0)
    past = n_pages * PAGE_SIZE
    S = past + LANES
    ck = ppc * PAGE_SIZE
    prow = PAGE_SIZE * N_KV_HEADS
    n_chunks = n_pages // ppc
    gk = 4 * PAGE_SIZE
    rows = KV_GROUP * ls

    def ki_copy(p):
        return pltpu.make_async_copy(cki_hbm.at[layer, pt_ref[b, p]], kibuf.at[p], sem_ki)

    def kv_copy(src, c, slot, i):
        return pltpu.make_async_copy(src.at[layer, pt_ref[b, c * ppc + i]],
                                     kvbuf.at[slot, pl.ds(i * prow, prow)], sem_kv.at[slot])

    def head_rows(slot, n):
        return kvbuf[slot, pl.ds(n, ck, stride=N_KV_HEADS), :].astype(BF16)

    def start_chunk(src, c, slot):
        for i in range(ppc):
            kv_copy(src, c, slot, i).start()

    def wait_chunk(src, c, slot):
        for i in range(ppc):
            kv_copy(src, c, slot, i).wait()

    def ki_start(p, carry):
        ki_copy(p).start()
        return carry

    def ki_wait(p, carry):
        ki_copy(p).wait()
        return carry

    lax.fori_loop(0, n_pages, ki_start, 0)
    start_chunk(ck_hbm, 0, 0)

    half = IDX_HEADS // 2
    qi_stack = jnp.concatenate([qi0_ref[:, h * IDX_DIM:(h + 1) * IDX_DIM] for h in range(half)]
                               + [qi1_ref[:, h * IDX_DIM:(h + 1) * IDX_DIM] for h in range(half)],
                               axis=0).astype(BF16)
    wib = wi_ref[...].astype(BF16).astype(F32)
    q_rows = [jnp.concatenate([q_ref[:, (n * KV_GROUP + g) * HEAD_DIM:(n * KV_GROUP + g + 1) * HEAD_DIM]
                               for g in range(KV_GROUP)], axis=0).astype(BF16)
              for n in range(N_KV_HEADS)]
    qpos = past + lax.broadcasted_iota(I32, (ls, 1), 0)
    zpad = jnp.zeros((LANES - ls, DSA_KVD), F32)

    def index_keys(ki_rows):
        s = _bdot_nt(qi_stack, ki_rows)
        r = jnp.maximum(s, 0.0).astype(BF16).astype(F32)
        score = r[0:ls] * wib[:, 0:1]
        for h in range(1, IDX_HEADS):
            score = score + r[h * ls:(h + 1) * ls] * wib[:, h:h + 1]
        return _order_key(score)

    lax.fori_loop(0, n_pages, ki_wait, 0)

    def index_step(g, carry):
        kic = kibuf[pl.ds(g * (gk // PAGE_SIZE), gk // PAGE_SIZE)].reshape(gk, IDX_DIM)
        key_s[:, pl.ds(pl.multiple_of(g * gk, gk), gk)] = index_keys(kic)
        return carry

    lax.fori_loop(0, past // gk, index_step, 0)
    kin_pad = jnp.concatenate([kin_ref[...], zpad[:, 0:IDX_DIM]], axis=0)
    new_pos = past + lax.broadcasted_iota(I32, (ls, LANES), 1)
    key_s[:, past:S] = jnp.where(new_pos <= qpos, index_keys(kin_pad), jnp.int32(INT_MIN))

    def count_ge(cand):
        return jnp.sum((key_s[...] >= cand).astype(I32), axis=1, keepdims=True)

    t0 = jnp.where(count_ge(jnp.zeros((ls, 1), I32)) >= topk, jnp.int32(0), jnp.int32(INT_MIN))

    def bit_step(it, t):
        cand = t | jnp.left_shift(jnp.int32(1), 30 - it)
        return jnp.where(count_ge(cand) >= topk, cand, t)

    thr = lax.fori_loop(0, 31, bit_step, t0)
    keyv = key_s[...]
    valid = lax.broadcasted_iota(I32, (ls, S), 1) <= qpos
    ge = (keyv >= thr) & valid
    n_ge = jnp.sum(ge.astype(I32), axis=1, keepdims=True)
    bias_s[...] = jnp.where(ge, 0.0, NEG_INF)

    @pl.when(jnp.max(n_ge) > topk)
    def _():
        n_gt = jnp.sum(((keyv > thr) & valid).astype(I32), axis=1, keepdims=True)
        room = (topk - n_gt).astype(F32)
        tri = (lax.broadcasted_iota(I32, (LANES, LANES), 0)
               < lax.broadcasted_iota(I32, (LANES, LANES), 1)).astype(BF16)

        def tie_step(c, carry):
            off = pl.multiple_of(c * LANES, LANES)
            kc = key_s[:, pl.ds(off, LANES)]
            ok = (off + lax.broadcasted_iota(I32, (ls, LANES), 1)) <= qpos
            eq = (kc == thr) & ok
            before = carry + jnp.dot(eq.astype(BF16), tri, preferred_element_type=F32)
            keep = ((kc > thr) & ok) | (eq & (before < room))
            bias_s[:, pl.ds(off, LANES)] = jnp.where(keep, 0.0, NEG_INF)
            return carry + jnp.sum(eq.astype(F32), axis=1, keepdims=True)

        lax.fori_loop(0, S // LANES, tie_step, jnp.zeros((ls, 1), F32))

    def masked_scores(n, k_rows, bias):
        s = _bdot_nt(q_rows[n], k_rows) * (HEAD_DIM ** -0.5)
        return (s.reshape(KV_GROUP, ls, s.shape[1]) + bias[None]).reshape(rows, s.shape[1])

    def k_step(c, carry):
        slot = c % 2
        wait_chunk(ck_hbm, c, slot)

        @pl.when(c + 1 < n_chunks)
        def _():
            start_chunk(ck_hbm, c + 1, 1 - slot)

        @pl.when(c + 1 == n_chunks)
        def _():
            start_chunk(cv_hbm, 0, 1 - slot)

        off = pl.multiple_of(c * ck, ck)
        bias = bias_s[:, pl.ds(off, ck)]
        for n in range(N_KV_HEADS):
            sc_s[n * rows:(n + 1) * rows, pl.ds(off, ck)] = masked_scores(n, head_rows(slot, n), bias)
        return carry

    lax.fori_loop(0, n_chunks, k_step, 0)
    kn_pad = jnp.concatenate([kn_ref[...], zpad], axis=0).astype(BF16)
    for n in range(N_KV_HEADS):
        sc_s[n * rows:(n + 1) * rows, past:S] = masked_scores(
            n, kn_pad[:, n * HEAD_DIM:(n + 1) * HEAD_DIM], bias_s[:, past:S])

    sc = sc_s[...]
    m = jnp.max(sc, axis=1, keepdims=True)
    p = jnp.exp(sc - m)
    l = jnp.sum(p, axis=1, keepdims=True)
    p_s[...] = p.astype(BF16)

    def v_step(v, acc):
        slot = (n_chunks + v) % 2
        wait_chunk(cv_hbm, v, slot)

        @pl.when(v + 1 < n_chunks)
        def _():
            start_chunk(cv_hbm, v + 1, 1 - slot)

        off = pl.multiple_of(v * ck, ck)
        return tuple(acc[n] + jnp.dot(p_s[n * rows:(n + 1) * rows, pl.ds(off, ck)], head_rows(slot, n),
                                      preferred_element_type=F32)
                     for n in range(N_KV_HEADS))

    acc = lax.fori_loop(0, n_chunks, v_step, tuple(jnp.zeros((rows, HEAD_DIM), F32) for _ in range(N_KV_HEADS)))
    vn_pad = jnp.concatenate([vn_ref[...], zpad], axis=0).astype(BF16)
    for n in range(N_KV_HEADS):
        o = acc[n] + jnp.dot(p_s[n * rows:(n + 1) * rows, past:S], vn_pad[:, n * HEAD_DIM:(n + 1) * HEAD_DIM],
                             preferred_element_type=F32)
        o = o / l[n * rows:(n + 1) * rows]
        for g in range(KV_GROUP):
            h = n * KV_GROUP + g
            o_ref[:, h * HEAD_DIM:(h + 1) * HEAD_DIM] = o[g * ls:(g + 1) * ls]


def dsa_sample_attend(pp, wi, cache_k, cache_v, cache_kidx, page_table, layer, row0, batch, ls, ppc=16):
    n_pool = cache_k.shape[1]
    n_pages = page_table.shape[1]
    past = n_pages * PAGE_SIZE
    topk = min(TOPK_MAX, (past + ls) // 4)
    assert row0 % ls == 0 and n_pages % ppc == 0 and n_pages % 4 == 0 and ls % 8 == 0
    rb = row0 // ls
    ck = cache_k.reshape(cache_k.shape[0], n_pool, PAGE_SIZE * N_KV_HEADS, HEAD_DIM)
    cv = cache_v.reshape(cache_v.shape[0], n_pool, PAGE_SIZE * N_KV_HEADS, HEAD_DIM)
    kcol = DSA_QD // DSA_KVD
    hq = IDX_HEADS * IDX_DIM // 2
    qicol = (DSA_QD + 2 * DSA_KVD) // hq
    kicol = (DSA_QD + 2 * DSA_KVD + IDX_HEADS * IDX_DIM) // IDX_DIM
    S = past + LANES
    return pl.pallas_call(
        functools.partial(_dsa_sample_kernel, layer=layer, ls=ls, n_pages=n_pages, ppc=ppc, topk=topk),
        grid_spec=pltpu.PrefetchScalarGridSpec(
            num_scalar_prefetch=1,
            grid=(batch,),
            in_specs=[pl.BlockSpec((ls, DSA_QD), lambda b, pt: (rb + b, 0)),
                      pl.BlockSpec((ls, hq), lambda b, pt: (rb + b, qicol)),
                      pl.BlockSpec((ls, hq), lambda b, pt: (rb + b, qicol + 1)),
                      pl.BlockSpec((ls, LANES), lambda b, pt: (rb + b, 0)),
                      pl.BlockSpec((ls, DSA_KVD), lambda b, pt: (rb + b, kcol)),
                      pl.BlockSpec((ls, DSA_KVD), lambda b, pt: (rb + b, kcol + 1)),
                      pl.BlockSpec((ls, IDX_DIM), lambda b, pt: (rb + b, kicol)),
                      pl.BlockSpec(memory_space=pl.ANY),
                      pl.BlockSpec(memory_space=pl.ANY),
                      pl.BlockSpec(memory_space=pl.ANY)],
            out_specs=pl.BlockSpec((ls, DSA_QD), lambda b, pt: (b, 0)),
            scratch_shapes=[pltpu.VMEM((n_pages, PAGE_SIZE, IDX_DIM), F32),
                            pltpu.VMEM((2, ppc * PAGE_SIZE * N_KV_HEADS, HEAD_DIM), F32),
                            pltpu.VMEM((ls, S), I32), pltpu.VMEM((ls, S), F32),
                            pltpu.VMEM((N_HEADS * ls, S), F32), pltpu.VMEM((N_HEADS * ls, S), BF16),
                            pltpu.SemaphoreType.DMA(()), pltpu.SemaphoreType.DMA((2,))]),
        out_shape=jax.ShapeDtypeStruct((batch * ls, DSA_QD), F32),
        compiler_params=_cparams("arbitrary"),
    )(page_table, pp, pp, pp, wi, pp, pp, pp, ck, cv, cache_kidx)


def _split3(a):
    hi = a.astype(BF16)
    lo = (a - hi.astype(F32)).astype(BF16)
    return hi, lo


def _dot3(a_parts, b_parts):
    ah, al = a_parts
    bh, bl = b_parts
    d = functools.partial(jnp.dot, preferred_element_type=F32)
    return d(ah, bh) + (d(ah, bl) + d(al, bh))


def _tri_inverse(mats, order):
    n = mats[0].shape[0]
    eye = (lax.broadcasted_iota(I32, (n, n), 0) == lax.broadcasted_iota(I32, (n, n), 1)).astype(F32)
    ps = [eye - a for a in mats]
    xss = [_split3(-a) for a in mats]
    steps = max(0, (order - 1).bit_length() - 1)
    for _ in range(steps):
        xss = [_split3(_dot3(xs, xs)) for xs in xss]
        ps = [p + _dot3(_split3(p), xs) for p, xs in zip(ps, xss)]
    return ps


def _silu(x):
    return x * jax.nn.sigmoid(x)


def _gdn_kernel(xq_ref, xk_ref, xv_ref, z_ref, tail_ref, cq_ref, ck_ref, cv_ref, wq_ref, wk_ref, wv_ref,
                alog_ref, dtb_ref, nw_ref, s0_ref, o_ref, s_ref,
                beta_s, g_s, u_s, w_s, qk_s, qg_s, kd_s, el_s, *, seq, chunk, valid, hpb, unroll):
    hb = pl.program_id(1)
    C = chunk
    N = seq // C
    HALO = 8
    NV = 2 * hpb
    tail = tail_ref[...]
    beta = jax.nn.sigmoid(tail)
    x = tail + dtb_ref[...]
    softplus = jnp.maximum(x, 0.0) + jnp.log1p(jnp.exp(-jnp.abs(x)))
    g = -jnp.exp(alog_ref[...]) * softplus
    if valid < seq:
        is_real = lax.broadcasted_iota(I32, (seq, LANES), 0) < valid
        beta = jnp.where(is_real, beta, 0.0)
        g = jnp.where(is_real, g, 0.0)
    beta_s[...] = beta
    g_s[...] = g

    R = NV * C
    row = lax.broadcasted_iota(I32, (R, R), 0)
    col = lax.broadcasted_iota(I32, (R, R), 1)
    log2c = C.bit_length() - 1
    same_head = lax.shift_right_logical(row, log2c) == lax.shift_right_logical(col, log2c)
    incl = same_head & (row >= col)
    strict = same_head & (row > col)
    lane = lax.broadcasted_iota(I32, (C, LANES), 1)
    sub_t = lax.broadcasted_iota(I32, (LANES, C), 0)
    rowc = lax.broadcasted_iota(I32, (C, LANES), 0)

    def conv(xref, cref, wref, r0, c):
        prev = xref[pl.ds(pl.multiple_of(jnp.maximum(r0 - HALO, 0), HALO), HALO), :]
        win = jnp.concatenate([jnp.where(c == 0, cref[0], prev), xref[pl.ds(r0, C), :]], axis=0)
        acc = win[HALO - 3:HALO - 3 + C] * wref[0:1, :]
        for j in range(1, CONV_W):
            acc = acc + win[HALO - 3 + j:HALO - 3 + j + C] * wref[j:j + 1, :]
        return _silu(acc)

    def l2n(t):
        return t * lax.rsqrt(jnp.sum(t * t, -1, keepdims=True) + NORM_EPS)

    def prep_inputs(c):
        r0 = pl.multiple_of(c * C, C)
        qc = conv(xq_ref, cq_ref, wq_ref, r0, c)
        kc = conv(xk_ref, ck_ref, wk_ref, r0, c)
        vv = conv(xv_ref, cv_ref, wv_ref, r0, c)
        beta = beta_s[pl.ds(r0, C), :]
        gc = g_s[pl.ds(r0, C), :]
        sh = 1
        while sh < C:
            gc = gc + jnp.where(rowc >= sh, pltpu.roll(gc, sh, 0), 0.0)
            sh *= 2
        gc_t = gc.T
        qn = [l2n(qc[:, hl * GDN_DK:(hl + 1) * GDN_DK]) * (GDN_DK ** -0.5) for hl in range(hpb)]
        kn = [l2n(kc[:, hl * GDN_DK:(hl + 1) * GDN_DK]) for hl in range(hpb)]
        bcols, gcols, grows = [], [], []
        for e in range(NV):
            hv = NV * hb + e
            bcols.append(jnp.sum(jnp.where(lane == hv, beta, 0.0), axis=1, keepdims=True))
            gcols.append(jnp.sum(jnp.where(lane == GDN_HV + hv, gc, 0.0), axis=1, keepdims=True))
            grows.append(jnp.sum(jnp.where(sub_t == GDN_HV + hv, gc_t, 0.0), axis=0, keepdims=True))
        bcol = jnp.concatenate(bcols, axis=0)
        gcol = jnp.concatenate(gcols, axis=0)
        grow = jnp.concatenate(grows, axis=1)
        kn_st = jnp.concatenate([kn[e // 2] for e in range(NV)], axis=0)
        qn_st = jnp.concatenate([qn[e // 2] for e in range(NV)], axis=0)
        v_st = jnp.concatenate([vv[:, e * GDN_DV:(e + 1) * GDN_DV] for e in range(NV)], axis=0)
        decay = jnp.exp(jnp.where(incl, gcol - grow, NEG_INF))
        kb = kn_st * bcol
        a = jnp.where(strict, _bdot_nt(kb, kn_st) * decay, 0.0)
        qk_s[c] = jnp.where(incl, _bdot_nt(qn_st, kn_st) * decay, 0.0).astype(BF16)
        qg_s[c] = (qn_st * jnp.exp(gcol)).astype(BF16)
        for e in range(NV):
            glast = grows[e][:, C - 1:C]
            kd = kn[e // 2] * jnp.exp(glast - gcols[e])
            kd_s[e, c] = kd.T.astype(BF16)
            el_s[e, c] = jnp.broadcast_to(jnp.exp(glast), (8, LANES))
        return a, v_st * bcol, kb * jnp.exp(gcol)

    def prep(it, carry):
        cs = [it * unroll + u for u in range(unroll)]
        ins = [prep_inputs(c) for c in cs]
        tmats = _tri_inverse([a for a, _, _ in ins], C)
        for c, tmat, (_, vb, kbg) in zip(cs, tmats, ins):
            u_s[c] = _bdot(tmat, vb)
            w_s[c] = _bdot(tmat, kbg).astype(BF16)
        return carry

    lax.fori_loop(0, N // unroll, prep, 0)

    nw = nw_ref[...]
    s_ref[...] = s0_ref[...]

    def scan(c, carry):
        r0 = pl.multiple_of(c * C, C)
        d = functools.partial(jnp.dot, preferred_element_type=F32)
        u = u_s[c]
        w = w_s[c]
        qg = qg_s[c]
        sts = [s_ref[0, e] for e in range(NV)]
        sbs = [st.astype(BF16) for st in sts]
        vb = jnp.concatenate([u[e * C:(e + 1) * C] - d(w[e * C:(e + 1) * C], sbs[e]) for e in range(NV)],
                             axis=0).astype(BF16)
        o_intra = d(qk_s[c], vb)
        for e in range(NV):
            o = d(qg[e * C:(e + 1) * C], sbs[e]) + o_intra[e * C:(e + 1) * C]
            s_ref[0, e] = sts[e] * el_s[e, c][0:1, :] + d(kd_s[e, c], vb[e * C:(e + 1) * C])
            zf = z_ref[pl.ds(r0, C), e * GDN_DV:(e + 1) * GDN_DV]
            og = o * lax.rsqrt(jnp.mean(o * o, -1, keepdims=True) + NORM_EPS) * nw * _silu(zf)
            o_ref[pl.ds(r0, C), e * GDN_DV:(e + 1) * GDN_DV] = og.astype(o_ref.dtype)
        return carry

    lax.fori_loop(0, N, scan, 0)


def gdn_core(main, tail, conv0, conv_w, a_log, dt_bias, norm_w, s0, batch, seq, valid=None, hpb=2, unroll=4):
    valid = seq if valid is None else valid
    C = min(GDN_CHUNK, seq)
    N = seq // C
    assert seq % C == 0
    unroll = unroll if N % unroll == 0 else 1
    conv0p = jnp.pad(conv0, ((0, 0), (8 - (CONV_W - 1), 0), (0, 0)))
    alog = jnp.pad(a_log, (GDN_HV, LANES - 2 * GDN_HV)).reshape(1, LANES)
    dtb = jnp.pad(dt_bias, (GDN_HV, LANES - 2 * GDN_HV)).reshape(1, LANES)
    qw = GDN_DK * hpb
    vw = 2 * GDN_DV * hpb
    kblk = GDN_KD // qw
    vblk = 2 * GDN_KD // vw
    zblk = GDN_CONV_DIM // vw
    nv = 2 * hpb
    return pl.pallas_call(
        functools.partial(_gdn_kernel, seq=seq, chunk=C, valid=valid, hpb=hpb, unroll=unroll),
        grid=(batch, GDN_HK // hpb),
        in_specs=[pl.BlockSpec((seq, qw), lambda b, h: (b, h)),
                  pl.BlockSpec((seq, qw), lambda b, h: (b, kblk + h)),
                  pl.BlockSpec((seq, vw), lambda b, h: (b, vblk + h)),
                  pl.BlockSpec((seq, vw), lambda b, h: (b, zblk + h)),
                  pl.BlockSpec((seq, LANES), lambda b, h: (b, 0)),
                  pl.BlockSpec((1, 8, qw), lambda b, h: (b, 0, h)),
                  pl.BlockSpec((1, 8, qw), lambda b, h: (b, 0, kblk + h)),
                  pl.BlockSpec((1, 8, vw), lambda b, h: (b, 0, vblk + h)),
                  pl.BlockSpec((CONV_W, qw), lambda b, h: (0, h)),
                  pl.BlockSpec((CONV_W, qw), lambda b, h: (0, kblk + h)),
                  pl.BlockSpec((CONV_W, vw), lambda b, h: (0, vblk + h)),
                  pl.BlockSpec((1, LANES), lambda b, h: (0, 0)),
                  pl.BlockSpec((1, LANES), lambda b, h: (0, 0)),
                  pl.BlockSpec((1, GDN_DV), lambda b, h: (0, 0)),
                  pl.BlockSpec((1, nv, GDN_DK, GDN_DV), lambda b, h: (b, h, 0, 0))],
        out_specs=[pl.BlockSpec((seq, vw), lambda b, h: (b, h)),
                   pl.BlockSpec((1, nv, GDN_DK, GDN_DV), lambda b, h: (b, h, 0, 0))],
        out_shape=[jax.ShapeDtypeStruct((batch * seq, GDN_VD), BF16),
                   jax.ShapeDtypeStruct((batch, GDN_HV, GDN_DK, GDN_DV), F32)],
        scratch_shapes=[pltpu.VMEM((seq, LANES), F32), pltpu.VMEM((seq, LANES), F32),
                        pltpu.VMEM((N, nv * C, GDN_DV), F32), pltpu.VMEM((N, nv * C, GDN_DK), BF16),
                        pltpu.VMEM((N, nv * C, nv * C), BF16), pltpu.VMEM((N, nv * C, GDN_DK), BF16),
                        pltpu.VMEM((nv, N, GDN_DK, C), BF16), pltpu.VMEM((nv, N, 8, LANES), F32)],
        compiler_params=_cparams("parallel", "parallel"),
    )(main, main, main, main, tail, conv0p, conv0p, conv0p, conv_w, conv_w, conv_w,
      alog, dtb, norm_w.reshape(1, GDN_DV), s0)


def l2norm(x):
    return x * lax.rsqrt(jnp.sum(x * x, -1, keepdims=True) + NORM_EPS)


def gdn_chunked(q, k, v, g, beta, s0):
    B, L, H, _ = q.shape
    DV = v.shape[-1]
    C = min(GDN_CHUNK, L)
    N = L // C

    def chunks(t):
        t = t.reshape((B, N, C, H) + t.shape[3:])
        return jnp.moveaxis(t, (1, 3), (0, 2))

    q, k, v, g, beta = chunks(q), chunks(k), chunks(v), chunks(g), chunks(beta)
    gc = jnp.cumsum(g, axis=-1)
    incl = jnp.tril(jnp.ones((C, C), bool))
    strict = jnp.tril(jnp.ones((C, C), bool), -1)
    decay = jnp.exp(jnp.where(incl, gc[..., :, None] - gc[..., None, :], -jnp.inf))
    kb = k * beta[..., None]
    vb = v * beta[..., None]
    lmat = jnp.where(strict, jnp.einsum('nbhcd,nbhed->nbhce', kb, k) * decay, 0.0)
    eye = jnp.eye(C, dtype=lmat.dtype)
    tmat = lax.linalg.triangular_solve(lmat + eye, jnp.broadcast_to(eye, lmat.shape),
                                       left_side=True, lower=True, unit_diagonal=True)
    u = jnp.einsum('nbhce,nbhed->nbhcd', tmat, vb)
    w = jnp.einsum('nbhce,nbhed->nbhcd', tmat, kb * jnp.exp(gc)[..., None])
    qk = jnp.einsum('nbhcd,nbhed->nbhce', q, k) * decay

    def step(s, xs):
        q_i, k_i, u_i, w_i, gc_i, qk_i = xs
        v_new = u_i - jnp.einsum('bhcd,bhde->bhce', w_i, s)
        o = (jnp.einsum('bhcd,bhde->bhce', q_i * jnp.exp(gc_i)[..., None], s)
             + jnp.einsum('bhcs,bhse->bhce', qk_i, v_new))
        g_last = gc_i[..., -1:]
        s = (s * jnp.exp(g_last)[..., None]
             + jnp.einsum('bhcd,bhce->bhde', k_i * jnp.exp(g_last - gc_i)[..., None], v_new))
        return s, o

    s, o = lax.scan(step, s0, (q, k, u, w, gc, qk))
    o = jnp.moveaxis(o, (0, 2), (1, 3)).reshape(B, N * C, H, DV)
    return o, s


def gdn_sample(main, tail, s0, conv0, conv_w, a_log, dt_bias, norm_w):
    B, L = conv0.shape[0], main.shape[0] // conv0.shape[0]
    main = main.reshape(B, L, GDN_MAIN)
    tail = tail.reshape(B, L, LANES)
    qkv = main[..., :GDN_CONV_DIM]
    z = main[..., GDN_CONV_DIM:]
    b = tail[..., :GDN_HV]
    a = tail[..., GDN_HV:2 * GDN_HV]
    xc = jnp.concatenate([conv0, qkv], axis=1)
    conv = xc[:, 0:L] * conv_w[0]
    for j in range(1, CONV_W):
        conv = conv + xc[:, j:j + L] * conv_w[j]
    new_conv = xc[:, L:]
    conv = jax.nn.silu(conv)
    q = conv[..., :GDN_KD].reshape(B, L, GDN_HK, GDN_DK)
    k = conv[..., GDN_KD:2 * GDN_KD].reshape(B, L, GDN_HK, GDN_DK)
    v = conv[..., 2 * GDN_KD:].reshape(B, L, GDN_HV, GDN_DV)
    rep = GDN_HV // GDN_HK
    q = jnp.repeat(l2norm(q) * (GDN_DK ** -0.5), rep, axis=2)
    k = jnp.repeat(l2norm(k), rep, axis=2)
    beta = jax.nn.sigmoid(b)
    g = -jnp.exp(a_log) * jax.nn.softplus(a + dt_bias)
    o, s = gdn_chunked(q, k, v, g, beta, s0)
    zf = z.reshape(B, L, GDN_HV, GDN_DV)
    o = o * lax.rsqrt(jnp.mean(o * o, -1, keepdims=True) + NORM_EPS) * norm_w * jax.nn.silu(zf)
    return o.reshape(B * L, GDN_VD).astype(BF16), s, new_conv


def index_topk(qi, wi, ki, qpos, topk):
    s = jnp.einsum('bqhd,bsd->bqhs', qi, ki)
    score = jnp.einsum('bqhs,bqh->bqs', jax.nn.relu(s), wi)
    valid = jnp.arange(ki.shape[1])[None, :] <= qpos[:, None]
    score = jnp.where(valid[None], score, -jnp.inf)
    _, idx = lax.top_k(score, topk)
    return idx


def sparse_attend(q, k_sel, v_sel, sel_valid):
    B, Q = q.shape[:2]
    qg = q.reshape(B, Q, N_KV_HEADS, KV_GROUP, HEAD_DIM)
    s = jnp.einsum('bqngd,bqknd->bqngk', qg, k_sel) * (HEAD_DIM ** -0.5)
    s = jnp.where(sel_valid[:, :, None, None, :], s, -jnp.inf)
    p = jax.nn.softmax(s, axis=-1)
    o = jnp.einsum('bqngk,bqknd->bqngd', p, v_sel)
    return o.reshape(B, Q, DSA_QD)


def gather_rows(t, idx):
    return jax.vmap(lambda tb, ib: tb[ib])(t, idx)


def dsa_sample(pp, wi, ck, cv, cki, page_table):
    B = page_table.shape[0]
    L = pp.shape[0] // B
    past = page_table.shape[1] * PAGE_SIZE
    pos = past + jnp.arange(L)
    o1, o2, o3, o4 = DSA_QD, DSA_QD + DSA_KVD, DSA_QD + 2 * DSA_KVD, DSA_QD + 2 * DSA_KVD + IDX_HEADS * IDX_DIM
    pp = pp.reshape(B, L, DSA_MAIN)
    q = pp[..., :o1].reshape(B, L, N_HEADS, HEAD_DIM)
    k = pp[..., o1:o2].reshape(B, L, N_KV_HEADS, HEAD_DIM)
    v = pp[..., o2:o3].reshape(B, L, N_KV_HEADS, HEAD_DIM)
    qi = pp[..., o3:o4].reshape(B, L, IDX_HEADS, IDX_DIM)
    ki = pp[..., o4:]
    wi = wi.reshape(B, L, LANES)[..., :IDX_HEADS]
    ki_past = cki[page_table].reshape(B, past, IDX_DIM)
    ki_all = jnp.concatenate([ki_past, ki], axis=1)
    topk = min(TOPK_MAX, (past + L) // 4)
    idx = index_topk(qi, wi, ki_all, pos, topk)
    from_past = (idx < past)[..., None, None]
    pidx = jnp.minimum(idx, past - 1)
    phys_page = jnp.take_along_axis(page_table, (pidx // PAGE_SIZE).reshape(B, -1), axis=1).reshape(pidx.shape)
    phys = phys_page * PAGE_SIZE + pidx % PAGE_SIZE
    nidx = jnp.clip(idx - past, 0, L - 1)
    ck_flat = ck.reshape(-1, N_KV_HEADS, HEAD_DIM)
    cv_flat = cv.reshape(-1, N_KV_HEADS, HEAD_DIM)
    k_sel = jnp.where(from_past, ck_flat[phys], gather_rows(k, nidx))
    v_sel = jnp.where(from_past, cv_flat[phys], gather_rows(v, nidx))
    o = sparse_attend(q, k_sel, v_sel, idx <= pos[None, :, None])
    return o.reshape(B * L, DSA_QD).astype(BF16), k, v, ki


def _expert_kernel(be_ref, x_ref, w1_ref, w3_ref, w2_ref, o_ref):
    del be_ref
    x = x_ref[...].astype(BF16)
    h1 = jnp.dot(x, w1_ref[...].astype(BF16), preferred_element_type=F32)
    h3 = jnp.dot(x, w3_ref[...].astype(BF16), preferred_element_type=F32)
    h = _silu(h1) * h3
    o_ref[...] = jnp.dot(h.astype(BF16), w2_ref[...].astype(BF16), preferred_element_type=F32)


def _combine_ln_kernel(x_ref, y0_ref, y1_ref, gate_ref, g_ref, b_ref, o_ref):
    gate = gate_ref[...]
    h = ALPHA * x_ref[...] + (y0_ref[...] * gate[:, 0:1] + y1_ref[...] * gate[:, 1:2])
    mu = jnp.mean(h, -1, keepdims=True)
    hc = h - mu
    var = jnp.mean(hc * hc, -1, keepdims=True)
    o_ref[...] = hc * lax.rsqrt(var + LN_EPS) * g_ref[...] + b_ref[...]


def moe_layer(xt, wg, bg, we, be, w1, w3, w2, layer, ln_g, ln_b, blk=128):
    T, D = xt.shape
    E = N_EXPERTS
    K = TOPK_EXPERTS
    xb16 = xt.astype(BF16)
    lg = jnp.dot(xb16, wg.astype(BF16), preferred_element_type=F32) + bg
    pg = jax.nn.softmax(lg, axis=-1)
    gsel = jnp.argmax(lg, axis=-1)
    le = (jnp.dot(xb16, we.astype(BF16), preferred_element_type=F32) + be).reshape(T, N_GROUPS, EXPERTS_PER_GROUP)
    le_g = jnp.take_along_axis(le, gsel[:, None, None], axis=1)[:, 0]
    pe = jax.nn.softmax(le_g, axis=-1)
    top_p, top_i = lax.top_k(pe, K)
    gate = top_p / jnp.sum(top_p, -1, keepdims=True) * jnp.take_along_axis(pg, gsel[:, None], axis=1)
    eidx = (gsel[:, None] * EXPERTS_PER_GROUP + top_i).astype(I32)

    A = T * K
    nb = A // blk + E
    cnt = jnp.sum((eidx[:, :, None] == jnp.arange(E, dtype=I32)).astype(I32), axis=1)
    cum = jnp.cumsum(cnt, axis=0) - cnt
    counts = jnp.sum(cnt, axis=0)
    padded = (counts + blk - 1) // blk * blk
    pend = jnp.cumsum(padded)
    pstart = pend - padded
    dest = pstart[eidx] + jnp.take_along_axis(cum, eidx, axis=1)
    tok = jnp.broadcast_to(jnp.arange(T, dtype=I32)[:, None], (T, K))
    slot_tok = jnp.zeros((nb * blk,), I32).at[dest.reshape(-1)].set(tok.reshape(-1))
    blk_start = jnp.arange(nb, dtype=I32) * blk
    blk_e = jnp.minimum(jnp.sum((pend[None, :] <= blk_start[:, None]).astype(I32), axis=1), E - 1).astype(I32)
    xb = xb16[slot_tok]

    yb = pl.pallas_call(
        _expert_kernel,
        grid_spec=pltpu.PrefetchScalarGridSpec(
            num_scalar_prefetch=1,
            grid=(nb,),
            in_specs=[pl.BlockSpec((blk, D), lambda i, be_: (i, 0)),
                      pl.BlockSpec((None, None, D, D_EXPERT), lambda i, be_: (layer, be_[i], 0, 0)),
                      pl.BlockSpec((None, None, D, D_EXPERT), lambda i, be_: (layer, be_[i], 0, 0)),
                      pl.BlockSpec((None, None, D_EXPERT, D), lambda i, be_: (layer, be_[i], 0, 0))],
            out_specs=pl.BlockSpec((blk, D), lambda i, be_: (i, 0))),
        out_shape=jax.ShapeDtypeStruct((nb * blk, D), F32),
        compiler_params=_cparams("arbitrary"),
    )(blk_e, xb, w1, w3, w2)

    y0 = yb[dest[:, 0]]
    y1 = yb[dest[:, 1]]
    tm = _row_tile(T, 344)
    return pl.pallas_call(
        _combine_ln_kernel,
        grid=(T // tm,),
        in_specs=[pl.BlockSpec((tm, D), lambda i: (i, 0)),
                  pl.BlockSpec((tm, D), lambda i: (i, 0)),
                  pl.BlockSpec((tm, D), lambda i: (i, 0)),
                  pl.BlockSpec((tm, K), lambda i: (i, 0)),
                  pl.BlockSpec((1, D), lambda i: (0, 0)),
                  pl.BlockSpec((1, D), lambda i: (0, 0))],
        out_specs=pl.BlockSpec((tm, D), lambda i: (i, 0)),
        out_shape=jax.ShapeDtypeStruct((T, D), F32),
        compiler_params=_cparams("parallel"),
    )(xt, y0, y1, gate, ln_g.reshape(1, D), ln_b.reshape(1, D))


def _rope_tables(pos):
    half = HEAD_DIM // 2
    inv = jnp.power(ROPE_THETA, -jnp.arange(half, dtype=F32) / half)
    ang = pos.astype(F32)[:, None] * inv[None, :]
    cos = jnp.cos(ang)
    sin = jnp.sin(ang)
    return jnp.concatenate([cos, cos], -1), jnp.concatenate([-sin, sin], -1)


def kernel(x_prompt, x_sample, state_gdn_s, state_gdn_conv, cache_k, cache_v, cache_kidx, page_table,
           gdn_w_in, gdn_conv_w, gdn_a_log, gdn_dt_bias, gdn_norm_w, gdn_w_out,
           dsa_w_in, dsa_w_out, ln1_g, ln1_b, ln2_g, ln2_b,
           moe_wg, moe_bg, moe_we, moe_be, moe_w1, moe_w3, moe_w2):
    B, L, D = x_prompt.shape
    BS, LS, _ = x_sample.shape
    TP = B * L
    past = page_table.shape[1] * PAGE_SIZE
    x = jnp.concatenate([x_prompt.reshape(TP, D), x_sample.reshape(BS * LS, D)], axis=0)
    pos = jnp.concatenate([jnp.tile(jnp.arange(L), B), jnp.tile(past + jnp.arange(LS), BS)])
    cos, sin = _rope_tables(pos)
    nq = DSA_QD // HEAD_DIM
    nkv = DSA_KVD // HEAD_DIM
    rope_ranges = ((0, nq + nkv), (nq + 2 * nkv, nq + 2 * nkv + IDX_HEADS + 1))
    wi_scale = IDX_HEADS ** -0.5 * IDX_DIM ** -0.5

    p_s, p_c, s_s, s_c = [], [], [], []
    p_k, p_v, p_ki, s_k, s_v, s_ki = [], [], [], [], [], []
    for i in range(DEPTH):
        j = i // N_MIXERS
        if i % N_MIXERS == 0:
            main, tail = project(x, gdn_w_in, j, GDN_MAIN, tn=512, tm_cap=688)
            gp = (gdn_conv_w[j], gdn_a_log[j], gdn_dt_bias[j], gdn_norm_w[j])
            c0 = jnp.zeros((B, CONV_W - 1, GDN_CONV_DIM), F32)
            s0 = jnp.zeros((B, GDN_HV, GDN_DK, GDN_DV), F32)
            op, sp = gdn_core(main, tail, c0, *gp, s0, B, L)
            cp = jnp.stack([main[b * L + L - (CONV_W - 1):(b + 1) * L, :GDN_CONV_DIM] for b in range(B)])
            pad_rows = ((0, 0), (0, GDN_CHUNK - LS), (0, 0))
            main_s = jnp.pad(main[TP:].reshape(BS, LS, GDN_MAIN), pad_rows).reshape(BS * GDN_CHUNK, GDN_MAIN)
            tail_s = jnp.pad(tail[TP:].reshape(BS, LS, LANES), pad_rows).reshape(BS * GDN_CHUNK, LANES)
            osp, ss = gdn_core(main_s, tail_s, state_gdn_conv[j], *gp, state_gdn_s[j], BS, GDN_CHUNK, valid=LS)
            os_ = osp.reshape(BS, GDN_CHUNK, GDN_VD)[:, :LS].reshape(BS * LS, GDN_VD)
            cs = main[TP:, :GDN_CONV_DIM].reshape(BS, LS, GDN_CONV_DIM)[:, LS - (CONV_W - 1):]
            p_s.append(sp); p_c.append(cp); s_s.append(ss); s_c.append(cs)
            w_out = gdn_w_out
        else:
            pp, wi = project(x, dsa_w_in, j, DSA_MAIN, tn=HEAD_DIM, tm_cap=1376, cos=cos, sin=sin,
                             rope_ranges=rope_ranges, tail_scale=wi_scale)
            op = dsa_prompt_attend(pp, wi, B, L)
            os_ = dsa_sample_attend(pp, wi, cache_k, cache_v, cache_kidx, page_table, j, TP, BS, LS).astype(BF16)
            kn = pp[TP:, DSA_QD:DSA_QD + DSA_KVD].reshape(BS, LS, N_KV_HEADS, HEAD_DIM)
            vn = pp[TP:, DSA_QD + DSA_KVD:DSA_QD + 2 * DSA_KVD].reshape(BS, LS, N_KV_HEADS, HEAD_DIM)
            kin = pp[TP:, DSA_MAIN - IDX_DIM:].reshape(BS, LS, IDX_DIM)
            p_k.append(pp[:TP, DSA_QD:DSA_QD + DSA_KVD].reshape(B, L, N_KV_HEADS, HEAD_DIM))
            p_v.append(pp[:TP, DSA_QD + DSA_KVD:DSA_QD + 2 * DSA_KVD].reshape(B, L, N_KV_HEADS, HEAD_DIM))
            p_ki.append(pp[:TP, DSA_MAIN - IDX_DIM:].reshape(B, L, IDX_DIM))
            s_k.append(kn); s_v.append(vn); s_ki.append(kin)
            w_out = dsa_w_out
        o_all = jnp.concatenate([op, os_], axis=0)
        x = matmul_res_ln(o_all, w_out, j, x, ln1_g[i], ln1_b[i])
        x = moe_layer(x, moe_wg[i], moe_bg[i], moe_we[i], moe_be[i], moe_w1, moe_w3, moe_w2, i,
                      ln2_g[i], ln2_b[i])
    xp = x[:TP].reshape(B, L, D)
    xs = x[TP:].reshape(BS, LS, D)
    return (xp, xs, jnp.stack(p_s), jnp.stack(p_c), jnp.stack(p_k), jnp.stack(p_v), jnp.stack(p_ki),
            jnp.stack(s_s), jnp.stack(s_c), jnp.stack(s_k), jnp.stack(s_v), jnp.stack(s_ki))
```

```python
import functools

import jax
import jax.numpy as jnp
from jax import lax
from jax.experimental import pallas as pl
from jax.experimental.pallas import tpu as pltpu

D_MODEL = 2048
DEPTH = 4
PAGE_SIZE = 128
N_MIXERS = 2
GDN_DK = 128
GDN_DV = 128
GDN_HK = D_MODEL // GDN_DK
GDN_HV = 2 * GDN_HK
GDN_KD = GDN_HK * GDN_DK
GDN_VD = GDN_HV * GDN_DV
GDN_CONV_DIM = 2 * GDN_KD + GDN_VD
GDN_MAIN = GDN_CONV_DIM + GDN_VD
CONV_W = 4
GDN_CHUNK = 64
HEAD_DIM = 128
N_HEADS = D_MODEL // HEAD_DIM
N_KV_HEADS = 4
KV_GROUP = N_HEADS // N_KV_HEADS
IDX_HEADS = 16
IDX_DIM = 128
DSA_QD = N_HEADS * HEAD_DIM
DSA_KVD = N_KV_HEADS * HEAD_DIM
DSA_MAIN = DSA_QD + 2 * DSA_KVD + IDX_HEADS * IDX_DIM
TOPK_MAX = 256
ROPE_THETA = 10000.0
N_GROUPS = 4
EXPERTS_PER_GROUP = 8
N_EXPERTS = N_GROUPS * EXPERTS_PER_GROUP
TOPK_EXPERTS = 2
D_EXPERT = D_MODEL // 4
ALPHA = (2 * DEPTH) ** 0.25
LN_EPS = 1e-5
NORM_EPS = 1e-6

LANES = 128
F32 = jnp.float32
BF16 = jnp.bfloat16
I32 = jnp.int32
VMEM_LIMIT = 56 * 1024 * 1024
INT_MIN = -2 ** 31
NEG_INF = float("-inf")


def _cparams(*sem):
    return pltpu.CompilerParams(dimension_semantics=sem, vmem_limit_bytes=VMEM_LIMIT)


def _bdot(a, b):
    return jnp.dot(a.astype(BF16), b.astype(BF16), preferred_element_type=F32)


def _bdot_nt(a, b):
    return lax.dot_general(a.astype(BF16), b.astype(BF16), (((1,), (1,)), ((), ())),
                           preferred_element_type=F32)


def _row_tile(T, cap):
    if T <= cap:
        return T
    best = None
    for t in range(16, cap + 1, 16):
        if T % t == 0:
            best = t
    assert best is not None, T
    return best


def _proj_kernel(x_ref, w_ref, wt_ref, *rest, rope_ranges, tail_rope, tail_scale):
    if rope_ranges:
        cos_ref, sin_ref, o_ref, t_ref, xs = rest
    else:
        o_ref, t_ref, xs = rest
    j = pl.program_id(1)

    @pl.when(j == 0)
    def _():
        xs[...] = x_ref[...].astype(BF16)

    def rope(a):
        return a * cos_ref[...] + pltpu.roll(a, HEAD_DIM // 2, 1) * sin_ref[...]

    xb = xs[...]
    acc = jnp.dot(xb, w_ref[...].astype(BF16), preferred_element_type=F32)
    if rope_ranges:
        per_tile = acc.shape[1] // HEAD_DIM
        parts = []
        for hh in range(per_tile):
            a = acc[:, hh * HEAD_DIM:(hh + 1) * HEAD_DIM]
            head = j * per_tile + hh
            is_rope = (head >= rope_ranges[0][0]) & (head < rope_ranges[0][1])
            for lo, hi in rope_ranges[1:]:
                is_rope = is_rope | ((head >= lo) & (head < hi))
            parts.append(jnp.where(is_rope, rope(a), a))
        acc = parts[0] if per_tile == 1 else jnp.concatenate(parts, axis=1)
    o_ref[...] = acc

    @pl.when(j == 0)
    def _():
        t = jnp.dot(xb, wt_ref[...].astype(BF16), preferred_element_type=F32)
        parts = []
        for hh in range(t.shape[1] // LANES):
            a = t[:, hh * LANES:(hh + 1) * LANES]
            parts.append(rope(a) if hh < tail_rope else a * tail_scale)
        t_ref[...] = parts[0] if len(parts) == 1 else jnp.concatenate(parts, axis=1)


def project(x, w, layer, n_main, tn, tm_cap, cos=None, sin=None, rope_ranges=(), tail_rope=0, tail_scale=1.0):
    T, D = x.shape
    tm = _row_tile(T, tm_cap)
    n_tail = w.shape[2] - n_main
    tw = -(-n_tail // LANES) * LANES
    w_tail = jnp.pad(w[layer, :, n_main:], ((0, 0), (0, tw - n_tail)))
    in_specs = [pl.BlockSpec((tm, D), lambda i, j: (i, 0)),
                pl.BlockSpec((None, D, tn), lambda i, j: (layer, 0, j)),
                pl.BlockSpec((D, tw), lambda i, j: (0, 0))]
    args = [x, w, w_tail]
    if rope_ranges:
        in_specs += [pl.BlockSpec((tm, LANES), lambda i, j: (i, 0))] * 2
        args += [cos, sin]
    return pl.pallas_call(
        functools.partial(_proj_kernel, rope_ranges=tuple(rope_ranges), tail_rope=tail_rope, tail_scale=tail_scale),
        grid=(T // tm, n_main // tn),
        in_specs=in_specs,
        out_specs=[pl.BlockSpec((tm, tn), lambda i, j: (i, j)),
                   pl.BlockSpec((tm, tw), lambda i, j: (i, 0))],
        out_shape=[jax.ShapeDtypeStruct((T, n_main), F32), jax.ShapeDtypeStruct((T, tw), F32)],
        scratch_shapes=[pltpu.VMEM((tm, D), BF16)],
        compiler_params=_cparams("parallel", "arbitrary"),
    )(*args)


def _mm_res_ln_kernel(x_ref, w_ref, r_ref, g_ref, b_ref, o_ref, *, nk):
    k = pl.program_id(1)
    part = jnp.dot(x_ref[...].astype(BF16), w_ref[...].astype(BF16), preferred_element_type=F32)

    @pl.when(k == 0)
    def _():
        o_ref[...] = part

    @pl.when(k > 0)
    def _():
        o_ref[...] += part

    @pl.when(k == nk - 1)
    def _():
        h = ALPHA * r_ref[...] + o_ref[...]
        mu = jnp.mean(h, -1, keepdims=True)
        hc = h - mu
        var = jnp.mean(hc * hc, -1, keepdims=True)
        o_ref[...] = hc * lax.rsqrt(var + LN_EPS) * g_ref[...] + b_ref[...]


def matmul_res_ln(x, w, layer, resid, g, b, tm_cap=688, tk=512):
    T, K = x.shape
    D = w.shape[2]
    tm = _row_tile(T, tm_cap)
    nk = K // tk
    return pl.pallas_call(
        functools.partial(_mm_res_ln_kernel, nk=nk),
        grid=(T // tm, nk),
        in_specs=[pl.BlockSpec((tm, tk), lambda i, k: (i, k)),
                  pl.BlockSpec((None, tk, D), lambda i, k: (layer, k, 0)),
                  pl.BlockSpec((tm, D), lambda i, k: (i, 0)),
                  pl.BlockSpec((1, D), lambda i, k: (0, 0)),
                  pl.BlockSpec((1, D), lambda i, k: (0, 0))],
        out_specs=pl.BlockSpec((tm, D), lambda i, k: (i, 0)),
        out_shape=jax.ShapeDtypeStruct((T, D), F32),
        compiler_params=_cparams("parallel", "arbitrary"),
    )(x, w, resid, g.reshape(1, D), b.reshape(1, D))


def _dsa_prompt_kernel(q_ref, qi0_ref, qi1_ref, wi_ref, k_ref, v_ref, ki_ref, o_ref,
                       kbf, vbf, kibf, key_s, bias_s, *, tq, seq, topk, s_step):
    i = pl.program_id(1)

    @pl.when(i == 0)
    def _():
        kbf[...] = k_ref[...].astype(BF16)
        vbf[...] = v_ref[...].astype(BF16)
        kibf[...] = ki_ref[...].astype(BF16)

    half = IDX_HEADS // 2
    qi_rows = [qi0_ref[:, h * IDX_DIM:(h + 1) * IDX_DIM].astype(BF16) for h in range(half)]
    qi_rows += [qi1_ref[:, h * IDX_DIM:(h + 1) * IDX_DIM].astype(BF16) for h in range(half)]
    qi_stack = jnp.concatenate(qi_rows, axis=0)
    wib = wi_ref[...].astype(BF16).astype(F32)
    q_rows = [jnp.concatenate([q_ref[:, (n * KV_GROUP + g) * HEAD_DIM:(n * KV_GROUP + g + 1) * HEAD_DIM]
                               for g in range(KV_GROUP)], axis=0).astype(BF16)
              for n in range(N_KV_HEADS)]

    def body(S):
        qpos = i * tq + lax.broadcasted_iota(I32, (tq, 1), 0)
        for c0 in range(0, S, s_step):
            s = _bdot_nt(qi_stack, kibf[c0:c0 + s_step, :])
            r = jnp.maximum(s, 0.0).astype(BF16).astype(F32)
            score = r[0:tq] * wib[:, 0:1]
            for h in range(1, IDX_HEADS):
                score = score + r[h * tq:(h + 1) * tq] * wib[:, h:h + 1]
            score = score + 0.0
            bits = pltpu.bitcast(score, I32)
            key = jnp.where(bits < 0, bits ^ jnp.int32(0x7FFFFFFF), bits)
            spos = c0 + lax.broadcasted_iota(I32, (tq, s_step), 1)
            key_s[:, c0:c0 + s_step] = jnp.where(spos <= qpos, key, jnp.int32(INT_MIN))

        def count_ge(cand):
            return jnp.sum((key_s[:, 0:S] >= cand).astype(I32), axis=1, keepdims=True)

        t0 = jnp.where(count_ge(jnp.zeros((tq, 1), I32)) >= topk, jnp.int32(0), jnp.int32(INT_MIN))
        t0 = jnp.broadcast_to(t0, (tq, 1))

        def bit_step(it, t):
            cand = t | jnp.left_shift(jnp.int32(1), 30 - it)
            return jnp.where(count_ge(cand) >= topk, cand, t)

        thr = lax.fori_loop(0, 31, bit_step, t0)

        keyv = key_s[:, 0:S]
        valid = lax.broadcasted_iota(I32, (tq, S), 1) <= qpos
        ge = keyv >= thr
        n_ge = jnp.sum((ge & valid).astype(I32), axis=1, keepdims=True)
        has_tie = jnp.max(n_ge) > topk
        bias_s[:, 0:S] = jnp.where(ge & valid, 0.0, NEG_INF)

        @pl.when(has_tie)
        def _():
            gt = keyv > thr
            n_gt = jnp.sum((gt & valid).astype(I32), axis=1, keepdims=True)
            room = (topk - n_gt).astype(F32)
            eq = ((keyv == thr) & valid)
            tri = (lax.broadcasted_iota(I32, (LANES, LANES), 0)
                   < lax.broadcasted_iota(I32, (LANES, LANES), 1)).astype(BF16)
            carry = jnp.zeros((tq, 1), F32)
            for c0 in range(0, S, LANES):
                eqc = eq[:, c0:c0 + LANES]
                before = carry + jnp.dot(eqc.astype(BF16), tri, preferred_element_type=F32)
                keep = (gt[:, c0:c0 + LANES] & valid[:, c0:c0 + LANES]) | (eqc & (before < room))
                bias_s[:, c0:c0 + LANES] = jnp.where(keep, 0.0, NEG_INF)
                carry = carry + jnp.sum(eqc.astype(F32), axis=1, keepdims=True)

        bias = bias_s[:, 0:S]
        for n in range(N_KV_HEADS):
            s = _bdot_nt(q_rows[n], kbf[0:S, n * HEAD_DIM:(n + 1) * HEAD_DIM]) * (HEAD_DIM ** -0.5)
            s = s.reshape(KV_GROUP, tq, S) + bias[None]
            m = jnp.max(s, axis=-1, keepdims=True)
            p = jnp.exp(s - m)
            l = jnp.sum(p, axis=-1, keepdims=True)
            o = jnp.dot(p.reshape(KV_GROUP * tq, S).astype(BF16), vbf[0:S, n * HEAD_DIM:(n + 1) * HEAD_DIM],
                        preferred_element_type=F32)
            o = o.reshape(KV_GROUP, tq, HEAD_DIM) / l
            for g in range(KV_GROUP):
                h = n * KV_GROUP + g
                o_ref[:, h * HEAD_DIM:(h + 1) * HEAD_DIM] = o[g].astype(o_ref.dtype)

    n_var = seq // s_step
    per = (seq // tq) // n_var
    for c in range(n_var):
        @pl.when(i // per == c)
        def _(c=c):
            body((c + 1) * s_step)


def dsa_prompt_attend(pp, tail, batch, seq, tq=128, s_step=512):
    topk = min(TOPK_MAX, seq // 4)
    s_step = min(s_step, seq)
    nq = seq // tq
    kcol = DSA_QD // DSA_KVD
    qicol = (DSA_QD + 2 * DSA_KVD) // (IDX_HEADS * IDX_DIM // 2)
    kicol = (DSA_QD + 2 * DSA_KVD + IDX_HEADS * IDX_DIM) // IDX_DIM
    assert (DSA_QD + 2 * DSA_KVD) % (IDX_HEADS * IDX_DIM // 2) == 0
    hq = IDX_HEADS * IDX_DIM // 2
    return pl.pallas_call(
        functools.partial(_dsa_prompt_kernel, tq=tq, seq=seq, topk=topk, s_step=s_step),
        grid=(batch, nq),
        in_specs=[pl.BlockSpec((tq, DSA_QD), lambda b, i: (b * nq + i, 0)),
                  pl.BlockSpec((tq, hq), lambda b, i: (b * nq + i, qicol)),
                  pl.BlockSpec((tq, hq), lambda b, i: (b * nq + i, qicol + 1)),
                  pl.BlockSpec((tq, LANES), lambda b, i: (b * nq + i, 1)),
                  pl.BlockSpec((seq, DSA_KVD), lambda b, i: (b, kcol)),
                  pl.BlockSpec((seq, DSA_KVD), lambda b, i: (b, kcol + 1)),
                  pl.BlockSpec((seq, IDX_DIM), lambda b, i: (b, 0))],
        out_specs=pl.BlockSpec((tq, DSA_QD), lambda b, i: (b * nq + i, 0)),
        out_shape=jax.ShapeDtypeStruct((batch * seq, DSA_QD), BF16),
        scratch_shapes=[pltpu.VMEM((seq, DSA_KVD), BF16), pltpu.VMEM((seq, DSA_KVD), BF16),
                        pltpu.VMEM((seq, IDX_DIM), BF16),
                        pltpu.VMEM((tq, seq), I32), pltpu.VMEM((tq, seq), F32)],
        compiler_params=_cparams("parallel", "arbitrary"),
    )(pp, pp, pp, tail, pp, pp, tail)


def _order_key(score):
    bits = pltpu.bitcast(score + 0.0, I32)
    return jnp.where(bits < 0, bits ^ jnp.int32(0x7FFFFFFF), bits)


def _dsa_sample_kernel(pt_ref, q_ref, qi0_ref, qi1_ref, wi_ref, kn_ref, vn_ref, kin_ref, ck_hbm, cv_hbm, cki_hbm,
                       o_ref, kibuf, kvbuf, key_s, bias_s, sc_s, p_s, sem_ki, sem_kv,
                       *, layer, ls, n_pages, ppc, topk):
    b = pl.program_id(0)
    past = n_pages * PAGE_SIZE
    S = past + LANES
    ck = ppc * PAGE_SIZE
    prow = PAGE_SIZE * N_KV_HEADS
    n_chunks = n_pages // ppc
    gk = 4 * PAGE_SIZE
    rows = KV_GROUP * ls

    def ki_copy(p):
        return pltpu.make_async_copy(cki_hbm.at[layer, pt_ref[b, p]], kibuf.at[p], sem_ki)

    def kv_copy(src, c, slot, i):
        return pltpu.make_async_copy(src.at[layer, pt_ref[b, c * ppc + i]],
                                     kvbuf.at[slot, pl.ds(i * prow, prow)], sem_kv.at[slot])

    def head_rows(slot, n):
        return kvbuf[slot, pl.ds(n, ck, stride=N_KV_HEADS), :].astype(BF16)

    def start_chunk(src, c, slot):
        for i in range(ppc):
            kv_copy(src, c, slot, i).start()

    def wait_chunk(src, c, slot):
        for i in range(ppc):
            kv_copy(src, c, slot, i).wait()

    def ki_start(p, carry):
        ki_copy(p).start()
        return carry

    def ki_wait(p, carry):
        ki_copy(p).wait()
        return carry

    lax.fori_loop(0, n_pages, ki_start, 0)
    start_chunk(ck_hbm, 0, 0)

    half = IDX_HEADS // 2
    qi_stack = jnp.concatenate([qi0_ref[:, h * IDX_DIM:(h + 1) * IDX_DIM] for h in range(half)]
                               + [qi1_ref[:, h * IDX_DIM:(h + 1) * IDX_DIM] for h in range(half)],
                               axis=0).astype(BF16)
    wib = wi_ref[...].astype(BF16).astype(F32)
    q_rows = [jnp.concatenate([q_ref[:, (n * KV_GROUP + g) * HEAD_DIM:(n * KV_GROUP + g + 1) * HEAD_DIM]
                               for g in range(KV_GROUP)], axis=0).astype(BF16)
              for n in range(N_KV_HEADS)]
    qpos = past + lax.broadcasted_iota(I32, (ls, 1), 0)
    zpad = jnp.zeros((LANES - ls, DSA_KVD), F32)

    def index_keys(ki_rows):
        s = _bdot_nt(qi_stack, ki_rows)
        r = jnp.maximum(s, 0.0).astype(BF16).astype(F32)
        score = r[0:ls] * wib[:, 0:1]
        for h in range(1, IDX_HEADS):
            score = score + r[h * ls:(h + 1) * ls] * wib[:, h:h + 1]
        return _order_key(score)

    lax.fori_loop(0, n_pages, ki_wait, 0)

    def index_step(g, carry):
        kic = kibuf[pl.ds(g * (gk // PAGE_SIZE), gk // PAGE_SIZE)].reshape(gk, IDX_DIM)
        key_s[:, pl.ds(pl.multiple_of(g * gk, gk), gk)] = index_keys(kic)
        return carry

    lax.fori_loop(0, past // gk, index_step, 0)
    kin_pad = jnp.concatenate([kin_ref[...], zpad[:, 0:IDX_DIM]], axis=0)
    new_pos = past + lax.broadcasted_iota(I32, (ls, LANES), 1)
    key_s[:, past:S] = jnp.where(new_pos <= qpos, index_keys(kin_pad), jnp.int32(INT_MIN))

    def count_ge(cand):
        return jnp.sum((key_s[...] >= cand).astype(I32), axis=1, keepdims=True)

    t0 = jnp.where(count_ge(jnp.zeros((ls, 1), I32)) >= topk, jnp.int32(0), jnp.int32(INT_MIN))

    def bit_step(it, t):
        cand = t | jnp.left_shift(jnp.int32(1), 30 - it)
        return jnp.where(count_ge(cand) >= topk, cand, t)

    thr = lax.fori_loop(0, 31, bit_step, t0)
    keyv = key_s[...]
    valid = lax.broadcasted_iota(I32, (ls, S), 1) <= qpos
    ge = (keyv >= thr) & valid
    n_ge = jnp.sum(ge.astype(I32), axis=1, keepdims=True)
    bias_s[...] = jnp.where(ge, 0.0, NEG_INF)

    @pl.when(jnp.max(n_ge) > topk)
    def _():
        n_gt = jnp.sum(((keyv > thr) & valid).astype(I32), axis=1, keepdims=True)
        room = (topk - n_gt).astype(F32)
        tri = (lax.broadcasted_iota(I32, (LANES, LANES), 0)
               < lax.broadcasted_iota(I32, (LANES, LANES), 1)).astype(BF16)

        def tie_step(c, carry):
            off = pl.multiple_of(c * LANES, LANES)
            kc = key_s[:, pl.ds(off, LANES)]
            ok = (off + lax.broadcasted_iota(I32, (ls, LANES), 1)) <= qpos
            eq = (kc == thr) & ok
            before = carry + jnp.dot(eq.astype(BF16), tri, preferred_element_type=F32)
            keep = ((kc > thr) & ok) | (eq & (before < room))
            bias_s[:, pl.ds(off, LANES)] = jnp.where(keep, 0.0, NEG_INF)
            return carry + jnp.sum(eq.astype(F32), axis=1, keepdims=True)

        lax.fori_loop(0, S // LANES, tie_step, jnp.zeros((ls, 1), F32))

    def masked_scores(n, k_rows, bias):
        s = _bdot_nt(q_rows[n], k_rows) * (HEAD_DIM ** -0.5)
        return (s.reshape(KV_GROUP, ls, s.shape[1]) + bias[None]).reshape(rows, s.shape[1])

    def k_step(c, carry):
        slot = c % 2
        wait_chunk(ck_hbm, c, slot)

        @pl.when(c + 1 < n_chunks)
        def _():
            start_chunk(ck_hbm, c + 1, 1 - slot)

        @pl.when(c + 1 == n_chunks)
        def _():
            start_chunk(cv_hbm, 0, 1 - slot)

        off = pl.multiple_of(c * ck, ck)
        bias = bias_s[:, pl.ds(off, ck)]
        for n in range(N_KV_HEADS):
            sc_s[n * rows:(n + 1) * rows, pl.ds(off, ck)] = masked_scores(n, head_rows(slot, n), bias)
        return carry

    lax.fori_loop(0, n_chunks, k_step, 0)
    kn_pad = jnp.concatenate([kn_ref[...], zpad], axis=0).astype(BF16)
    for n in range(N_KV_HEADS):
        sc_s[n * rows:(n + 1) * rows, past:S] = masked_scores(
            n, kn_pad[:, n * HEAD_DIM:(n + 1) * HEAD_DIM], bias_s[:, past:S])

    sc = sc_s[...]
    m = jnp.max(sc, axis=1, keepdims=True)
    p = jnp.exp(sc - m)
    l = jnp.sum(p, axis=1, keepdims=True)
    p_s[...] = p.astype(BF16)

    def v_step(v, acc):
        slot = (n_chunks + v) % 2
        wait_chunk(cv_hbm, v, slot)

        @pl.when(v + 1 < n_chunks)
        def _():
            start_chunk(cv_hbm, v + 1, 1 - slot)

        off = pl.multiple_of(v * ck, ck)
        return tuple(acc[n] + jnp.dot(p_s[n * rows:(n + 1) * rows, pl.ds(off, ck)], head_rows(slot, n),
                                      preferred_element_type=F32)
                     for n in range(N_KV_HEADS))

    acc = lax.fori_loop(0, n_chunks, v_step, tuple(jnp.zeros((rows, HEAD_DIM), F32) for _ in range(N_KV_HEADS)))
    vn_pad = jnp.concatenate([vn_ref[...], zpad], axis=0).astype(BF16)
    for n in range(N_KV_HEADS):
        o = acc[n] + jnp.dot(p_s[n * rows:(n + 1) * rows, past:S], vn_pad[:, n * HEAD_DIM:(n + 1) * HEAD_DIM],
                             preferred_element_type=F32)
        o = o / l[n * rows:(n + 1) * rows]
        for g in range(KV_GROUP):
            h = n * KV_GROUP + g
            o_ref[:, h * HEAD_DIM:(h + 1) * HEAD_DIM] = o[g * ls:(g + 1) * ls]


def dsa_sample_attend(pp, tail, cache_k, cache_v, cache_kidx, page_table, layer, row0, batch, ls, ppc=16):
    n_pool = cache_k.shape[1]
    n_pages = page_table.shape[1]
    past = n_pages * PAGE_SIZE
    topk = min(TOPK_MAX, (past + ls) // 4)
    assert row0 % ls == 0 and n_pages % ppc == 0 and n_pages % 4 == 0 and ls % 8 == 0
    rb = row0 // ls
    ck = cache_k.reshape(cache_k.shape[0], n_pool, PAGE_SIZE * N_KV_HEADS, HEAD_DIM)
    cv = cache_v.reshape(cache_v.shape[0], n_pool, PAGE_SIZE * N_KV_HEADS, HEAD_DIM)
    kcol = DSA_QD // DSA_KVD
    hq = IDX_HEADS * IDX_DIM // 2
    qicol = (DSA_QD + 2 * DSA_KVD) // hq
    kicol = (DSA_QD + 2 * DSA_KVD + IDX_HEADS * IDX_DIM) // IDX_DIM
    S = past + LANES
    return pl.pallas_call(
        functools.partial(_dsa_sample_kernel, layer=layer, ls=ls, n_pages=n_pages, ppc=ppc, topk=topk),
        grid_spec=pltpu.PrefetchScalarGridSpec(
            num_scalar_prefetch=1,
            grid=(batch,),
            in_specs=[pl.BlockSpec((ls, DSA_QD), lambda b, pt: (rb + b, 0)),
                      pl.BlockSpec((ls, hq), lambda b, pt: (rb + b, qicol)),
                      pl.BlockSpec((ls, hq), lambda b, pt: (rb + b, qicol + 1)),
                      pl.BlockSpec((ls, LANES), lambda b, pt: (rb + b, 1)),
                      pl.BlockSpec((ls, DSA_KVD), lambda b, pt: (rb + b, kcol)),
                      pl.BlockSpec((ls, DSA_KVD), lambda b, pt: (rb + b, kcol + 1)),
                      pl.BlockSpec((ls, IDX_DIM), lambda b, pt: (rb + b, 0)),
                      pl.BlockSpec(memory_space=pl.ANY),
                      pl.BlockSpec(memory_space=pl.ANY),
                      pl.BlockSpec(memory_space=pl.ANY)],
            out_specs=pl.BlockSpec((ls, DSA_QD), lambda b, pt: (b, 0)),
            scratch_shapes=[pltpu.VMEM((n_pages, PAGE_SIZE, IDX_DIM), F32),
                            pltpu.VMEM((2, ppc * PAGE_SIZE * N_KV_HEADS, HEAD_DIM), F32),
                            pltpu.VMEM((ls, S), I32), pltpu.VMEM((ls, S), F32),
                            pltpu.VMEM((N_HEADS * ls, S), F32), pltpu.VMEM((N_HEADS * ls, S), BF16),
                            pltpu.SemaphoreType.DMA(()), pltpu.SemaphoreType.DMA((2,))]),
        out_shape=jax.ShapeDtypeStruct((batch * ls, DSA_QD), F32),
        compiler_params=_cparams("arbitrary"),
    )(page_table, pp, pp, pp, tail, pp, pp, tail, ck, cv, cache_kidx)


def _split3(a):
    hi = a.astype(BF16)
    lo = (a - hi.astype(F32)).astype(BF16)
    return hi, lo


def _dot3(a_parts, b_parts):
    ah, al = a_parts
    bh, bl = b_parts
    d = functools.partial(jnp.dot, preferred_element_type=F32)
    return d(ah, bh) + (d(ah, bl) + d(al, bh))


def _tri_inverse(mats, order):
    n = mats[0].shape[0]
    eye = (lax.broadcasted_iota(I32, (n, n), 0) == lax.broadcasted_iota(I32, (n, n), 1)).astype(F32)
    ps = [eye - a for a in mats]
    xss = [_split3(-a) for a in mats]
    steps = max(0, (order - 1).bit_length() - 1)
    for _ in range(steps):
        xss = [_split3(_dot3(xs, xs)) for xs in xss]
        ps = [p + _dot3(_split3(p), xs) for p, xs in zip(ps, xss)]
    return ps


def _silu(x):
    return x * jax.nn.sigmoid(x)


def _gdn_kernel(xq_ref, xk_ref, xv_ref, z_ref, tail_ref, cq_ref, ck_ref, cv_ref, wq_ref, wk_ref, wv_ref,
                alog_ref, dtb_ref, nw_ref, s0_ref, o_ref, s_ref,
                beta_s, g_s, u_s, w_s, qk_s, qg_s, kd_s, el_s, *, seq, chunk, valid, hpb, unroll):
    hb = pl.program_id(1)
    C = chunk
    N = seq // C
    HALO = 8
    NV = 2 * hpb
    tail = tail_ref[...]
    beta = jax.nn.sigmoid(tail)
    x = tail + dtb_ref[...]
    softplus = jnp.maximum(x, 0.0) + jnp.log1p(jnp.exp(-jnp.abs(x)))
    g = -jnp.exp(alog_ref[...]) * softplus
    if valid < seq:
        is_real = lax.broadcasted_iota(I32, (seq, LANES), 0) < valid
        beta = jnp.where(is_real, beta, 0.0)
        g = jnp.where(is_real, g, 0.0)
    beta_s[...] = beta
    g_s[...] = g

    R = NV * C
    row = lax.broadcasted_iota(I32, (R, R), 0)
    col = lax.broadcasted_iota(I32, (R, R), 1)
    log2c = C.bit_length() - 1
    same_head = lax.shift_right_logical(row, log2c) == lax.shift_right_logical(col, log2c)
    incl = same_head & (row >= col)
    strict = same_head & (row > col)
    lane = lax.broadcasted_iota(I32, (C, LANES), 1)
    sub_t = lax.broadcasted_iota(I32, (LANES, C), 0)
    rowc = lax.broadcasted_iota(I32, (C, LANES), 0)

    def conv(xref, cref, wref, r0, c):
        prev = xref[pl.ds(pl.multiple_of(jnp.maximum(r0 - HALO, 0), HALO), HALO), :]
        win = jnp.concatenate([jnp.where(c == 0, cref[0], prev), xref[pl.ds(r0, C), :]], axis=0)
        acc = win[HALO - 3:HALO - 3 + C] * wref[0:1, :]
        for j in range(1, CONV_W):
            acc = acc + win[HALO - 3 + j:HALO - 3 + j + C] * wref[j:j + 1, :]
        return _silu(acc)

    def l2n(t):
        return t * lax.rsqrt(jnp.sum(t * t, -1, keepdims=True) + NORM_EPS)

    def prep_inputs(c):
        r0 = pl.multiple_of(c * C, C)
        qc = conv(xq_ref, cq_ref, wq_ref, r0, c)
        kc = conv(xk_ref, ck_ref, wk_ref, r0, c)
        vv = conv(xv_ref, cv_ref, wv_ref, r0, c)
        beta = beta_s[pl.ds(r0, C), :]
        gc = g_s[pl.ds(r0, C), :]
        sh = 1
        while sh < C:
            gc = gc + jnp.where(rowc >= sh, pltpu.roll(gc, sh, 0), 0.0)
            sh *= 2
        gc_t = gc.T
        qn = [l2n(qc[:, hl * GDN_DK:(hl + 1) * GDN_DK]) * (GDN_DK ** -0.5) for hl in range(hpb)]
        kn = [l2n(kc[:, hl * GDN_DK:(hl + 1) * GDN_DK]) for hl in range(hpb)]
        bcols, gcols, grows = [], [], []
        for e in range(NV):
            hv = NV * hb + e
            bcols.append(jnp.sum(jnp.where(lane == hv, beta, 0.0), axis=1, keepdims=True))
            gcols.append(jnp.sum(jnp.where(lane == GDN_HV + hv, gc, 0.0), axis=1, keepdims=True))
            grows.append(jnp.sum(jnp.where(sub_t == GDN_HV + hv, gc_t, 0.0), axis=0, keepdims=True))
        bcol = jnp.concatenate(bcols, axis=0)
        gcol = jnp.concatenate(gcols, axis=0)
        grow = jnp.concatenate(grows, axis=1)
        kn_st = jnp.concatenate([kn[e // 2] for e in range(NV)], axis=0)
        qn_st = jnp.concatenate([qn[e // 2] for e in range(NV)], axis=0)
        v_st = jnp.concatenate([vv[:, e * GDN_DV:(e + 1) * GDN_DV] for e in range(NV)], axis=0)
        decay = jnp.exp(jnp.where(incl, gcol - grow, NEG_INF))
        kb = kn_st * bcol
        a = jnp.where(strict, _bdot_nt(kb, kn_st) * decay, 0.0)
        qk_s[c] = jnp.where(incl, _bdot_nt(qn_st, kn_st) * decay, 0.0).astype(BF16)
        qg_s[c] = (qn_st * jnp.exp(gcol)).astype(BF16)
        for e in range(NV):
            glast = grows[e][:, C - 1:C]
            kd = kn[e // 2] * jnp.exp(glast - gcols[e])
            kd_s[e, c] = kd.T.astype(BF16)
            el_s[e, c] = jnp.broadcast_to(jnp.exp(glast), (8, LANES))
        return a, v_st * bcol, kb * jnp.exp(gcol)

    def prep(it, carry):
        cs = [it * unroll + u for u in range(unroll)]
        ins = [prep_inputs(c) for c in cs]
        tmats = _tri_inverse([a for a, _, _ in ins], C)
        for c, tmat, (_, vb, kbg) in zip(cs, tmats, ins):
            u_s[c] = _bdot(tmat, vb)
            w_s[c] = _bdot(tmat, kbg).astype(BF16)
        return carry

    lax.fori_loop(0, N // unroll, prep, 0)

    nw = nw_ref[...]
    s_ref[...] = s0_ref[...]

    def scan(c, carry):
        r0 = pl.multiple_of(c * C, C)
        d = functools.partial(jnp.dot, preferred_element_type=F32)
        u = u_s[c]
        w = w_s[c]
        qg = qg_s[c]
        sts = [s_ref[0, e] for e in range(NV)]
        sbs = [st.astype(BF16) for st in sts]
        vb = jnp.concatenate([u[e * C:(e + 1) * C] - d(w[e * C:(e + 1) * C], sbs[e]) for e in range(NV)],
                             axis=0).astype(BF16)
        o_intra = d(qk_s[c], vb)
        for e in range(NV):
            o = d(qg[e * C:(e + 1) * C], sbs[e]) + o_intra[e * C:(e + 1) * C]
            s_ref[0, e] = sts[e] * el_s[e, c][0:1, :] + d(kd_s[e, c], vb[e * C:(e + 1) * C])
            zf = z_ref[pl.ds(r0, C), e * GDN_DV:(e + 1) * GDN_DV]
            og = o * lax.rsqrt(jnp.mean(o * o, -1, keepdims=True) + NORM_EPS) * nw * _silu(zf)
            o_ref[pl.ds(r0, C), e * GDN_DV:(e + 1) * GDN_DV] = og.astype(o_ref.dtype)
        return carry

    lax.fori_loop(0, N, scan, 0)


def gdn_core(main, tail, conv0, conv_w, a_log, dt_bias, norm_w, s0, batch, seq, valid=None, hpb=2, unroll=4):
    valid = seq if valid is None else valid
    C = min(GDN_CHUNK, seq)
    N = seq // C
    assert seq % C == 0
    unroll = unroll if N % unroll == 0 else 1
    conv0p = jnp.pad(conv0, ((0, 0), (8 - (CONV_W - 1), 0), (0, 0)))
    alog = jnp.pad(a_log, (GDN_HV, LANES - 2 * GDN_HV)).reshape(1, LANES)
    dtb = jnp.pad(dt_bias, (GDN_HV, LANES - 2 * GDN_HV)).reshape(1, LANES)
    qw = GDN_DK * hpb
    vw = 2 * GDN_DV * hpb
    kblk = GDN_KD // qw
    vblk = 2 * GDN_KD // vw
    zblk = GDN_CONV_DIM // vw
    nv = 2 * hpb
    return pl.pallas_call(
        functools.partial(_gdn_kernel, seq=seq, chunk=C, valid=valid, hpb=hpb, unroll=unroll),
        grid=(batch, GDN_HK // hpb),
        in_specs=[pl.BlockSpec((seq, qw), lambda b, h: (b, h)),
                  pl.BlockSpec((seq, qw), lambda b, h: (b, kblk + h)),
                  pl.BlockSpec((seq, vw), lambda b, h: (b, vblk + h)),
                  pl.BlockSpec((seq, vw), lambda b, h: (b, zblk + h)),
                  pl.BlockSpec((seq, LANES), lambda b, h: (b, 0)),
                  pl.BlockSpec((1, 8, qw), lambda b, h: (b, 0, h)),
                  pl.BlockSpec((1, 8, qw), lambda b, h: (b, 0, kblk + h)),
                  pl.BlockSpec((1, 8, vw), lambda b, h: (b, 0, vblk + h)),
                  pl.BlockSpec((CONV_W, qw), lambda b, h: (0, h)),
                  pl.BlockSpec((CONV_W, qw), lambda b, h: (0, kblk + h)),
                  pl.BlockSpec((CONV_W, vw), lambda b, h: (0, vblk + h)),
                  pl.BlockSpec((1, LANES), lambda b, h: (0, 0)),
                  pl.BlockSpec((1, LANES), lambda b, h: (0, 0)),
                  pl.BlockSpec((1, GDN_DV), lambda b, h: (0, 0)),
                  pl.BlockSpec((1, nv, GDN_DK, GDN_DV), lambda b, h: (b, h, 0, 0))],
        out_specs=[pl.BlockSpec((seq, vw), lambda b, h: (b, h)),
                   pl.BlockSpec((1, nv, GDN_DK, GDN_DV), lambda b, h: (b, h, 0, 0))],
        out_shape=[jax.ShapeDtypeStruct((batch * seq, GDN_VD), BF16),
                   jax.ShapeDtypeStruct((batch, GDN_HV, GDN_DK, GDN_DV), F32)],
        scratch_shapes=[pltpu.VMEM((seq, LANES), F32), pltpu.VMEM((seq, LANES), F32),
                        pltpu.VMEM((N, nv * C, GDN_DV), F32), pltpu.VMEM((N, nv * C, GDN_DK), BF16),
                        pltpu.VMEM((N, nv * C, nv * C), BF16), pltpu.VMEM((N, nv * C, GDN_DK), BF16),
                        pltpu.VMEM((nv, N, GDN_DK, C), BF16), pltpu.VMEM((nv, N, 8, LANES), F32)],
        compiler_params=_cparams("parallel", "parallel"),
    )(main, main, main, main, tail, conv0p, conv0p, conv0p, conv_w, conv_w, conv_w,
      alog, dtb, norm_w.reshape(1, GDN_DV), s0)


def l2norm(x):
    return x * lax.rsqrt(jnp.sum(x * x, -1, keepdims=True) + NORM_EPS)


def gdn_chunked(q, k, v, g, beta, s0):
    B, L, H, _ = q.shape
    DV = v.shape[-1]
    C = min(GDN_CHUNK, L)
    N = L // C

    def chunks(t):
        t = t.reshape((B, N, C, H) + t.shape[3:])
        return jnp.moveaxis(t, (1, 3), (0, 2))

    q, k, v, g, beta = chunks(q), chunks(k), chunks(v), chunks(g), chunks(beta)
    gc = jnp.cumsum(g, axis=-1)
    incl = jnp.tril(jnp.ones((C, C), bool))
    strict = jnp.tril(jnp.ones((C, C), bool), -1)
    decay = jnp.exp(jnp.where(incl, gc[..., :, None] - gc[..., None, :], -jnp.inf))
    kb = k * beta[..., None]
    vb = v * beta[..., None]
    lmat = jnp.where(strict, jnp.einsum('nbhcd,nbhed->nbhce', kb, k) * decay, 0.0)
    eye = jnp.eye(C, dtype=lmat.dtype)
    tmat = lax.linalg.triangular_solve(lmat + eye, jnp.broadcast_to(eye, lmat.shape),
                                       left_side=True, lower=True, unit_diagonal=True)
    u = jnp.einsum('nbhce,nbhed->nbhcd', tmat, vb)
    w = jnp.einsum('nbhce,nbhed->nbhcd', tmat, kb * jnp.exp(gc)[..., None])
    qk = jnp.einsum('nbhcd,nbhed->nbhce', q, k) * decay

    def step(s, xs):
        q_i, k_i, u_i, w_i, gc_i, qk_i = xs
        v_new = u_i - jnp.einsum('bhcd,bhde->bhce', w_i, s)
        o = (jnp.einsum('bhcd,bhde->bhce', q_i * jnp.exp(gc_i)[..., None], s)
             + jnp.einsum('bhcs,bhse->bhce', qk_i, v_new))
        g_last = gc_i[..., -1:]
        s = (s * jnp.exp(g_last)[..., None]
             + jnp.einsum('bhcd,bhce->bhde', k_i * jnp.exp(g_last - gc_i)[..., None], v_new))
        return s, o

    s, o = lax.scan(step, s0, (q, k, u, w, gc, qk))
    o = jnp.moveaxis(o, (0, 2), (1, 3)).reshape(B, N * C, H, DV)
    return o, s


def gdn_sample(main, tail, s0, conv0, conv_w, a_log, dt_bias, norm_w):
    B, L = conv0.shape[0], main.shape[0] // conv0.shape[0]
    main = main.reshape(B, L, GDN_MAIN)
    tail = tail.reshape(B, L, LANES)
    qkv = main[..., :GDN_CONV_DIM]
    z = main[..., GDN_CONV_DIM:]
    b = tail[..., :GDN_HV]
    a = tail[..., GDN_HV:2 * GDN_HV]
    xc = jnp.concatenate([conv0, qkv], axis=1)
    conv = xc[:, 0:L] * conv_w[0]
    for j in range(1, CONV_W):
        conv = conv + xc[:, j:j + L] * conv_w[j]
    new_conv = xc[:, L:]
    conv = jax.nn.silu(conv)
    q = conv[..., :GDN_KD].reshape(B, L, GDN_HK, GDN_DK)
    k = conv[..., GDN_KD:2 * GDN_KD].reshape(B, L, GDN_HK, GDN_DK)
    v = conv[..., 2 * GDN_KD:].reshape(B, L, GDN_HV, GDN_DV)
    rep = GDN_HV // GDN_HK
    q = jnp.repeat(l2norm(q) * (GDN_DK ** -0.5), rep, axis=2)
    k = jnp.repeat(l2norm(k), rep, axis=2)
    beta = jax.nn.sigmoid(b)
    g = -jnp.exp(a_log) * jax.nn.softplus(a + dt_bias)
    o, s = gdn_chunked(q, k, v, g, beta, s0)
    zf = z.reshape(B, L, GDN_HV, GDN_DV)
    o = o * lax.rsqrt(jnp.mean(o * o, -1, keepdims=True) + NORM_EPS) * norm_w * jax.nn.silu(zf)
    return o.reshape(B * L, GDN_VD).astype(BF16), s, new_conv


def index_topk(qi, wi, ki, qpos, topk):
    s = jnp.einsum('bqhd,bsd->bqhs', qi, ki)
    score = jnp.einsum('bqhs,bqh->bqs', jax.nn.relu(s), wi)
    valid = jnp.arange(ki.shape[1])[None, :] <= qpos[:, None]
    score = jnp.where(valid[None], score, -jnp.inf)
    _, idx = lax.top_k(score, topk)
    return idx


def sparse_attend(q, k_sel, v_sel, sel_valid):
    B, Q = q.shape[:2]
    qg = q.reshape(B, Q, N_KV_HEADS, KV_GROUP, HEAD_DIM)
    s = jnp.einsum('bqngd,bqknd->bqngk', qg, k_sel) * (HEAD_DIM ** -0.5)
    s = jnp.where(sel_valid[:, :, None, None, :], s, -jnp.inf)
    p = jax.nn.softmax(s, axis=-1)
    o = jnp.einsum('bqngk,bqknd->bqngd', p, v_sel)
    return o.reshape(B, Q, DSA_QD)


def gather_rows(t, idx):
    return jax.vmap(lambda tb, ib: tb[ib])(t, idx)


def dsa_sample(pp, wi, ck, cv, cki, page_table):
    B = page_table.shape[0]
    L = pp.shape[0] // B
    past = page_table.shape[1] * PAGE_SIZE
    pos = past + jnp.arange(L)
    o1, o2, o3, o4 = DSA_QD, DSA_QD + DSA_KVD, DSA_QD + 2 * DSA_KVD, DSA_QD + 2 * DSA_KVD + IDX_HEADS * IDX_DIM
    pp = pp.reshape(B, L, DSA_MAIN)
    q = pp[..., :o1].reshape(B, L, N_HEADS, HEAD_DIM)
    k = pp[..., o1:o2].reshape(B, L, N_KV_HEADS, HEAD_DIM)
    v = pp[..., o2:o3].reshape(B, L, N_KV_HEADS, HEAD_DIM)
    qi = pp[..., o3:o4].reshape(B, L, IDX_HEADS, IDX_DIM)
    ki = pp[..., o4:]
    wi = wi.reshape(B, L, LANES)[..., :IDX_HEADS]
    ki_past = cki[page_table].reshape(B, past, IDX_DIM)
    ki_all = jnp.concatenate([ki_past, ki], axis=1)
    topk = min(TOPK_MAX, (past + L) // 4)
    idx = index_topk(qi, wi, ki_all, pos, topk)
    from_past = (idx < past)[..., None, None]
    pidx = jnp.minimum(idx, past - 1)
    phys_page = jnp.take_along_axis(page_table, (pidx // PAGE_SIZE).reshape(B, -1), axis=1).reshape(pidx.shape)
    phys = phys_page * PAGE_SIZE + pidx % PAGE_SIZE
    nidx = jnp.clip(idx - past, 0, L - 1)
    ck_flat = ck.reshape(-1, N_KV_HEADS, HEAD_DIM)
    cv_flat = cv.reshape(-1, N_KV_HEADS, HEAD_DIM)
    k_sel = jnp.where(from_past, ck_flat[phys], gather_rows(k, nidx))
    v_sel = jnp.where(from_past, cv_flat[phys], gather_rows(v, nidx))
    o = sparse_attend(q, k_sel, v_sel, idx <= pos[None, :, None])
    return o.reshape(B * L, DSA_QD).astype(BF16), k, v, ki


def _expert_kernel(be_ref, x_ref, w1_ref, w3_ref, w2_ref, o_ref):
    del be_ref
    x = x_ref[...].astype(BF16)
    h1 = jnp.dot(x, w1_ref[...].astype(BF16), preferred_element_type=F32)
    h3 = jnp.dot(x, w3_ref[...].astype(BF16), preferred_element_type=F32)
    h = _silu(h1) * h3
    o_ref[...] = jnp.dot(h.astype(BF16), w2_ref[...].astype(BF16), preferred_element_type=F32)


def _combine_ln_kernel(x_ref, y0_ref, y1_ref, gate_ref, g_ref, b_ref, o_ref):
    gate = gate_ref[...]
    h = ALPHA * x_ref[...] + (y0_ref[...] * gate[:, 0:1] + y1_ref[...] * gate[:, 1:2])
    mu = jnp.mean(h, -1, keepdims=True)
    hc = h - mu
    var = jnp.mean(hc * hc, -1, keepdims=True)
    o_ref[...] = hc * lax.rsqrt(var + LN_EPS) * g_ref[...] + b_ref[...]


def moe_layer(xt, wg, bg, we, be, w1, w3, w2, layer, ln_g, ln_b, blk=128):
    T, D = xt.shape
    E = N_EXPERTS
    K = TOPK_EXPERTS
    xb16 = xt.astype(BF16)
    lg = jnp.dot(xb16, wg.astype(BF16), preferred_element_type=F32) + bg
    pg = jax.nn.softmax(lg, axis=-1)
    gsel = jnp.argmax(lg, axis=-1)
    le = (jnp.dot(xb16, we.astype(BF16), preferred_element_type=F32) + be).reshape(T, N_GROUPS, EXPERTS_PER_GROUP)
    le_g = jnp.take_along_axis(le, gsel[:, None, None], axis=1)[:, 0]
    pe = jax.nn.softmax(le_g, axis=-1)
    top_p, top_i = lax.top_k(pe, K)
    gate = top_p / jnp.sum(top_p, -1, keepdims=True) * jnp.take_along_axis(pg, gsel[:, None], axis=1)
    eidx = (gsel[:, None] * EXPERTS_PER_GROUP + top_i).astype(I32)

    A = T * K
    nb = A // blk + E
    cnt = jnp.sum((eidx[:, :, None] == jnp.arange(E, dtype=I32)).astype(I32), axis=1)
    cum = jnp.cumsum(cnt, axis=0) - cnt
    counts = jnp.sum(cnt, axis=0)
    padded = (counts + blk - 1) // blk * blk
    pend = jnp.cumsum(padded)
    pstart = pend - padded
    dest = pstart[eidx] + jnp.take_along_axis(cum, eidx, axis=1)
    tok = jnp.broadcast_to(jnp.arange(T, dtype=I32)[:, None], (T, K))
    slot_tok = jnp.zeros((nb * blk,), I32).at[dest.reshape(-1)].set(tok.reshape(-1))
    blk_start = jnp.arange(nb, dtype=I32) * blk
    blk_e = jnp.minimum(jnp.sum((pend[None, :] <= blk_start[:, None]).astype(I32), axis=1), E - 1).astype(I32)
    xb = xb16[slot_tok]

    yb = pl.pallas_call(
        _expert_kernel,
        grid_spec=pltpu.PrefetchScalarGridSpec(
            num_scalar_prefetch=1,
            grid=(nb,),
            in_specs=[pl.BlockSpec((blk, D), lambda i, be_: (i, 0)),
                      pl.BlockSpec((None, None, D, D_EXPERT), lambda i, be_: (layer, be_[i], 0, 0)),
                      pl.BlockSpec((None, None, D, D_EXPERT), lambda i, be_: (layer, be_[i], 0, 0)),
                      pl.BlockSpec((None, None, D_EXPERT, D), lambda i, be_: (layer, be_[i], 0, 0))],
            out_specs=pl.BlockSpec((blk, D), lambda i, be_: (i, 0))),
        out_shape=jax.ShapeDtypeStruct((nb * blk, D), F32),
        compiler_params=_cparams("arbitrary"),
    )(blk_e, xb, w1, w3, w2)

    y0 = yb[dest[:, 0]]
    y1 = yb[dest[:, 1]]
    tm = _row_tile(T, 344)
    return pl.pallas_call(
        _combine_ln_kernel,
        grid=(T // tm,),
        in_specs=[pl.BlockSpec((tm, D), lambda i: (i, 0)),
                  pl.BlockSpec((tm, D), lambda i: (i, 0)),
                  pl.BlockSpec((tm, D), lambda i: (i, 0)),
                  pl.BlockSpec((tm, K), lambda i: (i, 0)),
                  pl.BlockSpec((1, D), lambda i: (0, 0)),
                  pl.BlockSpec((1, D), lambda i: (0, 0))],
        out_specs=pl.BlockSpec((tm, D), lambda i: (i, 0)),
        out_shape=jax.ShapeDtypeStruct((T, D), F32),
        compiler_params=_cparams("parallel"),
    )(xt, y0, y1, gate, ln_g.reshape(1, D), ln_b.reshape(1, D))


def _rope_tables(pos):
    half = HEAD_DIM // 2
    inv = jnp.power(ROPE_THETA, -jnp.arange(half, dtype=F32) / half)
    ang = pos.astype(F32)[:, None] * inv[None, :]
    cos = jnp.cos(ang)
    sin = jnp.sin(ang)
    return jnp.concatenate([cos, cos], -1), jnp.concatenate([-sin, sin], -1)


def kernel(x_prompt, x_sample, state_gdn_s, state_gdn_conv, cache_k, cache_v, cache_kidx, page_table,
           gdn_w_in, gdn_conv_w, gdn_a_log, gdn_dt_bias, gdn_norm_w, gdn_w_out,
           dsa_w_in, dsa_w_out, ln1_g, ln1_b, ln2_g, ln2_b,
           moe_wg, moe_bg, moe_we, moe_be, moe_w1, moe_w3, moe_w2):
    B, L, D = x_prompt.shape
    BS, LS, _ = x_sample.shape
    TP = B * L
    past = page_table.shape[1] * PAGE_SIZE
    x = jnp.concatenate([x_prompt.reshape(TP, D), x_sample.reshape(BS * LS, D)], axis=0)
    pos = jnp.concatenate([jnp.tile(jnp.arange(L), B), jnp.tile(past + jnp.arange(LS), BS)])
    cos, sin = _rope_tables(pos)
    nq = DSA_QD // HEAD_DIM
    nkv = DSA_KVD // HEAD_DIM
    rope_ranges = ((0, nq + nkv), (nq + 2 * nkv, nq + 2 * nkv + IDX_HEADS))
    wi_scale = IDX_HEADS ** -0.5 * IDX_DIM ** -0.5

    p_s, p_c, s_s, s_c = [], [], [], []
    p_k, p_v, p_ki, s_k, s_v, s_ki = [], [], [], [], [], []
    for i in range(DEPTH):
        j = i // N_MIXERS
        if i % N_MIXERS == 0:
            main, tail = project(x, gdn_w_in, j, GDN_MAIN, tn=512, tm_cap=688)
            gp = (gdn_conv_w[j], gdn_a_log[j], gdn_dt_bias[j], gdn_norm_w[j])
            c0 = jnp.zeros((B, CONV_W - 1, GDN_CONV_DIM), F32)
            s0 = jnp.zeros((B, GDN_HV, GDN_DK, GDN_DV), F32)
            op, sp = gdn_core(main, tail, c0, *gp, s0, B, L)
            cp = jnp.stack([main[b * L + L - (CONV_W - 1):(b + 1) * L, :GDN_CONV_DIM] for b in range(B)])
            pad_rows = ((0, 0), (0, GDN_CHUNK - LS), (0, 0))
            main_s = jnp.pad(main[TP:].reshape(BS, LS, GDN_MAIN), pad_rows).reshape(BS * GDN_CHUNK, GDN_MAIN)
            tail_s = jnp.pad(tail[TP:].reshape(BS, LS, LANES), pad_rows).reshape(BS * GDN_CHUNK, LANES)
            osp, ss = gdn_core(main_s, tail_s, state_gdn_conv[j], *gp, state_gdn_s[j], BS, GDN_CHUNK, valid=LS)
            os_ = osp.reshape(BS, GDN_CHUNK, GDN_VD)[:, :LS].reshape(BS * LS, GDN_VD)
            cs = main[TP:, :GDN_CONV_DIM].reshape(BS, LS, GDN_CONV_DIM)[:, LS - (CONV_W - 1):]
            p_s.append(sp); p_c.append(cp); s_s.append(ss); s_c.append(cs)
            w_out = gdn_w_out
        else:
            pp, kw = project(x, dsa_w_in, j, DSA_MAIN, tn=2 * HEAD_DIM, tm_cap=688, cos=cos, sin=sin,
                             rope_ranges=rope_ranges, tail_rope=1, tail_scale=wi_scale)
            op = dsa_prompt_attend(pp, kw, B, L)
            os_ = dsa_sample_attend(pp, kw, cache_k, cache_v, cache_kidx, page_table, j, TP, BS, LS).astype(BF16)
            kn = pp[TP:, DSA_QD:DSA_QD + DSA_KVD].reshape(BS, LS, N_KV_HEADS, HEAD_DIM)
            vn = pp[TP:, DSA_QD + DSA_KVD:DSA_QD + 2 * DSA_KVD].reshape(BS, LS, N_KV_HEADS, HEAD_DIM)
            kin = kw[TP:, :IDX_DIM].reshape(BS, LS, IDX_DIM)
            p_k.append(pp[:TP, DSA_QD:DSA_QD + DSA_KVD].reshape(B, L, N_KV_HEADS, HEAD_DIM))
            p_v.append(pp[:TP, DSA_QD + DSA_KVD:DSA_QD + 2 * DSA_KVD].reshape(B, L, N_KV_HEADS, HEAD_DIM))
            p_ki.append(kw[:TP, :IDX_DIM].reshape(B, L, IDX_DIM))
            s_k.append(kn); s_v.append(vn); s_ki.append(kin)
            w_out = dsa_w_out
        o_all = jnp.concatenate([op, os_], axis=0)
        x = matmul_res_ln(o_all, w_out, j, x, ln1_g[i], ln1_b[i])
        x = moe_layer(x, moe_wg[i], moe_bg[i], moe_we[i], moe_be[i], moe_w1, moe_w3, moe_w2, i,
                      ln2_g[i], ln2_b[i])
    xp = x[:TP].reshape(B, L, D)
    xs = x[TP:].reshape(BS, LS, D)
    return (xp, xs, jnp.stack(p_s), jnp.stack(p_c), jnp.stack(p_k), jnp.stack(p_v), jnp.stack(p_ki),
            jnp.stack(s_s), jnp.stack(s_c), jnp.stack(s_k), jnp.stack(s_v), jnp.stack(s_ki))
```

```python
import functools

import jax
import jax.numpy as jnp
from jax import lax
from jax.experimental import pallas as pl
from jax.experimental.pallas import tpu as pltpu

D_MODEL = 2048
DEPTH = 4
PAGE_SIZE = 128
N_MIXERS = 2
GDN_DK = 128
GDN_DV = 128
GDN_HK = D_MODEL // GDN_DK
GDN_HV = 2 * GDN_HK
GDN_KD = GDN_HK * GDN_DK
GDN_VD = GDN_HV * GDN_DV
GDN_CONV_DIM = 2 * GDN_KD + GDN_VD
GDN_MAIN = GDN_CONV_DIM + GDN_VD
CONV_W = 4
GDN_CHUNK = 64
HEAD_DIM = 128
N_HEADS = D_MODEL // HEAD_DIM
N_KV_HEADS = 4
KV_GROUP = N_HEADS // N_KV_HEADS
IDX_HEADS = 16
IDX_DIM = 128
DSA_QD = N_HEADS * HEAD_DIM
DSA_KVD = N_KV_HEADS * HEAD_DIM
DSA_MAIN = DSA_QD + 2 * DSA_KVD + IDX_HEADS * IDX_DIM
TOPK_MAX = 256
ROPE_THETA = 10000.0
N_GROUPS = 4
EXPERTS_PER_GROUP = 8
N_EXPERTS = N_GROUPS * EXPERTS_PER_GROUP
TOPK_EXPERTS = 2
D_EXPERT = D_MODEL // 4
ALPHA = (2 * DEPTH) ** 0.25
LN_EPS = 1e-5
NORM_EPS = 1e-6

LANES = 128
F32 = jnp.float32
BF16 = jnp.bfloat16
I32 = jnp.int32
VMEM_LIMIT = 56 * 1024 * 1024
INT_MIN = -2 ** 31
NEG_INF = float("-inf")


def _cparams(*sem):
    return pltpu.CompilerParams(dimension_semantics=sem, vmem_limit_bytes=VMEM_LIMIT)


def _bdot(a, b):
    return jnp.dot(a.astype(BF16), b.astype(BF16), preferred_element_type=F32)


def _bdot_nt(a, b):
    return lax.dot_general(a.astype(BF16), b.astype(BF16), (((1,), (1,)), ((), ())),
                           preferred_element_type=F32)


def _row_tile(T, cap):
    if T <= cap:
        return T
    best = None
    for t in range(16, cap + 1, 16):
        if T % t == 0:
            best = t
    assert best is not None, T
    return best


def _proj_kernel(x_ref, w_ref, wt_ref, *rest, rope_ranges, tail_rope, tail_scale):
    if rope_ranges:
        cos_ref, sin_ref, o_ref, t_ref, xs = rest
    else:
        o_ref, t_ref, xs = rest
    j = pl.program_id(1)

    @pl.when(j == 0)
    def _():
        xs[...] = x_ref[...].astype(BF16)

    def rope(a):
        return a * cos_ref[...] + pltpu.roll(a, HEAD_DIM // 2, 1) * sin_ref[...]

    xb = xs[...]
    acc = jnp.dot(xb, w_ref[...].astype(BF16), preferred_element_type=F32)
    if rope_ranges:
        per_tile = acc.shape[1] // HEAD_DIM
        parts = []
        for hh in range(per_tile):
            a = acc[:, hh * HEAD_DIM:(hh + 1) * HEAD_DIM]
            head = j * per_tile + hh
            is_rope = (head >= rope_ranges[0][0]) & (head < rope_ranges[0][1])
            for lo, hi in rope_ranges[1:]:
                is_rope = is_rope | ((head >= lo) & (head < hi))
            parts.append(jnp.where(is_rope, rope(a), a))
        acc = parts[0] if per_tile == 1 else jnp.concatenate(parts, axis=1)
    o_ref[...] = acc

    @pl.when(j == 0)
    def _():
        t = jnp.dot(xb, wt_ref[...].astype(BF16), preferred_element_type=F32)
        parts = []
        for hh in range(t.shape[1] // LANES):
            a = t[:, hh * LANES:(hh + 1) * LANES]
            parts.append(rope(a) if hh < tail_rope else a * tail_scale)
        t_ref[...] = parts[0] if len(parts) == 1 else jnp.concatenate(parts, axis=1)


def project(x, w, layer, n_main, tn, tm_cap, cos=None, sin=None, rope_ranges=(), tail_rope=0, tail_scale=1.0):
    T, D = x.shape
    tm = _row_tile(T, tm_cap)
    n_tail = w.shape[2] - n_main
    tw = -(-n_tail // LANES) * LANES
    w_tail = jnp.pad(w[layer, :, n_main:], ((0, 0), (0, tw - n_tail)))
    in_specs = [pl.BlockSpec((tm, D), lambda i, j: (i, 0)),
                pl.BlockSpec((None, D, tn), lambda i, j: (layer, 0, j)),
                pl.BlockSpec((D, tw), lambda i, j: (0, 0))]
    args = [x, w, w_tail]
    if rope_ranges:
        in_specs += [pl.BlockSpec((tm, LANES), lambda i, j: (i, 0))] * 2
        args += [cos, sin]
    return pl.pallas_call(
        functools.partial(_proj_kernel, rope_ranges=tuple(rope_ranges), tail_rope=tail_rope, tail_scale=tail_scale),
        grid=(T // tm, n_main // tn),
        in_specs=in_specs,
        out_specs=[pl.BlockSpec((tm, tn), lambda i, j: (i, j)),
                   pl.BlockSpec((tm, tw), lambda i, j: (i, 0))],
        out_shape=[jax.ShapeDtypeStruct((T, n_main), F32), jax.ShapeDtypeStruct((T, tw), F32)],
        scratch_shapes=[pltpu.VMEM((tm, D), BF16)],
        compiler_params=_cparams("parallel", "arbitrary"),
    )(*args)


def _mm_res_ln_kernel(x_ref, w_ref, r_ref, g_ref, b_ref, o_ref, *, nk):
    k = pl.program_id(1)
    part = jnp.dot(x_ref[...].astype(BF16), w_ref[...].astype(BF16), preferred_element_type=F32)

    @pl.when(k == 0)
    def _():
        o_ref[...] = part

    @pl.when(k > 0)
    def _():
        o_ref[...] += part

    @pl.when(k == nk - 1)
    def _():
        h = ALPHA * r_ref[...] + o_ref[...]
        mu = jnp.mean(h, -1, keepdims=True)
        hc = h - mu
        var = jnp.mean(hc * hc, -1, keepdims=True)
        o_ref[...] = hc * lax.rsqrt(var + LN_EPS) * g_ref[...] + b_ref[...]


def matmul_res_ln(x, w, layer, resid, g, b, tm_cap=688, tk=512):
    T, K = x.shape
    D = w.shape[2]
    tm = _row_tile(T, tm_cap)
    nk = K // tk
    return pl.pallas_call(
        functools.partial(_mm_res_ln_kernel, nk=nk),
        grid=(T // tm, nk),
        in_specs=[pl.BlockSpec((tm, tk), lambda i, k: (i, k)),
                  pl.BlockSpec((None, tk, D), lambda i, k: (layer, k, 0)),
                  pl.BlockSpec((tm, D), lambda i, k: (i, 0)),
                  pl.BlockSpec((1, D), lambda i, k: (0, 0)),
                  pl.BlockSpec((1, D), lambda i, k: (0, 0))],
        out_specs=pl.BlockSpec((tm, D), lambda i, k: (i, 0)),
        out_shape=jax.ShapeDtypeStruct((T, D), F32),
        compiler_params=_cparams("parallel", "arbitrary"),
    )(x, w, resid, g.reshape(1, D), b.reshape(1, D))


def _dsa_prompt_kernel(q_ref, qi0_ref, qi1_ref, wi_ref, k_ref, v_ref, ki_ref, o_ref,
                       kbf, vbf, kibf, key_s, bias_s, *, tq, seq, topk, s_step):
    i = pl.program_id(1)

    @pl.when(i == 0)
    def _():
        kbf[...] = k_ref[...].astype(BF16)
        vbf[...] = v_ref[...].astype(BF16)
        kibf[...] = ki_ref[...].astype(BF16)

    half = IDX_HEADS // 2
    qi_rows = [qi0_ref[:, h * IDX_DIM:(h + 1) * IDX_DIM].astype(BF16) for h in range(half)]
    qi_rows += [qi1_ref[:, h * IDX_DIM:(h + 1) * IDX_DIM].astype(BF16) for h in range(half)]
    qi_stack = jnp.concatenate(qi_rows, axis=0)
    wib = wi_ref[...].astype(BF16).astype(F32)
    q_rows = [jnp.concatenate([q_ref[:, (n * KV_GROUP + g) * HEAD_DIM:(n * KV_GROUP + g + 1) * HEAD_DIM]
                               for g in range(KV_GROUP)], axis=0).astype(BF16)
              for n in range(N_KV_HEADS)]

    def body(S):
        qpos = i * tq + lax.broadcasted_iota(I32, (tq, 1), 0)
        for c0 in range(0, S, s_step):
            s = _bdot_nt(qi_stack, kibf[c0:c0 + s_step, :])
            r = jnp.maximum(s, 0.0).astype(BF16).astype(F32)
            score = r[0:tq] * wib[:, 0:1]
            for h in range(1, IDX_HEADS):
                score = score + r[h * tq:(h + 1) * tq] * wib[:, h:h + 1]
            score = score + 0.0
            bits = pltpu.bitcast(score, I32)
            key = jnp.where(bits < 0, bits ^ jnp.int32(0x7FFFFFFF), bits)
            spos = c0 + lax.broadcasted_iota(I32, (tq, s_step), 1)
            key_s[:, c0:c0 + s_step] = jnp.where(spos <= qpos, key, jnp.int32(INT_MIN))

        def count_ge(cand):
            return jnp.sum((key_s[:, 0:S] >= cand).astype(I32), axis=1, keepdims=True)

        t0 = jnp.where(count_ge(jnp.zeros((tq, 1), I32)) >= topk, jnp.int32(0), jnp.int32(INT_MIN))
        t0 = jnp.broadcast_to(t0, (tq, 1))

        def bit_step(it, t):
            cand = t | jnp.left_shift(jnp.int32(1), 30 - it)
            return jnp.where(count_ge(cand) >= topk, cand, t)

        thr = lax.fori_loop(0, 31, bit_step, t0)

        keyv = key_s[:, 0:S]
        valid = lax.broadcasted_iota(I32, (tq, S), 1) <= qpos
        ge = keyv >= thr
        n_ge = jnp.sum((ge & valid).astype(I32), axis=1, keepdims=True)
        has_tie = jnp.max(n_ge) > topk
        bias_s[:, 0:S] = jnp.where(ge & valid, 0.0, NEG_INF)

        @pl.when(has_tie)
        def _():
            gt = keyv > thr
            n_gt = jnp.sum((gt & valid).astype(I32), axis=1, keepdims=True)
            room = (topk - n_gt).astype(F32)
            eq = ((keyv == thr) & valid)
            tri = (lax.broadcasted_iota(I32, (LANES, LANES), 0)
                   < lax.broadcasted_iota(I32, (LANES, LANES), 1)).astype(BF16)
            carry = jnp.zeros((tq, 1), F32)
            for c0 in range(0, S, LANES):
                eqc = eq[:, c0:c0 + LANES]
                before = carry + jnp.dot(eqc.astype(BF16), tri, preferred_element_type=F32)
                keep = (gt[:, c0:c0 + LANES] & valid[:, c0:c0 + LANES]) | (eqc & (before < room))
                bias_s[:, c0:c0 + LANES] = jnp.where(keep, 0.0, NEG_INF)
                carry = carry + jnp.sum(eqc.astype(F32), axis=1, keepdims=True)

        bias = bias_s[:, 0:S]
        for n in range(N_KV_HEADS):
            s = _bdot_nt(q_rows[n], kbf[0:S, n * HEAD_DIM:(n + 1) * HEAD_DIM]) * (HEAD_DIM ** -0.5)
            s = s.reshape(KV_GROUP, tq, S) + bias[None]
            m = jnp.max(s, axis=-1, keepdims=True)
            p = jnp.exp(s - m)
            l = jnp.sum(p, axis=-1, keepdims=True)
            o = jnp.dot(p.reshape(KV_GROUP * tq, S).astype(BF16), vbf[0:S, n * HEAD_DIM:(n + 1) * HEAD_DIM],
                        preferred_element_type=F32)
            o = o.reshape(KV_GROUP, tq, HEAD_DIM) / l
            for g in range(KV_GROUP):
                h = n * KV_GROUP + g
                o_ref[:, h * HEAD_DIM:(h + 1) * HEAD_DIM] = o[g].astype(o_ref.dtype)

    n_var = seq // s_step
    per = (seq // tq) // n_var
    for c in range(n_var):
        @pl.when(i // per == c)
        def _(c=c):
            body((c + 1) * s_step)


def dsa_prompt_attend(pp, tail, batch, seq, tq=128, s_step=512):
    topk = min(TOPK_MAX, seq // 4)
    s_step = min(s_step, seq)
    nq = seq // tq
    kcol = DSA_QD // DSA_KVD
    qicol = (DSA_QD + 2 * DSA_KVD) // (IDX_HEADS * IDX_DIM // 2)
    kicol = (DSA_QD + 2 * DSA_KVD + IDX_HEADS * IDX_DIM) // IDX_DIM
    assert (DSA_QD + 2 * DSA_KVD) % (IDX_HEADS * IDX_DIM // 2) == 0
    hq = IDX_HEADS * IDX_DIM // 2
    return pl.pallas_call(
        functools.partial(_dsa_prompt_kernel, tq=tq, seq=seq, topk=topk, s_step=s_step),
        grid=(batch, nq),
        in_specs=[pl.BlockSpec((tq, DSA_QD), lambda b, i: (b * nq + i, 0)),
                  pl.BlockSpec((tq, hq), lambda b, i: (b * nq + i, qicol)),
                  pl.BlockSpec((tq, hq), lambda b, i: (b * nq + i, qicol + 1)),
                  pl.BlockSpec((tq, LANES), lambda b, i: (b * nq + i, 1)),
                  pl.BlockSpec((seq, DSA_KVD), lambda b, i: (b, kcol)),
                  pl.BlockSpec((seq, DSA_KVD), lambda b, i: (b, kcol + 1)),
                  pl.BlockSpec((seq, IDX_DIM), lambda b, i: (b, 0))],
        out_specs=pl.BlockSpec((tq, DSA_QD), lambda b, i: (b * nq + i, 0)),
        out_shape=jax.ShapeDtypeStruct((batch * seq, DSA_QD), BF16),
        scratch_shapes=[pltpu.VMEM((seq, DSA_KVD), BF16), pltpu.VMEM((seq, DSA_KVD), BF16),
                        pltpu.VMEM((seq, IDX_DIM), BF16),
                        pltpu.VMEM((tq, seq), I32), pltpu.VMEM((tq, seq), F32)],
        compiler_params=_cparams("parallel", "arbitrary"),
    )(pp, pp, pp, tail, pp, pp, tail)


def _order_key(score):
    bits = pltpu.bitcast(score + 0.0, I32)
    return jnp.where(bits < 0, bits ^ jnp.int32(0x7FFFFFFF), bits)


def _dsa_sample_kernel(pt_ref, q_ref, qi0_ref, qi1_ref, wi_ref, kn_ref, vn_ref, kin_ref, ck_hbm, cv_hbm, cki_hbm,
                       o_ref, kibuf, kvbuf, key_s, bias_s, sc_s, p_s, sem_ki, sem_kv,
                       *, layer, ls, n_pages, ppc, topk):
    b = pl.program_id(0)
    past = n_pages * PAGE_SIZE
    S = past + LANES
    ck = ppc * PAGE_SIZE
    prow = PAGE_SIZE * N_KV_HEADS
    n_chunks = n_pages // ppc
    gk = 4 * PAGE_SIZE
    rows = KV_GROUP * ls

    def ki_copy(p):
        return pltpu.make_async_copy(cki_hbm.at[layer, pt_ref[b, p]], kibuf.at[p], sem_ki)

    def kv_copy(src, c, slot, i):
        return pltpu.make_async_copy(src.at[layer, pt_ref[b, c * ppc + i]],
                                     kvbuf.at[slot, pl.ds(i * prow, prow)], sem_kv.at[slot])

    def head_rows(slot, n):
        return kvbuf[slot, pl.ds(n, ck, stride=N_KV_HEADS), :].astype(BF16)

    def start_chunk(src, c, slot):
        for i in range(ppc):
            kv_copy(src, c, slot, i).start()

    def wait_chunk(src, c, slot):
        for i in range(ppc):
            kv_copy(src, c, slot, i).wait()

    def ki_start(p, carry):
        ki_copy(p).start()
        return carry

    def ki_wait(p, carry):
        ki_copy(p).wait()
        return carry

    lax.fori_loop(0, n_pages, ki_start, 0)
    start_chunk(ck_hbm, 0, 0)

    half = IDX_HEADS // 2
    qi_stack = jnp.concatenate([qi0_ref[:, h * IDX_DIM:(h + 1) * IDX_DIM] for h in range(half)]
                               + [qi1_ref[:, h * IDX_DIM:(h + 1) * IDX_DIM] for h in range(half)],
                               axis=0).astype(BF16)
    wib = wi_ref[...].astype(BF16).astype(F32)
    q_rows = [jnp.concatenate([q_ref[:, (n * KV_GROUP + g) * HEAD_DIM:(n * KV_GROUP + g + 1) * HEAD_DIM]
                               for g in range(KV_GROUP)], axis=0).astype(BF16)
              for n in range(N_KV_HEADS)]
    qpos = past + lax.broadcasted_iota(I32, (ls, 1), 0)
    zpad = jnp.zeros((LANES - ls, DSA_KVD), F32)

    def index_keys(ki_rows):
        s = _bdot_nt(qi_stack, ki_rows)
        r = jnp.maximum(s, 0.0).astype(BF16).astype(F32)
        score = r[0:ls] * wib[:, 0:1]
        for h in range(1, IDX_HEADS):
            score = score + r[h * ls:(h + 1) * ls] * wib[:, h:h + 1]
        return _order_key(score)

    lax.fori_loop(0, n_pages, ki_wait, 0)

    def index_step(g, carry):
        kic = kibuf[pl.ds(g * (gk // PAGE_SIZE), gk // PAGE_SIZE)].reshape(gk, IDX_DIM)
        key_s[:, pl.ds(pl.multiple_of(g * gk, gk), gk)] = index_keys(kic)
        return carry

    lax.fori_loop(0, past // gk, index_step, 0)
    kin_pad = jnp.concatenate([kin_ref[...], zpad[:, 0:IDX_DIM]], axis=0)
    new_pos = past + lax.broadcasted_iota(I32, (ls, LANES), 1)
    key_s[:, past:S] = jnp.where(new_pos <= qpos, index_keys(kin_pad), jnp.int32(INT_MIN))

    def count_ge(cand):
        return jnp.sum((key_s[...] >= cand).astype(I32), axis=1, keepdims=True)

    t0 = jnp.where(count_ge(jnp.zeros((ls, 1), I32)) >= topk, jnp.int32(0), jnp.int32(INT_MIN))

    def bit_step(it, t):
        cand = t | jnp.left_shift(jnp.int32(1), 30 - it)
        return jnp.where(count_ge(cand) >= topk, cand, t)

    thr = lax.fori_loop(0, 31, bit_step, t0)
    keyv = key_s[...]
    valid = lax.broadcasted_iota(I32, (ls, S), 1) <= qpos
    ge = (keyv >= thr) & valid
    n_ge = jnp.sum(ge.astype(I32), axis=1, keepdims=True)
    bias_s[...] = jnp.where(ge, 0.0, NEG_INF)

    @pl.when(jnp.max(n_ge) > topk)
    def _():
        n_gt = jnp.sum(((keyv > thr) & valid).astype(I32), axis=1, keepdims=True)
        room = (topk - n_gt).astype(F32)
        tri = (lax.broadcasted_iota(I32, (LANES, LANES), 0)
               < lax.broadcasted_iota(I32, (LANES, LANES), 1)).astype(BF16)

        def tie_step(c, carry):
            off = pl.multiple_of(c * LANES, LANES)
            kc = key_s[:, pl.ds(off, LANES)]
            ok = (off + lax.broadcasted_iota(I32, (ls, LANES), 1)) <= qpos
            eq = (kc == thr) & ok
            before = carry + jnp.dot(eq.astype(BF16), tri, preferred_element_type=F32)
            keep = ((kc > thr) & ok) | (eq & (before < room))
            bias_s[:, pl.ds(off, LANES)] = jnp.where(keep, 0.0, NEG_INF)
            return carry + jnp.sum(eq.astype(F32), axis=1, keepdims=True)

        lax.fori_loop(0, S // LANES, tie_step, jnp.zeros((ls, 1), F32))

    def masked_scores(n, k_rows, bias):
        s = _bdot_nt(q_rows[n], k_rows) * (HEAD_DIM ** -0.5)
        return (s.reshape(KV_GROUP, ls, s.shape[1]) + bias[None]).reshape(rows, s.shape[1])

    def k_step(c, carry):
        slot = c % 2
        wait_chunk(ck_hbm, c, slot)

        @pl.when(c + 1 < n_chunks)
        def _():
            start_chunk(ck_hbm, c + 1, 1 - slot)

        @pl.when(c + 1 == n_chunks)
        def _():
            start_chunk(cv_hbm, 0, 1 - slot)

        off = pl.multiple_of(c * ck, ck)
        bias = bias_s[:, pl.ds(off, ck)]
        for n in range(N_KV_HEADS):
            sc_s[n * rows:(n + 1) * rows, pl.ds(off, ck)] = masked_scores(n, head_rows(slot, n), bias)
        return carry

    lax.fori_loop(0, n_chunks, k_step, 0)
    kn_pad = jnp.concatenate([kn_ref[...], zpad], axis=0).astype(BF16)
    for n in range(N_KV_HEADS):
        sc_s[n * rows:(n + 1) * rows, past:S] = masked_scores(
            n, kn_pad[:, n * HEAD_DIM:(n + 1) * HEAD_DIM], bias_s[:, past:S])

    sc = sc_s[...]
    m = jnp.max(sc, axis=1, keepdims=True)
    p = jnp.exp(sc - m)
    l = jnp.sum(p, axis=1, keepdims=True)
    p_s[...] = p.astype(BF16)

    def v_step(v, acc):
        slot = (n_chunks + v) % 2
        wait_chunk(cv_hbm, v, slot)

        @pl.when(v + 1 < n_chunks)
        def _():
            start_chunk(cv_hbm, v + 1, 1 - slot)

        off = pl.multiple_of(v * ck, ck)
        return tuple(acc[n] + jnp.dot(p_s[n * rows:(n + 1) * rows, pl.ds(off, ck)], head_rows(slot, n),
                                      preferred_element_type=F32)
                     for n in range(N_KV_HEADS))

    acc = lax.fori_loop(0, n_chunks, v_step, tuple(jnp.zeros((rows, HEAD_DIM), F32) for _ in range(N_KV_HEADS)))
    vn_pad = jnp.concatenate([vn_ref[...], zpad], axis=0).astype(BF16)
    for n in range(N_KV_HEADS):
        o = acc[n] + jnp.dot(p_s[n * rows:(n + 1) * rows, past:S], vn_pad[:, n * HEAD_DIM:(n + 1) * HEAD_DIM],
                             preferred_element_type=F32)
        o = o / l[n * rows:(n + 1) * rows]
        for g in range(KV_GROUP):
            h = n * KV_GROUP + g
            o_ref[:, h * HEAD_DIM:(h + 1) * HEAD_DIM] = o[g * ls:(g + 1) * ls]


def dsa_sample_attend(pp, tail, cache_k, cache_v, cache_kidx, page_table, layer, row0, batch, ls, ppc=16):
    n_pool = cache_k.shape[1]
    n_pages = page_table.shape[1]
    past = n_pages * PAGE_SIZE
    topk = min(TOPK_MAX, (past + ls) // 4)
    assert row0 % ls == 0 and n_pages % ppc == 0 and n_pages % 4 == 0 and ls % 8 == 0
    rb = row0 // ls
    ck = cache_k.reshape(cache_k.shape[0], n_pool, PAGE_SIZE * N_KV_HEADS, HEAD_DIM)
    cv = cache_v.reshape(cache_v.shape[0], n_pool, PAGE_SIZE * N_KV_HEADS, HEAD_DIM)
    kcol = DSA_QD // DSA_KVD
    hq = IDX_HEADS * IDX_DIM // 2
    qicol = (DSA_QD + 2 * DSA_KVD) // hq
    kicol = (DSA_QD + 2 * DSA_KVD + IDX_HEADS * IDX_DIM) // IDX_DIM
    S = past + LANES
    return pl.pallas_call(
        functools.partial(_dsa_sample_kernel, layer=layer, ls=ls, n_pages=n_pages, ppc=ppc, topk=topk),
        grid_spec=pltpu.PrefetchScalarGridSpec(
            num_scalar_prefetch=1,
            grid=(batch,),
            in_specs=[pl.BlockSpec((ls, DSA_QD), lambda b, pt: (rb + b, 0)),
                      pl.BlockSpec((ls, hq), lambda b, pt: (rb + b, qicol)),
                      pl.BlockSpec((ls, hq), lambda b, pt: (rb + b, qicol + 1)),
                      pl.BlockSpec((ls, LANES), lambda b, pt: (rb + b, 1)),
                      pl.BlockSpec((ls, DSA_KVD), lambda b, pt: (rb + b, kcol)),
                      pl.BlockSpec((ls, DSA_KVD), lambda b, pt: (rb + b, kcol + 1)),
                      pl.BlockSpec((ls, IDX_DIM), lambda b, pt: (rb + b, 0)),
                      pl.BlockSpec(memory_space=pl.ANY),
                      pl.BlockSpec(memory_space=pl.ANY),
                      pl.BlockSpec(memory_space=pl.ANY)],
            out_specs=pl.BlockSpec((ls, DSA_QD), lambda b, pt: (b, 0)),
            scratch_shapes=[pltpu.VMEM((n_pages, PAGE_SIZE, IDX_DIM), F32),
                            pltpu.VMEM((2, ppc * PAGE_SIZE * N_KV_HEADS, HEAD_DIM), F32),
                            pltpu.VMEM((ls, S), I32), pltpu.VMEM((ls, S), F32),
                            pltpu.VMEM((N_HEADS * ls, S), F32), pltpu.VMEM((N_HEADS * ls, S), BF16),
                            pltpu.SemaphoreType.DMA(()), pltpu.SemaphoreType.DMA((2,))]),
        out_shape=jax.ShapeDtypeStruct((batch * ls, DSA_QD), F32),
        compiler_params=_cparams("arbitrary"),
    )(page_table, pp, pp, pp, tail, pp, pp, tail, ck, cv, cache_kidx)


def _split3(a):
    hi = a.astype(BF16)
    lo = (a - hi.astype(F32)).astype(BF16)
    return hi, lo


def _dot3(a_parts, b_parts):
    ah, al = a_parts
    bh, bl = b_parts
    d = functools.partial(jnp.dot, preferred_element_type=F32)
    return d(ah, bh) + (d(ah, bl) + d(al, bh))


def _tri_inverse(mats, order):
    n = mats[0].shape[0]
    eye = (lax.broadcasted_iota(I32, (n, n), 0) == lax.broadcasted_iota(I32, (n, n), 1)).astype(F32)
    ps = [eye - a for a in mats]
    xss = [_split3(-a) for a in mats]
    steps = max(0, (order - 1).bit_length() - 1)
    for _ in range(steps):
        xss = [_split3(_dot3(xs, xs)) for xs in xss]
        ps = [p + _dot3(_split3(p), xs) for p, xs in zip(ps, xss)]
    return ps


def _silu(x):
    return x * jax.nn.sigmoid(x)


def _gdn_kernel(xq_ref, xk_ref, xv_ref, z_ref, tail_ref, cq_ref, ck_ref, cv_ref, wq_ref, wk_ref, wv_ref,
                alog_ref, dtb_ref, nw_ref, s0_ref, o_ref, s_ref,
                beta_s, g_s, u_s, w_s, qk_s, qg_s, kd_s, el_s, *, seq, chunk, valid, hpb, unroll):
    hb = pl.program_id(1)
    C = chunk
    N = seq // C
    HALO = 8
    NV = 2 * hpb
    tail = tail_ref[...]
    beta = jax.nn.sigmoid(tail)
    x = tail + dtb_ref[...]
    softplus = jnp.maximum(x, 0.0) + jnp.log1p(jnp.exp(-jnp.abs(x)))
    g = -jnp.exp(alog_ref[...]) * softplus
    if valid < seq:
        is_real = lax.broadcasted_iota(I32, (seq, LANES), 0) < valid
        beta = jnp.where(is_real, beta, 0.0)
        g = jnp.where(is_real, g, 0.0)
    beta_s[...] = beta
    g_s[...] = g

    R = NV * C
    row = lax.broadcasted_iota(I32, (R, R), 0)
    col = lax.broadcasted_iota(I32, (R, R), 1)
    log2c = C.bit_length() - 1
    same_head = lax.shift_right_logical(row, log2c) == lax.shift_right_logical(col, log2c)
    incl = same_head & (row >= col)
    strict = same_head & (row > col)
    lane = lax.broadcasted_iota(I32, (C, LANES), 1)
    sub_t = lax.broadcasted_iota(I32, (LANES, C), 0)
    rowc = lax.broadcasted_iota(I32, (C, LANES), 0)

    def conv(xref, cref, wref, r0, c):
        prev = xref[pl.ds(pl.multiple_of(jnp.maximum(r0 - HALO, 0), HALO), HALO), :]
        win = jnp.concatenate([jnp.where(c == 0, cref[0], prev), xref[pl.ds(r0, C), :]], axis=0)
        acc = win[HALO - 3:HALO - 3 + C] * wref[0:1, :]
        for j in range(1, CONV_W):
            acc = acc + win[HALO - 3 + j:HALO - 3 + j + C] * wref[j:j + 1, :]
        return _silu(acc)

    def l2n(t):
        return t * lax.rsqrt(jnp.sum(t * t, -1, keepdims=True) + NORM_EPS)

    def prep_inputs(c):
        r0 = pl.multiple_of(c * C, C)
        qc = conv(xq_ref, cq_ref, wq_ref, r0, c)
        kc = conv(xk_ref, ck_ref, wk_ref, r0, c)
        vv = conv(xv_ref, cv_ref, wv_ref, r0, c)
        beta = beta_s[pl.ds(r0, C), :]
        gc = g_s[pl.ds(r0, C), :]
        sh = 1
        while sh < C:
            gc = gc + jnp.where(rowc >= sh, pltpu.roll(gc, sh, 0), 0.0)
            sh *= 2
        gc_t = gc.T
        qn = [l2n(qc[:, hl * GDN_DK:(hl + 1) * GDN_DK]) * (GDN_DK ** -0.5) for hl in range(hpb)]
        kn = [l2n(kc[:, hl * GDN_DK:(hl + 1) * GDN_DK]) for hl in range(hpb)]
        bcols, gcols, grows = [], [], []
        for e in range(NV):
            hv = NV * hb + e
            bcols.append(jnp.sum(jnp.where(lane == hv, beta, 0.0), axis=1, keepdims=True))
            gcols.append(jnp.sum(jnp.where(lane == GDN_HV + hv, gc, 0.0), axis=1, keepdims=True))
            grows.append(jnp.sum(jnp.where(sub_t == GDN_HV + hv, gc_t, 0.0), axis=0, keepdims=True))
        bcol = jnp.concatenate(bcols, axis=0)
        gcol = jnp.concatenate(gcols, axis=0)
        grow = jnp.concatenate(grows, axis=1)
        kn_st = jnp.concatenate([kn[e // 2] for e in range(NV)], axis=0)
        qn_st = jnp.concatenate([qn[e // 2] for e in range(NV)], axis=0)
        v_st = jnp.concatenate([vv[:, e * GDN_DV:(e + 1) * GDN_DV] for e in range(NV)], axis=0)
        decay = jnp.exp(jnp.where(incl, gcol - grow, NEG_INF))
        kb = kn_st * bcol
        a = jnp.where(strict, _bdot_nt(kb, kn_st) * decay, 0.0)
        qk_s[c] = jnp.where(incl, _bdot_nt(qn_st, kn_st) * decay, 0.0).astype(BF16)
        qg_s[c] = (qn_st * jnp.exp(gcol)).astype(BF16)
        for e in range(NV):
            glast = grows[e][:, C - 1:C]
            kd = kn[e // 2] * jnp.exp(glast - gcols[e])
            kd_s[e, c] = kd.T.astype(BF16)
            el_s[e, c] = jnp.broadcast_to(jnp.exp(glast), (8, LANES))
        return a, v_st * bcol, kb * jnp.exp(gcol)

    def prep(it, carry):
        cs = [it * unroll + u for u in range(unroll)]
        ins = [prep_inputs(c) for c in cs]
        tmats = _tri_inverse([a for a, _, _ in ins], C)
        for c, tmat, (_, vb, kbg) in zip(cs, tmats, ins):
            u_s[c] = _bdot(tmat, vb)
            w_s[c] = _bdot(tmat, kbg).astype(BF16)
        return carry

    lax.fori_loop(0, N // unroll, prep, 0)

    nw = nw_ref[...]
    s_ref[...] = s0_ref[...]

    def scan(c, carry):
        r0 = pl.multiple_of(c * C, C)
        d = functools.partial(jnp.dot, preferred_element_type=F32)
        u = u_s[c]
        w = w_s[c]
        qg = qg_s[c]
        sts = [s_ref[0, e] for e in range(NV)]
        sbs = [st.astype(BF16) for st in sts]
        vb = jnp.concatenate([u[e * C:(e + 1) * C] - d(w[e * C:(e + 1) * C], sbs[e]) for e in range(NV)],
                             axis=0).astype(BF16)
        o_intra = d(qk_s[c], vb)
        for e in range(NV):
            o = d(qg[e * C:(e + 1) * C], sbs[e]) + o_intra[e * C:(e + 1) * C]
            s_ref[0, e] = sts[e] * el_s[e, c][0:1, :] + d(kd_s[e, c], vb[e * C:(e + 1) * C])
            zf = z_ref[pl.ds(r0, C), e * GDN_DV:(e + 1) * GDN_DV]
            og = o * lax.rsqrt(jnp.mean(o * o, -1, keepdims=True) + NORM_EPS) * nw * _silu(zf)
            o_ref[pl.ds(r0, C), e * GDN_DV:(e + 1) * GDN_DV] = og.astype(o_ref.dtype)
        return carry

    lax.fori_loop(0, N, scan, 0)


def gdn_core(main, tail, conv0, conv_w, a_log, dt_bias, norm_w, s0, batch, seq, valid=None, hpb=2, unroll=4):
    valid = seq if valid is None else valid
    C = min(GDN_CHUNK, seq)
    N = seq // C
    assert seq % C == 0
    unroll = unroll if N % unroll == 0 else 1
    conv0p = jnp.pad(conv0, ((0, 0), (8 - (CONV_W - 1), 0), (0, 0)))
    alog = jnp.pad(a_log, (GDN_HV, LANES - 2 * GDN_HV)).reshape(1, LANES)
    dtb = jnp.pad(dt_bias, (GDN_HV, LANES - 2 * GDN_HV)).reshape(1, LANES)
    qw = GDN_DK * hpb
    vw = 2 * GDN_DV * hpb
    kblk = GDN_KD // qw
    vblk = 2 * GDN_KD // vw
    zblk = GDN_CONV_DIM // vw
    nv = 2 * hpb
    return pl.pallas_call(
        functools.partial(_gdn_kernel, seq=seq, chunk=C, valid=valid, hpb=hpb, unroll=unroll),
        grid=(batch, GDN_HK // hpb),
        in_specs=[pl.BlockSpec((seq, qw), lambda b, h: (b, h)),
                  pl.BlockSpec((seq, qw), lambda b, h: (b, kblk + h)),
                  pl.BlockSpec((seq, vw), lambda b, h: (b, vblk + h)),
                  pl.BlockSpec((seq, vw), lambda b, h: (b, zblk + h)),
                  pl.BlockSpec((seq, LANES), lambda b, h: (b, 0)),
                  pl.BlockSpec((1, 8, qw), lambda b, h: (b, 0, h)),
                  pl.BlockSpec((1, 8, qw), lambda b, h: (b, 0, kblk + h)),
                  pl.BlockSpec((1, 8, vw), lambda b, h: (b, 0, vblk + h)),
                  pl.BlockSpec((CONV_W, qw), lambda b, h: (0, h)),
                  pl.BlockSpec((CONV_W, qw), lambda b, h: (0, kblk + h)),
                  pl.BlockSpec((CONV_W, vw), lambda b, h: (0, vblk + h)),
                  pl.BlockSpec((1, LANES), lambda b, h: (0, 0)),
                  pl.BlockSpec((1, LANES), lambda b, h: (0, 0)),
                  pl.BlockSpec((1, GDN_DV), lambda b, h: (0, 0)),
                  pl.BlockSpec((1, nv, GDN_DK, GDN_DV), lambda b, h: (b, h, 0, 0))],
        out_specs=[pl.BlockSpec((seq, vw), lambda b, h: (b, h)),
                   pl.BlockSpec((1, nv, GDN_DK, GDN_DV), lambda b, h: (b, h, 0, 0))],
        out_shape=[jax.ShapeDtypeStruct((batch * seq, GDN_VD), BF16),
                   jax.ShapeDtypeStruct((batch, GDN_HV, GDN_DK, GDN_DV), F32)],
        scratch_shapes=[pltpu.VMEM((seq, LANES), F32), pltpu.VMEM((seq, LANES), F32),
                        pltpu.VMEM((N, nv * C, GDN_DV), F32), pltpu.VMEM((N, nv * C, GDN_DK), BF16),
                        pltpu.VMEM((N, nv * C, nv * C), BF16), pltpu.VMEM((N, nv * C, GDN_DK), BF16),
                        pltpu.VMEM((nv, N, GDN_DK, C), BF16), pltpu.VMEM((nv, N, 8, LANES), F32)],
        compiler_params=_cparams("parallel", "parallel"),
    )(main, main, main, main, tail, conv0p, conv0p, conv0p, conv_w, conv_w, conv_w,
      alog, dtb, norm_w.reshape(1, GDN_DV), s0)


def _expert_kernel(be_ref, x_ref, w1_ref, w3_ref, w2_ref, o_ref):
    del be_ref
    x = x_ref[...].astype(BF16)
    h1 = jnp.dot(x, w1_ref[...].astype(BF16), preferred_element_type=F32)
    h3 = jnp.dot(x, w3_ref[...].astype(BF16), preferred_element_type=F32)
    h = _silu(h1) * h3
    o_ref[...] = jnp.dot(h.astype(BF16), w2_ref[...].astype(BF16), preferred_element_type=F32)


def _combine_ln_kernel(x_ref, y0_ref, y1_ref, gate_ref, g_ref, b_ref, o_ref):
    gate = gate_ref[...]
    h = ALPHA * x_ref[...] + (y0_ref[...] * gate[:, 0:1] + y1_ref[...] * gate[:, 1:2])
    mu = jnp.mean(h, -1, keepdims=True)
    hc = h - mu
    var = jnp.mean(hc * hc, -1, keepdims=True)
    o_ref[...] = hc * lax.rsqrt(var + LN_EPS) * g_ref[...] + b_ref[...]


def moe_layer(xt, wg, bg, we, be, w1, w3, w2, layer, ln_g, ln_b, blk=128):
    T, D = xt.shape
    E = N_EXPERTS
    K = TOPK_EXPERTS
    xb16 = xt.astype(BF16)
    lg = jnp.dot(xb16, wg.astype(BF16), preferred_element_type=F32) + bg
    pg = jax.nn.softmax(lg, axis=-1)
    gsel = jnp.argmax(lg, axis=-1)
    le = (jnp.dot(xb16, we.astype(BF16), preferred_element_type=F32) + be).reshape(T, N_GROUPS, EXPERTS_PER_GROUP)
    le_g = jnp.take_along_axis(le, gsel[:, None, None], axis=1)[:, 0]
    pe = jax.nn.softmax(le_g, axis=-1)
    top_p, top_i = lax.top_k(pe, K)
    gate = top_p / jnp.sum(top_p, -1, keepdims=True) * jnp.take_along_axis(pg, gsel[:, None], axis=1)
    eidx = (gsel[:, None] * EXPERTS_PER_GROUP + top_i).astype(I32)

    A = T * K
    nb = A // blk + E
    cnt = jnp.sum((eidx[:, :, None] == jnp.arange(E, dtype=I32)).astype(I32), axis=1)
    cum = jnp.cumsum(cnt, axis=0) - cnt
    counts = jnp.sum(cnt, axis=0)
    padded = (counts + blk - 1) // blk * blk
    pend = jnp.cumsum(padded)
    pstart = pend - padded
    dest = pstart[eidx] + jnp.take_along_axis(cum, eidx, axis=1)
    tok = jnp.broadcast_to(jnp.arange(T, dtype=I32)[:, None], (T, K))
    slot_tok = jnp.zeros((nb * blk,), I32).at[dest.reshape(-1)].set(tok.reshape(-1))
    blk_start = jnp.arange(nb, dtype=I32) * blk
    blk_e = jnp.minimum(jnp.sum((pend[None, :] <= blk_start[:, None]).astype(I32), axis=1), E - 1).astype(I32)
    xb = xb16[slot_tok]

    yb = pl.pallas_call(
        _expert_kernel,
        grid_spec=pltpu.PrefetchScalarGridSpec(
            num_scalar_prefetch=1,
            grid=(nb,),
            in_specs=[pl.BlockSpec((blk, D), lambda i, be_: (i, 0)),
                      pl.BlockSpec((None, None, D, D_EXPERT), lambda i, be_: (layer, be_[i], 0, 0)),
                      pl.BlockSpec((None, None, D, D_EXPERT), lambda i, be_: (layer, be_[i], 0, 0)),
                      pl.BlockSpec((None, None, D_EXPERT, D), lambda i, be_: (layer, be_[i], 0, 0))],
            out_specs=pl.BlockSpec((blk, D), lambda i, be_: (i, 0))),
        out_shape=jax.ShapeDtypeStruct((nb * blk, D), F32),
        compiler_params=_cparams("arbitrary"),
    )(blk_e, xb, w1, w3, w2)

    y0 = yb[dest[:, 0]]
    y1 = yb[dest[:, 1]]
    tm = _row_tile(T, 344)
    return pl.pallas_call(
        _combine_ln_kernel,
        grid=(T // tm,),
        in_specs=[pl.BlockSpec((tm, D), lambda i: (i, 0)),
                  pl.BlockSpec((tm, D), lambda i: (i, 0)),
                  pl.BlockSpec((tm, D), lambda i: (i, 0)),
                  pl.BlockSpec((tm, K), lambda i: (i, 0)),
                  pl.BlockSpec((1, D), lambda i: (0, 0)),
                  pl.BlockSpec((1, D), lambda i: (0, 0))],
        out_specs=pl.BlockSpec((tm, D), lambda i: (i, 0)),
        out_shape=jax.ShapeDtypeStruct((T, D), F32),
        compiler_params=_cparams("parallel"),
    )(xt, y0, y1, gate, ln_g.reshape(1, D), ln_b.reshape(1, D))


def _rope_tables(pos):
    half = HEAD_DIM // 2
    inv = jnp.power(ROPE_THETA, -jnp.arange(half, dtype=F32) / half)
    ang = pos.astype(F32)[:, None] * inv[None, :]
    cos = jnp.cos(ang)
    sin = jnp.sin(ang)
    return jnp.concatenate([cos, cos], -1), jnp.concatenate([-sin, sin], -1)


def kernel(x_prompt, x_sample, state_gdn_s, state_gdn_conv, cache_k, cache_v, cache_kidx, page_table,
           gdn_w_in, gdn_conv_w, gdn_a_log, gdn_dt_bias, gdn_norm_w, gdn_w_out,
           dsa_w_in, dsa_w_out, ln1_g, ln1_b, ln2_g, ln2_b,
           moe_wg, moe_bg, moe_we, moe_be, moe_w1, moe_w3, moe_w2):
    B, L, D = x_prompt.shape
    BS, LS, _ = x_sample.shape
    TP = B * L
    past = page_table.shape[1] * PAGE_SIZE
    x = jnp.concatenate([x_prompt.reshape(TP, D), x_sample.reshape(BS * LS, D)], axis=0)
    pos = jnp.concatenate([jnp.tile(jnp.arange(L), B), jnp.tile(past + jnp.arange(LS), BS)])
    cos, sin = _rope_tables(pos)
    nq = DSA_QD // HEAD_DIM
    nkv = DSA_KVD // HEAD_DIM
    rope_ranges = ((0, nq + nkv), (nq + 2 * nkv, nq + 2 * nkv + IDX_HEADS))
    wi_scale = IDX_HEADS ** -0.5 * IDX_DIM ** -0.5

    p_s, p_c, s_s, s_c = [], [], [], []
    p_k, p_v, p_ki, s_k, s_v, s_ki = [], [], [], [], [], []
    for i in range(DEPTH):
        j = i // N_MIXERS
        if i % N_MIXERS == 0:
            main, tail = project(x, gdn_w_in, j, GDN_MAIN, tn=512, tm_cap=688)
            gp = (gdn_conv_w[j], gdn_a_log[j], gdn_dt_bias[j], gdn_norm_w[j])
            c0 = jnp.zeros((B, CONV_W - 1, GDN_CONV_DIM), F32)
            s0 = jnp.zeros((B, GDN_HV, GDN_DK, GDN_DV), F32)
            op, sp = gdn_core(main, tail, c0, *gp, s0, B, L)
            cp = jnp.stack([main[b * L + L - (CONV_W - 1):(b + 1) * L, :GDN_CONV_DIM] for b in range(B)])
            pad_rows = ((0, 0), (0, GDN_CHUNK - LS), (0, 0))
            main_s = jnp.pad(main[TP:].reshape(BS, LS, GDN_MAIN), pad_rows).reshape(BS * GDN_CHUNK, GDN_MAIN)
            tail_s = jnp.pad(tail[TP:].reshape(BS, LS, LANES), pad_rows).reshape(BS * GDN_CHUNK, LANES)
            osp, ss = gdn_core(main_s, tail_s, state_gdn_conv[j], *gp, state_gdn_s[j], BS, GDN_CHUNK, valid=LS)
            os_ = osp.reshape(BS, GDN_CHUNK, GDN_VD)[:, :LS].reshape(BS * LS, GDN_VD)
            cs = main[TP:, :GDN_CONV_DIM].reshape(BS, LS, GDN_CONV_DIM)[:, LS - (CONV_W - 1):]
            p_s.append(sp); p_c.append(cp); s_s.append(ss); s_c.append(cs)
            w_out = gdn_w_out
        else:
            pp, kw = project(x, dsa_w_in, j, DSA_MAIN, tn=2 * HEAD_DIM, tm_cap=688, cos=cos, sin=sin,
                             rope_ranges=rope_ranges, tail_rope=1, tail_scale=wi_scale)
            op = dsa_prompt_attend(pp, kw, B, L)
            os_ = dsa_sample_attend(pp, kw, cache_k, cache_v, cache_kidx, page_table, j, TP, BS, LS).astype(BF16)
            kn = pp[TP:, DSA_QD:DSA_QD + DSA_KVD].reshape(BS, LS, N_KV_HEADS, HEAD_DIM)
            vn = pp[TP:, DSA_QD + DSA_KVD:DSA_QD + 2 * DSA_KVD].reshape(BS, LS, N_KV_HEADS, HEAD_DIM)
            kin = kw[TP:, :IDX_DIM].reshape(BS, LS, IDX_DIM)
            p_k.append(pp[:TP, DSA_QD:DSA_QD + DSA_KVD].reshape(B, L, N_KV_HEADS, HEAD_DIM))
            p_v.append(pp[:TP, DSA_QD + DSA_KVD:DSA_QD + 2 * DSA_KVD].reshape(B, L, N_KV_HEADS, HEAD_DIM))
            p_ki.append(kw[:TP, :IDX_DIM].reshape(B, L, IDX_DIM))
            s_k.append(kn); s_v.append(vn); s_ki.append(kin)
            w_out = dsa_w_out
        o_all = jnp.concatenate([op, os_], axis=0)
        x = matmul_res_ln(o_all, w_out, j, x, ln1_g[i], ln1_b[i])
        x = moe_layer(x, moe_wg[i], moe_bg[i], moe_we[i], moe_be[i], moe_w1, moe_w3, moe_w2, i,
                      ln2_g[i], ln2_b[i])
    xp = x[:TP].reshape(B, L, D)
    xs = x[TP:].reshape(BS, LS, D)
    return (xp, xs, jnp.stack(p_s), jnp.stack(p_c), jnp.stack(p_k), jnp.stack(p_v), jnp.stack(p_ki),
            jnp.stack(s_s), jnp.stack(s_c), jnp.stack(s_k), jnp.stack(s_v), jnp.stack(s_ki))
```

```python
import functools

import jax
import jax.numpy as jnp
from jax import lax
from jax.experimental import pallas as pl
from jax.experimental.pallas import tpu as pltpu

D_MODEL = 2048
DEPTH = 4
PAGE_SIZE = 128
N_MIXERS = 2
GDN_DK = 128
GDN_DV = 128
GDN_HK = D_MODEL // GDN_DK
GDN_HV = 2 * GDN_HK
GDN_KD = GDN_HK * GDN_DK
GDN_VD = GDN_HV * GDN_DV
GDN_CONV_DIM = 2 * GDN_KD + GDN_VD
GDN_MAIN = GDN_CONV_DIM + GDN_VD
CONV_W = 4
GDN_CHUNK = 64
HEAD_DIM = 128
N_HEADS = D_MODEL // HEAD_DIM
N_KV_HEADS = 4
KV_GROUP = N_HEADS // N_KV_HEADS
IDX_HEADS = 16
IDX_DIM = 128
DSA_QD = N_HEADS * HEAD_DIM
DSA_KVD = N_KV_HEADS * HEAD_DIM
DSA_MAIN = DSA_QD + 2 * DSA_KVD + IDX_HEADS * IDX_DIM
TOPK_MAX = 256
ROPE_THETA = 10000.0
N_GROUPS = 4
EXPERTS_PER_GROUP = 8
N_EXPERTS = N_GROUPS * EXPERTS_PER_GROUP
TOPK_EXPERTS = 2
D_EXPERT = D_MODEL // 4
ALPHA = (2 * DEPTH) ** 0.25
LN_EPS = 1e-5
NORM_EPS = 1e-6

LANES = 128
F32 = jnp.float32
BF16 = jnp.bfloat16
I32 = jnp.int32
VMEM_LIMIT = 56 * 1024 * 1024
INT_MIN = -2 ** 31
NEG_INF = float("-inf")


def _cparams(*sem):
    return pltpu.CompilerParams(dimension_semantics=sem, vmem_limit_bytes=VMEM_LIMIT)


def _bdot(a, b):
    return jnp.dot(a.astype(BF16), b.astype(BF16), preferred_element_type=F32)


def _bdot_nt(a, b):
    return lax.dot_general(a.astype(BF16), b.astype(BF16), (((1,), (1,)), ((), ())),
                           preferred_element_type=F32)


def _row_tile(T, cap):
    if T <= cap:
        return T
    best = None
    for t in range(16, cap + 1, 16):
        if T % t == 0:
            best = t
    assert best is not None, T
    return best


def _proj_kernel(x_ref, w_ref, wt_ref, *rest, rope_ranges, tail_rope, tail_scale):
    if rope_ranges:
        cos_ref, sin_ref, o_ref, t_ref, xs = rest
    else:
        o_ref, t_ref, xs = rest
    j = pl.program_id(1)

    @pl.when(j == 0)
    def _():
        xs[...] = x_ref[...].astype(BF16)

    def rope(a):
        return a * cos_ref[...] + pltpu.roll(a, HEAD_DIM // 2, 1) * sin_ref[...]

    xb = xs[...]
    acc = jnp.dot(xb, w_ref[...].astype(BF16), preferred_element_type=F32)
    if rope_ranges:
        per_tile = acc.shape[1] // HEAD_DIM
        parts = []
        for hh in range(per_tile):
            a = acc[:, hh * HEAD_DIM:(hh + 1) * HEAD_DIM]
            head = j * per_tile + hh
            is_rope = (head >= rope_ranges[0][0]) & (head < rope_ranges[0][1])
            for lo, hi in rope_ranges[1:]:
                is_rope = is_rope | ((head >= lo) & (head < hi))
            parts.append(jnp.where(is_rope, rope(a), a))
        acc = parts[0] if per_tile == 1 else jnp.concatenate(parts, axis=1)
    o_ref[...] = acc

    @pl.when(j == 0)
    def _():
        t = jnp.dot(xb, wt_ref[...].astype(BF16), preferred_element_type=F32)
        parts = []
        for hh in range(t.shape[1] // LANES):
            a = t[:, hh * LANES:(hh + 1) * LANES]
            parts.append(rope(a) if hh < tail_rope else a * tail_scale)
        t_ref[...] = parts[0] if len(parts) == 1 else jnp.concatenate(parts, axis=1)


def project(x, w, layer, n_main, tn, tm_cap, cos=None, sin=None, rope_ranges=(), tail_rope=0, tail_scale=1.0):
    T, D = x.shape
    tm = _row_tile(T, tm_cap)
    n_tail = w.shape[2] - n_main
    tw = -(-n_tail // LANES) * LANES
    w_tail = jnp.pad(w[layer, :, n_main:], ((0, 0), (0, tw - n_tail)))
    in_specs = [pl.BlockSpec((tm, D), lambda i, j: (i, 0)),
                pl.BlockSpec((None, D, tn), lambda i, j: (layer, 0, j)),
                pl.BlockSpec((D, tw), lambda i, j: (0, 0))]
    args = [x, w, w_tail]
    if rope_ranges:
        in_specs += [pl.BlockSpec((tm, LANES), lambda i, j: (i, 0))] * 2
        args += [cos, sin]
    return pl.pallas_call(
        functools.partial(_proj_kernel, rope_ranges=tuple(rope_ranges), tail_rope=tail_rope, tail_scale=tail_scale),
        grid=(T // tm, n_main // tn),
        in_specs=in_specs,
        out_specs=[pl.BlockSpec((tm, tn), lambda i, j: (i, j)),
                   pl.BlockSpec((tm, tw), lambda i, j: (i, 0))],
        out_shape=[jax.ShapeDtypeStruct((T, n_main), F32), jax.ShapeDtypeStruct((T, tw), F32)],
        scratch_shapes=[pltpu.VMEM((tm, D), BF16)],
        compiler_params=_cparams("parallel", "arbitrary"),
    )(*args)


def _mm_res_ln_kernel(x_ref, w_ref, r_ref, g_ref, b_ref, o_ref, *, nk):
    k = pl.program_id(1)
    part = jnp.dot(x_ref[...].astype(BF16), w_ref[...].astype(BF16), preferred_element_type=F32)

    @pl.when(k == 0)
    def _():
        o_ref[...] = part

    @pl.when(k > 0)
    def _():
        o_ref[...] += part

    @pl.when(k == nk - 1)
    def _():
        h = ALPHA * r_ref[...] + o_ref[...]
        mu = jnp.mean(h, -1, keepdims=True)
        hc = h - mu
        var = jnp.mean(hc * hc, -1, keepdims=True)
        o_ref[...] = hc * lax.rsqrt(var + LN_EPS) * g_ref[...] + b_ref[...]


def matmul_res_ln(x, w, layer, resid, g, b, tm_cap=688, tk=512):
    T, K = x.shape
    D = w.shape[2]
    tm = _row_tile(T, tm_cap)
    nk = K // tk
    return pl.pallas_call(
        functools.partial(_mm_res_ln_kernel, nk=nk),
        grid=(T // tm, nk),
        in_specs=[pl.BlockSpec((tm, tk), lambda i, k: (i, k)),
                  pl.BlockSpec((None, tk, D), lambda i, k: (layer, k, 0)),
                  pl.BlockSpec((tm, D), lambda i, k: (i, 0)),
                  pl.BlockSpec((1, D), lambda i, k: (0, 0)),
                  pl.BlockSpec((1, D), lambda i, k: (0, 0))],
        out_specs=pl.BlockSpec((tm, D), lambda i, k: (i, 0)),
        out_shape=jax.ShapeDtypeStruct((T, D), F32),
        compiler_params=_cparams("parallel", "arbitrary"),
    )(x, w, resid, g.reshape(1, D), b.reshape(1, D))


def _dsa_prompt_kernel(q_ref, qi0_ref, qi1_ref, wi_ref, k_ref, v_ref, ki_ref, o_ref,
                       kbf, vbf, kibf, key_s, bias_s, *, tq, seq, topk, s_step):
    i = pl.program_id(1)

    @pl.when(i == 0)
    def _():
        kbf[...] = k_ref[...].astype(BF16)
        vbf[...] = v_ref[...].astype(BF16)
        kibf[...] = ki_ref[...].astype(BF16)

    half = IDX_HEADS // 2
    qi_rows = [qi0_ref[:, h * IDX_DIM:(h + 1) * IDX_DIM].astype(BF16) for h in range(half)]
    qi_rows += [qi1_ref[:, h * IDX_DIM:(h + 1) * IDX_DIM].astype(BF16) for h in range(half)]
    qi_stack = jnp.concatenate(qi_rows, axis=0)
    wib = wi_ref[...].astype(BF16).astype(F32)
    q_rows = [jnp.concatenate([q_ref[:, (n * KV_GROUP + g) * HEAD_DIM:(n * KV_GROUP + g + 1) * HEAD_DIM]
                               for g in range(KV_GROUP)], axis=0).astype(BF16)
              for n in range(N_KV_HEADS)]

    def body(S):
        qpos = i * tq + lax.broadcasted_iota(I32, (tq, 1), 0)
        for c0 in range(0, S, s_step):
            s = _bdot_nt(qi_stack, kibf[c0:c0 + s_step, :])
            r = jnp.maximum(s, 0.0).astype(BF16).astype(F32)
            score = r[0:tq] * wib[:, 0:1]
            for h in range(1, IDX_HEADS):
                score = score + r[h * tq:(h + 1) * tq] * wib[:, h:h + 1]
            score = score + 0.0
            bits = pltpu.bitcast(score, I32)
            key = jnp.where(bits < 0, bits ^ jnp.int32(0x7FFFFFFF), bits)
            spos = c0 + lax.broadcasted_iota(I32, (tq, s_step), 1)
            key_s[:, c0:c0 + s_step] = jnp.where(spos <= qpos, key, jnp.int32(INT_MIN))

        def count_ge(cand):
            return jnp.sum((key_s[:, 0:S] >= cand).astype(I32), axis=1, keepdims=True)

        t0 = jnp.where(count_ge(jnp.zeros((tq, 1), I32)) >= topk, jnp.int32(0), jnp.int32(INT_MIN))
        t0 = jnp.broadcast_to(t0, (tq, 1))

        def bit_step(it, t):
            cand = t | jnp.left_shift(jnp.int32(1), 30 - it)
            return jnp.where(count_ge(cand) >= topk, cand, t)

        thr = lax.fori_loop(0, 31, bit_step, t0)

        keyv = key_s[:, 0:S]
        valid = lax.broadcasted_iota(I32, (tq, S), 1) <= qpos
        ge = keyv >= thr
        n_ge = jnp.sum((ge & valid).astype(I32), axis=1, keepdims=True)
        has_tie = jnp.max(n_ge) > topk
        bias_s[:, 0:S] = jnp.where(ge & valid, 0.0, NEG_INF)

        @pl.when(has_tie)
        def _():
            gt = keyv > thr
            n_gt = jnp.sum((gt & valid).astype(I32), axis=1, keepdims=True)
            room = (topk - n_gt).astype(F32)
            eq = ((keyv == thr) & valid)
            tri = (lax.broadcasted_iota(I32, (LANES, LANES), 0)
                   < lax.broadcasted_iota(I32, (LANES, LANES), 1)).astype(BF16)
            carry = jnp.zeros((tq, 1), F32)
            for c0 in range(0, S, LANES):
                eqc = eq[:, c0:c0 + LANES]
                before = carry + jnp.dot(eqc.astype(BF16), tri, preferred_element_type=F32)
                keep = (gt[:, c0:c0 + LANES] & valid[:, c0:c0 + LANES]) | (eqc & (before < room))
                bias_s[:, c0:c0 + LANES] = jnp.where(keep, 0.0, NEG_INF)
                carry = carry + jnp.sum(eqc.astype(F32), axis=1, keepdims=True)

        bias = bias_s[:, 0:S]
        for n in range(N_KV_HEADS):
            s = _bdot_nt(q_rows[n], kbf[0:S, n * HEAD_DIM:(n + 1) * HEAD_DIM]) * (HEAD_DIM ** -0.5)
            s = s.reshape(KV_GROUP, tq, S) + bias[None]
            m = jnp.max(s, axis=-1, keepdims=True)
            p = jnp.exp(s - m)
            l = jnp.sum(p, axis=-1, keepdims=True)
            o = jnp.dot(p.reshape(KV_GROUP * tq, S).astype(BF16), vbf[0:S, n * HEAD_DIM:(n + 1) * HEAD_DIM],
                        preferred_element_type=F32)
            o = o.reshape(KV_GROUP, tq, HEAD_DIM) / l
            for g in range(KV_GROUP):
                h = n * KV_GROUP + g
                o_ref[:, h * HEAD_DIM:(h + 1) * HEAD_DIM] = o[g].astype(o_ref.dtype)

    n_var = seq // s_step
    per = (seq // tq) // n_var
    for c in range(n_var):
        @pl.when(i // per == c)
        def _(c=c):
            body((c + 1) * s_step)


def dsa_prompt_attend(pp, tail, batch, seq, tq=128, s_step=256):
    topk = min(TOPK_MAX, seq // 4)
    s_step = min(s_step, seq)
    nq = seq // tq
    kcol = DSA_QD // DSA_KVD
    qicol = (DSA_QD + 2 * DSA_KVD) // (IDX_HEADS * IDX_DIM // 2)
    kicol = (DSA_QD + 2 * DSA_KVD + IDX_HEADS * IDX_DIM) // IDX_DIM
    assert (DSA_QD + 2 * DSA_KVD) % (IDX_HEADS * IDX_DIM // 2) == 0
    hq = IDX_HEADS * IDX_DIM // 2
    return pl.pallas_call(
        functools.partial(_dsa_prompt_kernel, tq=tq, seq=seq, topk=topk, s_step=s_step),
        grid=(batch, nq),
        in_specs=[pl.BlockSpec((tq, DSA_QD), lambda b, i: (b * nq + i, 0)),
                  pl.BlockSpec((tq, hq), lambda b, i: (b * nq + i, qicol)),
                  pl.BlockSpec((tq, hq), lambda b, i: (b * nq + i, qicol + 1)),
                  pl.BlockSpec((tq, LANES), lambda b, i: (b * nq + i, 1)),
                  pl.BlockSpec((seq, DSA_KVD), lambda b, i: (b, kcol)),
                  pl.BlockSpec((seq, DSA_KVD), lambda b, i: (b, kcol + 1)),
                  pl.BlockSpec((seq, IDX_DIM), lambda b, i: (b, 0))],
        out_specs=pl.BlockSpec((tq, DSA_QD), lambda b, i: (b * nq + i, 0)),
        out_shape=jax.ShapeDtypeStruct((batch * seq, DSA_QD), BF16),
        scratch_shapes=[pltpu.VMEM((seq, DSA_KVD), BF16), pltpu.VMEM((seq, DSA_KVD), BF16),
                        pltpu.VMEM((seq, IDX_DIM), BF16),
                        pltpu.VMEM((tq, seq), I32), pltpu.VMEM((tq, seq), F32)],
        compiler_params=_cparams("parallel", "arbitrary"),
    )(pp, pp, pp, tail, pp, pp, tail)


def _order_key(score):
    bits = pltpu.bitcast(score + 0.0, I32)
    return jnp.where(bits < 0, bits ^ jnp.int32(0x7FFFFFFF), bits)


def _dsa_sample_kernel(pt_ref, q_ref, qi0_ref, qi1_ref, wi_ref, kn_ref, vn_ref, kin_ref, ck_hbm, cv_hbm, cki_hbm,
                       o_ref, kibuf, kvbuf, key_s, bias_s, sc_s, p_s, sem_ki, sem_kv,
                       *, layer, ls, n_pages, ppc, topk):
    b = pl.program_id(0)
    past = n_pages * PAGE_SIZE
    S = past + LANES
    ck = ppc * PAGE_SIZE
    prow = PAGE_SIZE * N_KV_HEADS
    n_chunks = n_pages // ppc
    gk = 4 * PAGE_SIZE
    rows = KV_GROUP * ls

    def ki_copy(p):
        return pltpu.make_async_copy(cki_hbm.at[layer, pt_ref[b, p]], kibuf.at[p], sem_ki)

    def kv_copy(src, c, slot, i):
        return pltpu.make_async_copy(src.at[layer, pt_ref[b, c * ppc + i]],
                                     kvbuf.at[slot, pl.ds(i * prow, prow)], sem_kv.at[slot])

    def head_rows(slot, n):
        return kvbuf[slot, pl.ds(n, ck, stride=N_KV_HEADS), :].astype(BF16)

    def start_chunk(src, c, slot):
        for i in range(ppc):
            kv_copy(src, c, slot, i).start()

    def wait_chunk(src, c, slot):
        for i in range(ppc):
            kv_copy(src, c, slot, i).wait()

    def ki_start(p, carry):
        ki_copy(p).start()
        return carry

    def ki_wait(p, carry):
        ki_copy(p).wait()
        return carry

    lax.fori_loop(0, n_pages, ki_start, 0)
    start_chunk(ck_hbm, 0, 0)

    half = IDX_HEADS // 2
    qi_stack = jnp.concatenate([qi0_ref[:, h * IDX_DIM:(h + 1) * IDX_DIM] for h in range(half)]
                               + [qi1_ref[:, h * IDX_DIM:(h + 1) * IDX_DIM] for h in range(half)],
                               axis=0).astype(BF16)
    wib = wi_ref[...].astype(BF16).astype(F32)
    q_rows = [jnp.concatenate([q_ref[:, (n * KV_GROUP + g) * HEAD_DIM:(n * KV_GROUP + g + 1) * HEAD_DIM]
                               for g in range(KV_GROUP)], axis=0).astype(BF16)
              for n in range(N_KV_HEADS)]
    qpos = past + lax.broadcasted_iota(I32, (ls, 1), 0)
    zpad = jnp.zeros((LANES - ls, DSA_KVD), F32)

    def index_keys(ki_rows):
        s = _bdot_nt(qi_stack, ki_rows)
        r = jnp.maximum(s, 0.0).astype(BF16).astype(F32)
        score = r[0:ls] * wib[:, 0:1]
        for h in range(1, IDX_HEADS):
            score = score + r[h * ls:(h + 1) * ls] * wib[:, h:h + 1]
        return _order_key(score)

    lax.fori_loop(0, n_pages, ki_wait, 0)

    def index_step(g, carry):
        kic = kibuf[pl.ds(g * (gk // PAGE_SIZE), gk // PAGE_SIZE)].reshape(gk, IDX_DIM)
        key_s[:, pl.ds(pl.multiple_of(g * gk, gk), gk)] = index_keys(kic)
        return carry

    lax.fori_loop(0, past // gk, index_step, 0)
    kin_pad = jnp.concatenate([kin_ref[...], zpad[:, 0:IDX_DIM]], axis=0)
    new_pos = past + lax.broadcasted_iota(I32, (ls, LANES), 1)
    key_s[:, past:S] = jnp.where(new_pos <= qpos, index_keys(kin_pad), jnp.int32(INT_MIN))

    def count_ge(cand):
        return jnp.sum((key_s[...] >= cand).astype(I32), axis=1, keepdims=True)

    t0 = jnp.where(count_ge(jnp.zeros((ls, 1), I32)) >= topk, jnp.int32(0), jnp.int32(INT_MIN))

    def bit_step(it, t):
        cand = t | jnp.left_shift(jnp.int32(1), 30 - it)
        return jnp.where(count_ge(cand) >= topk, cand, t)

    thr = lax.fori_loop(0, 31, bit_step, t0)
    keyv = key_s[...]
    valid = lax.broadcasted_iota(I32, (ls, S), 1) <= qpos
    ge = (keyv >= thr) & valid
    n_ge = jnp.sum(ge.astype(I32), axis=1, keepdims=True)
    bias_s[...] = jnp.where(ge, 0.0, NEG_INF)

    @pl.when(jnp.max(n_ge) > topk)
    def _():
        n_gt = jnp.sum(((keyv > thr) & valid).astype(I32), axis=1, keepdims=True)
        room = (topk - n_gt).astype(F32)
        tri = (lax.broadcasted_iota(I32, (LANES, LANES), 0)
               < lax.broadcasted_iota(I32, (LANES, LANES), 1)).astype(BF16)

        def tie_step(c, carry):
            off = pl.multiple_of(c * LANES, LANES)
            kc = key_s[:, pl.ds(off, LANES)]
            ok = (off + lax.broadcasted_iota(I32, (ls, LANES), 1)) <= qpos
            eq = (kc == thr) & ok
            before = carry + jnp.dot(eq.astype(BF16), tri, preferred_element_type=F32)
            keep = ((kc > thr) & ok) | (eq & (before < room))
            bias_s[:, pl.ds(off, LANES)] = jnp.where(keep, 0.0, NEG_INF)
            return carry + jnp.sum(eq.astype(F32), axis=1, keepdims=True)

        lax.fori_loop(0, S // LANES, tie_step, jnp.zeros((ls, 1), F32))

    def masked_scores(n, k_rows, bias):
        s = _bdot_nt(q_rows[n], k_rows) * (HEAD_DIM ** -0.5)
        return (s.reshape(KV_GROUP, ls, s.shape[1]) + bias[None]).reshape(rows, s.shape[1])

    def k_step(c, carry):
        slot = c % 2
        wait_chunk(ck_hbm, c, slot)

        @pl.when(c + 1 < n_chunks)
        def _():
            start_chunk(ck_hbm, c + 1, 1 - slot)

        @pl.when(c + 1 == n_chunks)
        def _():
            start_chunk(cv_hbm, 0, 1 - slot)

        off = pl.multiple_of(c * ck, ck)
        bias = bias_s[:, pl.ds(off, ck)]
        for n in range(N_KV_HEADS):
            sc_s[n * rows:(n + 1) * rows, pl.ds(off, ck)] = masked_scores(n, head_rows(slot, n), bias)
        return carry

    lax.fori_loop(0, n_chunks, k_step, 0)
    kn_pad = jnp.concatenate([kn_ref[...], zpad], axis=0).astype(BF16)
    for n in range(N_KV_HEADS):
        sc_s[n * rows:(n + 1) * rows, past:S] = masked_scores(
            n, kn_pad[:, n * HEAD_DIM:(n + 1) * HEAD_DIM], bias_s[:, past:S])

    sc = sc_s[...]
    m = jnp.max(sc, axis=1, keepdims=True)
    p = jnp.exp(sc - m)
    l = jnp.sum(p, axis=1, keepdims=True)
    p_s[...] = p.astype(BF16)

    def v_step(v, acc):
        slot = (n_chunks + v) % 2
        wait_chunk(cv_hbm, v, slot)

        @pl.when(v + 1 < n_chunks)
        def _():
            start_chunk(cv_hbm, v + 1, 1 - slot)

        off = pl.multiple_of(v * ck, ck)
        return tuple(acc[n] + jnp.dot(p_s[n * rows:(n + 1) * rows, pl.ds(off, ck)], head_rows(slot, n),
                                      preferred_element_type=F32)
                     for n in range(N_KV_HEADS))

    acc = lax.fori_loop(0, n_chunks, v_step, tuple(jnp.zeros((rows, HEAD_DIM), F32) for _ in range(N_KV_HEADS)))
    vn_pad = jnp.concatenate([vn_ref[...], zpad], axis=0).astype(BF16)
    for n in range(N_KV_HEADS):
        o = acc[n] + jnp.dot(p_s[n * rows:(n + 1) * rows, past:S], vn_pad[:, n * HEAD_DIM:(n + 1) * HEAD_DIM],
                             preferred_element_type=F32)
        o = o / l[n * rows:(n + 1) * rows]
        for g in range(KV_GROUP):
            h = n * KV_GROUP + g
            o_ref[:, h * HEAD_DIM:(h + 1) * HEAD_DIM] = o[g * ls:(g + 1) * ls]


def dsa_sample_attend(pp, tail, cache_k, cache_v, cache_kidx, page_table, layer, row0, batch, ls, ppc=16):
    n_pool = cache_k.shape[1]
    n_pages = page_table.shape[1]
    past = n_pages * PAGE_SIZE
    topk = min(TOPK_MAX, (past + ls) // 4)
    assert row0 % ls == 0 and n_pages % ppc == 0 and n_pages % 4 == 0 and ls % 8 == 0
    rb = row0 // ls
    ck = cache_k.reshape(cache_k.shape[0], n_pool, PAGE_SIZE * N_KV_HEADS, HEAD_DIM)
    cv = cache_v.reshape(cache_v.shape[0], n_pool, PAGE_SIZE * N_KV_HEADS, HEAD_DIM)
    kcol = DSA_QD // DSA_KVD
    hq = IDX_HEADS * IDX_DIM // 2
    qicol = (DSA_QD + 2 * DSA_KVD) // hq
    kicol = (DSA_QD + 2 * DSA_KVD + IDX_HEADS * IDX_DIM) // IDX_DIM
    S = past + LANES
    return pl.pallas_call(
        functools.partial(_dsa_sample_kernel, layer=layer, ls=ls, n_pages=n_pages, ppc=ppc, topk=topk),
        grid_spec=pltpu.PrefetchScalarGridSpec(
            num_scalar_prefetch=1,
            grid=(batch,),
            in_specs=[pl.BlockSpec((ls, DSA_QD), lambda b, pt: (rb + b, 0)),
                      pl.BlockSpec((ls, hq), lambda b, pt: (rb + b, qicol)),
                      pl.BlockSpec((ls, hq), lambda b, pt: (rb + b, qicol + 1)),
                      pl.BlockSpec((ls, LANES), lambda b, pt: (rb + b, 1)),
                      pl.BlockSpec((ls, DSA_KVD), lambda b, pt: (rb + b, kcol)),
                      pl.BlockSpec((ls, DSA_KVD), lambda b, pt: (rb + b, kcol + 1)),
                      pl.BlockSpec((ls, IDX_DIM), lambda b, pt: (rb + b, 0)),
                      pl.BlockSpec(memory_space=pl.ANY),
                      pl.BlockSpec(memory_space=pl.ANY),
                      pl.BlockSpec(memory_space=pl.ANY)],
            out_specs=pl.BlockSpec((ls, DSA_QD), lambda b, pt: (b, 0)),
            scratch_shapes=[pltpu.VMEM((n_pages, PAGE_SIZE, IDX_DIM), F32),
                            pltpu.VMEM((2, ppc * PAGE_SIZE * N_KV_HEADS, HEAD_DIM), F32),
                            pltpu.VMEM((ls, S), I32), pltpu.VMEM((ls, S), F32),
                            pltpu.VMEM((N_HEADS * ls, S), F32), pltpu.VMEM((N_HEADS * ls, S), BF16),
                            pltpu.SemaphoreType.DMA(()), pltpu.SemaphoreType.DMA((2,))]),
        out_shape=jax.ShapeDtypeStruct((batch * ls, DSA_QD), F32),
        compiler_params=_cparams("arbitrary"),
    )(page_table, pp, pp, pp, tail, pp, pp, tail, ck, cv, cache_kidx)


def _split3(a):
    hi = a.astype(BF16)
    lo = (a - hi.astype(F32)).astype(BF16)
    return hi, lo


def _dot3(a_parts, b_parts):
    ah, al = a_parts
    bh, bl = b_parts
    d = functools.partial(jnp.dot, preferred_element_type=F32)
    return d(ah, bh) + (d(ah, bl) + d(al, bh))


def _tri_inverse(mats, order):
    n = mats[0].shape[0]
    eye = (lax.broadcasted_iota(I32, (n, n), 0) == lax.broadcasted_iota(I32, (n, n), 1)).astype(F32)
    ps = [eye - a for a in mats]
    xss = [_split3(-a) for a in mats]
    steps = max(0, (order - 1).bit_length() - 1)
    for _ in range(steps):
        xss = [_split3(_dot3(xs, xs)) for xs in xss]
        ps = [p + _dot3(_split3(p), xs) for p, xs in zip(ps, xss)]
    return ps


def _silu(x):
    return x * jax.nn.sigmoid(x)


def _gdn_kernel(xq_ref, xk_ref, xv_ref, z_ref, tail_ref, cq_ref, ck_ref, cv_ref, wq_ref, wk_ref, wv_ref,
                alog_ref, dtb_ref, nw_ref, s0_ref, o_ref, s_ref,
                beta_s, g_s, u_s, w_s, qk_s, qg_s, kd_s, el_s, *, seq, chunk, valid, hpb, unroll):
    hb = pl.program_id(1)
    C = chunk
    N = seq // C
    HALO = 8
    NV = 2 * hpb
    tail = tail_ref[...]
    beta = jax.nn.sigmoid(tail)
    x = tail + dtb_ref[...]
    softplus = jnp.maximum(x, 0.0) + jnp.log1p(jnp.exp(-jnp.abs(x)))
    g = -jnp.exp(alog_ref[...]) * softplus
    if valid < seq:
        is_real = lax.broadcasted_iota(I32, (seq, LANES), 0) < valid
        beta = jnp.where(is_real, beta, 0.0)
        g = jnp.where(is_real, g, 0.0)
    beta_s[...] = beta
    g_s[...] = g

    R = NV * C
    row = lax.broadcasted_iota(I32, (R, R), 0)
    col = lax.broadcasted_iota(I32, (R, R), 1)
    log2c = C.bit_length() - 1
    same_head = lax.shift_right_logical(row, log2c) == lax.shift_right_logical(col, log2c)
    incl = same_head & (row >= col)
    strict = same_head & (row > col)
    lane = lax.broadcasted_iota(I32, (C, LANES), 1)
    sub_t = lax.broadcasted_iota(I32, (LANES, C), 0)
    rowc = lax.broadcasted_iota(I32, (C, LANES), 0)

    def conv(xref, cref, wref, r0, c):
        prev = xref[pl.ds(pl.multiple_of(jnp.maximum(r0 - HALO, 0), HALO), HALO), :]
        win = jnp.concatenate([jnp.where(c == 0, cref[0], prev), xref[pl.ds(r0, C), :]], axis=0)
        acc = win[HALO - 3:HALO - 3 + C] * wref[0:1, :]
        for j in range(1, CONV_W):
            acc = acc + win[HALO - 3 + j:HALO - 3 + j + C] * wref[j:j + 1, :]
        return _silu(acc)

    def l2n(t):
        return t * lax.rsqrt(jnp.sum(t * t, -1, keepdims=True) + NORM_EPS)

    def prep_inputs(c):
        r0 = pl.multiple_of(c * C, C)
        qc = conv(xq_ref, cq_ref, wq_ref, r0, c)
        kc = conv(xk_ref, ck_ref, wk_ref, r0, c)
        vv = conv(xv_ref, cv_ref, wv_ref, r0, c)
        beta = beta_s[pl.ds(r0, C), :]
        gc = g_s[pl.ds(r0, C), :]
        sh = 1
        while sh < C:
            gc = gc + jnp.where(rowc >= sh, pltpu.roll(gc, sh, 0), 0.0)
            sh *= 2
        gc_t = gc.T
        qn = [l2n(qc[:, hl * GDN_DK:(hl + 1) * GDN_DK]) * (GDN_DK ** -0.5) for hl in range(hpb)]
        kn = [l2n(kc[:, hl * GDN_DK:(hl + 1) * GDN_DK]) for hl in range(hpb)]
        bcols, gcols, grows = [], [], []
        for e in range(NV):
            hv = NV * hb + e
            bcols.append(jnp.sum(jnp.where(lane == hv, beta, 0.0), axis=1, keepdims=True))
            gcols.append(jnp.sum(jnp.where(lane == GDN_HV + hv, gc, 0.0), axis=1, keepdims=True))
            grows.append(jnp.sum(jnp.where(sub_t == GDN_HV + hv, gc_t, 0.0), axis=0, keepdims=True))
        bcol = jnp.concatenate(bcols, axis=0)
        gcol = jnp.concatenate(gcols, axis=0)
        grow = jnp.concatenate(grows, axis=1)
        kn_st = jnp.concatenate([kn[e // 2] for e in range(NV)], axis=0)
        qn_st = jnp.concatenate([qn[e // 2] for e in range(NV)], axis=0)
        v_st = jnp.concatenate([vv[:, e * GDN_DV:(e + 1) * GDN_DV] for e in range(NV)], axis=0)
        decay = jnp.exp(jnp.where(incl, gcol - grow, NEG_INF))
        kb = kn_st * bcol
        a = jnp.where(strict, _bdot_nt(kb, kn_st) * decay, 0.0)
        qk_s[c] = jnp.where(incl, _bdot_nt(qn_st, kn_st) * decay, 0.0).astype(BF16)
        qg_s[c] = (qn_st * jnp.exp(gcol)).astype(BF16)
        for e in range(NV):
            glast = grows[e][:, C - 1:C]
            kd = kn[e // 2] * jnp.exp(glast - gcols[e])
            kd_s[e, c] = kd.T.astype(BF16)
            el_s[e, c] = jnp.broadcast_to(jnp.exp(glast), (8, LANES))
        return a, v_st * bcol, kb * jnp.exp(gcol)

    def prep(it, carry):
        cs = [it * unroll + u for u in range(unroll)]
        ins = [prep_inputs(c) for c in cs]
        tmats = _tri_inverse([a for a, _, _ in ins], C)
        for c, tmat, (_, vb, kbg) in zip(cs, tmats, ins):
            u_s[c] = _bdot(tmat, vb)
            w_s[c] = _bdot(tmat, kbg).astype(BF16)
        return carry

    lax.fori_loop(0, N // unroll, prep, 0)

    nw = nw_ref[...]
    s_ref[...] = s0_ref[...]

    def scan(c, carry):
        r0 = pl.multiple_of(c * C, C)
        d = functools.partial(jnp.dot, preferred_element_type=F32)
        u = u_s[c]
        w = w_s[c]
        qg = qg_s[c]
        sts = [s_ref[0, e] for e in range(NV)]
        sbs = [st.astype(BF16) for st in sts]
        vb = jnp.concatenate([u[e * C:(e + 1) * C] - d(w[e * C:(e + 1) * C], sbs[e]) for e in range(NV)],
                             axis=0).astype(BF16)
        o_intra = d(qk_s[c], vb)
        for e in range(NV):
            o = d(qg[e * C:(e + 1) * C], sbs[e]) + o_intra[e * C:(e + 1) * C]
            s_ref[0, e] = sts[e] * el_s[e, c][0:1, :] + d(kd_s[e, c], vb[e * C:(e + 1) * C])
            zf = z_ref[pl.ds(r0, C), e * GDN_DV:(e + 1) * GDN_DV]
            og = o * lax.rsqrt(jnp.mean(o * o, -1, keepdims=True) + NORM_EPS) * nw * _silu(zf)
            o_ref[pl.ds(r0, C), e * GDN_DV:(e + 1) * GDN_DV] = og.astype(o_ref.dtype)
        return carry

    lax.fori_loop(0, N, scan, 0)


def gdn_core(main, tail, conv0, conv_w, a_log, dt_bias, norm_w, s0, batch, seq, valid=None, hpb=2, unroll=4):
    valid = seq if valid is None else valid
    C = min(GDN_CHUNK, seq)
    N = seq // C
    assert seq % C == 0
    unroll = unroll if N % unroll == 0 else 1
    conv0p = jnp.pad(conv0, ((0, 0), (8 - (CONV_W - 1), 0), (0, 0)))
    alog = jnp.pad(a_log, (GDN_HV, LANES - 2 * GDN_HV)).reshape(1, LANES)
    dtb = jnp.pad(dt_bias, (GDN_HV, LANES - 2 * GDN_HV)).reshape(1, LANES)
    qw = GDN_DK * hpb
    vw = 2 * GDN_DV * hpb
    kblk = GDN_KD // qw
    vblk = 2 * GDN_KD // vw
    zblk = GDN_CONV_DIM // vw
    nv = 2 * hpb
    return pl.pallas_call(
        functools.partial(_gdn_kernel, seq=seq, chunk=C, valid=valid, hpb=hpb, unroll=unroll),
        grid=(batch, GDN_HK // hpb),
        in_specs=[pl.BlockSpec((seq, qw), lambda b, h: (b, h)),
                  pl.BlockSpec((seq, qw), lambda b, h: (b, kblk + h)),
                  pl.BlockSpec((seq, vw), lambda b, h: (b, vblk + h)),
                  pl.BlockSpec((seq, vw), lambda b, h: (b, zblk + h)),
                  pl.BlockSpec((seq, LANES), lambda b, h: (b, 0)),
                  pl.BlockSpec((1, 8, qw), lambda b, h: (b, 0, h)),
                  pl.BlockSpec((1, 8, qw), lambda b, h: (b, 0, kblk + h)),
                  pl.BlockSpec((1, 8, vw), lambda b, h: (b, 0, vblk + h)),
                  pl.BlockSpec((CONV_W, qw), lambda b, h: (0, h)),
                  pl.BlockSpec((CONV_W, qw), lambda b, h: (0, kblk + h)),
                  pl.BlockSpec((CONV_W, vw), lambda b, h: (0, vblk + h)),
                  pl.BlockSpec((1, LANES), lambda b, h: (0, 0)),
                  pl.BlockSpec((1, LANES), lambda b, h: (0, 0)),
                  pl.BlockSpec((1, GDN_DV), lambda b, h: (0, 0)),
                  pl.BlockSpec((1, nv, GDN_DK, GDN_DV), lambda b, h: (b, h, 0, 0))],
        out_specs=[pl.BlockSpec((seq, vw), lambda b, h: (b, h)),
                   pl.BlockSpec((1, nv, GDN_DK, GDN_DV), lambda b, h: (b, h, 0, 0))],
        out_shape=[jax.ShapeDtypeStruct((batch * seq, GDN_VD), BF16),
                   jax.ShapeDtypeStruct((batch, GDN_HV, GDN_DK, GDN_DV), F32)],
        scratch_shapes=[pltpu.VMEM((seq, LANES), F32), pltpu.VMEM((seq, LANES), F32),
                        pltpu.VMEM((N, nv * C, GDN_DV), F32), pltpu.VMEM((N, nv * C, GDN_DK), BF16),
                        pltpu.VMEM((N, nv * C, nv * C), BF16), pltpu.VMEM((N, nv * C, GDN_DK), BF16),
                        pltpu.VMEM((nv, N, GDN_DK, C), BF16), pltpu.VMEM((nv, N, 8, LANES), F32)],
        compiler_params=_cparams("parallel", "parallel"),
    )(main, main, main, main, tail, conv0p, conv0p, conv0p, conv_w, conv_w, conv_w,
      alog, dtb, norm_w.reshape(1, GDN_DV), s0)


def _expert_kernel(be_ref, x_ref, w1_ref, w3_ref, w2_ref, o_ref):
    del be_ref
    x = x_ref[...].astype(BF16)
    h1 = jnp.dot(x, w1_ref[...].astype(BF16), preferred_element_type=F32)
    h3 = jnp.dot(x, w3_ref[...].astype(BF16), preferred_element_type=F32)
    h = _silu(h1) * h3
    o_ref[...] = jnp.dot(h.astype(BF16), w2_ref[...].astype(BF16), preferred_element_type=F32)


def _combine_ln_kernel(x_ref, y0_ref, y1_ref, gate_ref, g_ref, b_ref, o_ref):
    gate = gate_ref[...]
    h = ALPHA * x_ref[...] + (y0_ref[...] * gate[:, 0:1] + y1_ref[...] * gate[:, 1:2])
    mu = jnp.mean(h, -1, keepdims=True)
    hc = h - mu
    var = jnp.mean(hc * hc, -1, keepdims=True)
    o_ref[...] = hc * lax.rsqrt(var + LN_EPS) * g_ref[...] + b_ref[...]


def moe_layer(xt, wg, bg, we, be, w1, w3, w2, layer, ln_g, ln_b, blk=128):
    T, D = xt.shape
    E = N_EXPERTS
    K = TOPK_EXPERTS
    xb16 = xt.astype(BF16)
    lg = jnp.dot(xb16, wg.astype(BF16), preferred_element_type=F32) + bg
    pg = jax.nn.softmax(lg, axis=-1)
    gsel = jnp.argmax(lg, axis=-1)
    le = (jnp.dot(xb16, we.astype(BF16), preferred_element_type=F32) + be).reshape(T, N_GROUPS, EXPERTS_PER_GROUP)
    le_g = jnp.take_along_axis(le, gsel[:, None, None], axis=1)[:, 0]
    pe = jax.nn.softmax(le_g, axis=-1)
    top_p, top_i = lax.top_k(pe, K)
    gate = top_p / jnp.sum(top_p, -1, keepdims=True) * jnp.take_along_axis(pg, gsel[:, None], axis=1)
    eidx = (gsel[:, None] * EXPERTS_PER_GROUP + top_i).astype(I32)

    A = T * K
    nb = A // blk + E
    cnt = jnp.sum((eidx[:, :, None] == jnp.arange(E, dtype=I32)).astype(I32), axis=1)
    cum = jnp.cumsum(cnt, axis=0) - cnt
    counts = jnp.sum(cnt, axis=0)
    padded = (counts + blk - 1) // blk * blk
    pend = jnp.cumsum(padded)
    pstart = pend - padded
    dest = pstart[eidx] + jnp.take_along_axis(cum, eidx, axis=1)
    tok = jnp.broadcast_to(jnp.arange(T, dtype=I32)[:, None], (T, K))
    slot_tok = jnp.zeros((nb * blk,), I32).at[dest.reshape(-1)].set(tok.reshape(-1))
    blk_start = jnp.arange(nb, dtype=I32) * blk
    blk_e = jnp.minimum(jnp.sum((pend[None, :] <= blk_start[:, None]).astype(I32), axis=1), E - 1).astype(I32)
    xb = xb16[slot_tok]

    yb = pl.pallas_call(
        _expert_kernel,
        grid_spec=pltpu.PrefetchScalarGridSpec(
            num_scalar_prefetch=1,
            grid=(nb,),
            in_specs=[pl.BlockSpec((blk, D), lambda i, be_: (i, 0)),
                      pl.BlockSpec((None, None, D, D_EXPERT), lambda i, be_: (layer, be_[i], 0, 0)),
                      pl.BlockSpec((None, None, D, D_EXPERT), lambda i, be_: (layer, be_[i], 0, 0)),
                      pl.BlockSpec((None, None, D_EXPERT, D), lambda i, be_: (layer, be_[i], 0, 0))],
            out_specs=pl.BlockSpec((blk, D), lambda i, be_: (i, 0))),
        out_shape=jax.ShapeDtypeStruct((nb * blk, D), F32),
        compiler_params=_cparams("arbitrary"),
    )(blk_e, xb, w1, w3, w2)

    y0 = yb[dest[:, 0]]
    y1 = yb[dest[:, 1]]
    tm = _row_tile(T, 344)
    return pl.pallas_call(
        _combine_ln_kernel,
        grid=(T // tm,),
        in_specs=[pl.BlockSpec((tm, D), lambda i: (i, 0)),
                  pl.BlockSpec((tm, D), lambda i: (i, 0)),
                  pl.BlockSpec((tm, D), lambda i: (i, 0)),
                  pl.BlockSpec((tm, K), lambda i: (i, 0)),
                  pl.BlockSpec((1, D), lambda i: (0, 0)),
                  pl.BlockSpec((1, D), lambda i: (0, 0))],
        out_specs=pl.BlockSpec((tm, D), lambda i: (i, 0)),
        out_shape=jax.ShapeDtypeStruct((T, D), F32),
        compiler_params=_cparams("parallel"),
    )(xt, y0, y1, gate, ln_g.reshape(1, D), ln_b.reshape(1, D))


def _rope_tables(pos):
    half = HEAD_DIM // 2
    inv = jnp.power(ROPE_THETA, -jnp.arange(half, dtype=F32) / half)
    ang = pos.astype(F32)[:, None] * inv[None, :]
    cos = jnp.cos(ang)
    sin = jnp.sin(ang)
    return jnp.concatenate([cos, cos], -1), jnp.concatenate([-sin, sin], -1)


def kernel(x_prompt, x_sample, state_gdn_s, state_gdn_conv, cache_k, cache_v, cache_kidx, page_table,
           gdn_w_in, gdn_conv_w, gdn_a_log, gdn_dt_bias, gdn_norm_w, gdn_w_out,
           dsa_w_in, dsa_w_out, ln1_g, ln1_b, ln2_g, ln2_b,
           moe_wg, moe_bg, moe_we, moe_be, moe_w1, moe_w3, moe_w2):
    B, L, D = x_prompt.shape
    BS, LS, _ = x_sample.shape
    TP = B * L
    past = page_table.shape[1] * PAGE_SIZE
    x = jnp.concatenate([x_prompt.reshape(TP, D), x_sample.reshape(BS * LS, D)], axis=0)
    pos = jnp.concatenate([jnp.tile(jnp.arange(L), B), jnp.tile(past + jnp.arange(LS), BS)])
    cos, sin = _rope_tables(pos)
    nq = DSA_QD // HEAD_DIM
    nkv = DSA_KVD // HEAD_DIM
    rope_ranges = ((0, nq + nkv), (nq + 2 * nkv, nq + 2 * nkv + IDX_HEADS))
    wi_scale = IDX_HEADS ** -0.5 * IDX_DIM ** -0.5

    p_s, p_c, s_s, s_c = [], [], [], []
    p_k, p_v, p_ki, s_k, s_v, s_ki = [], [], [], [], [], []
    for i in range(DEPTH):
        j = i // N_MIXERS
        if i % N_MIXERS == 0:
            main, tail = project(x, gdn_w_in, j, GDN_MAIN, tn=512, tm_cap=688)
            gp = (gdn_conv_w[j], gdn_a_log[j], gdn_dt_bias[j], gdn_norm_w[j])
            c0 = jnp.zeros((B, CONV_W - 1, GDN_CONV_DIM), F32)
            s0 = jnp.zeros((B, GDN_HV, GDN_DK, GDN_DV), F32)
            op, sp = gdn_core(main, tail, c0, *gp, s0, B, L)
            cp = jnp.stack([main[b * L + L - (CONV_W - 1):(b + 1) * L, :GDN_CONV_DIM] for b in range(B)])
            pad_rows = ((0, 0), (0, GDN_CHUNK - LS), (0, 0))
            main_s = jnp.pad(main[TP:].reshape(BS, LS, GDN_MAIN), pad_rows).reshape(BS * GDN_CHUNK, GDN_MAIN)
            tail_s = jnp.pad(tail[TP:].reshape(BS, LS, LANES), pad_rows).reshape(BS * GDN_CHUNK, LANES)
            osp, ss = gdn_core(main_s, tail_s, state_gdn_conv[j], *gp, state_gdn_s[j], BS, GDN_CHUNK, valid=LS)
            os_ = osp.reshape(BS, GDN_CHUNK, GDN_VD)[:, :LS].reshape(BS * LS, GDN_VD)
            cs = main[TP:, :GDN_CONV_DIM].reshape(BS, LS, GDN_CONV_DIM)[:, LS - (CONV_W - 1):]
            p_s.append(sp); p_c.append(cp); s_s.append(ss); s_c.append(cs)
            w_out = gdn_w_out
        else:
            pp, kw = project(x, dsa_w_in, j, DSA_MAIN, tn=2 * HEAD_DIM, tm_cap=688, cos=cos, sin=sin,
                             rope_ranges=rope_ranges, tail_rope=1, tail_scale=wi_scale)
            op = dsa_prompt_attend(pp, kw, B, L)
            os_ = dsa_sample_attend(pp, kw, cache_k, cache_v, cache_kidx, page_table, j, TP, BS, LS).astype(BF16)
            kn = pp[TP:, DSA_QD:DSA_QD + DSA_KVD].reshape(BS, LS, N_KV_HEADS, HEAD_DIM)
            vn = pp[TP:, DSA_QD + DSA_KVD:DSA_QD + 2 * DSA_KVD].reshape(BS, LS, N_KV_HEADS, HEAD_DIM)
            kin = kw[TP:, :IDX_DIM].reshape(BS, LS, IDX_DIM)
            p_k.append(pp[:TP, DSA_QD:DSA_QD + DSA_KVD].reshape(B, L, N_KV_HEADS, HEAD_DIM))
            p_v.append(pp[:TP, DSA_QD + DSA_KVD:DSA_QD + 2 * DSA_KVD].reshape(B, L, N_KV_HEADS, HEAD_DIM))
            p_ki.append(kw[:TP, :IDX_DIM].reshape(B, L, IDX_DIM))
            s_k.append(kn); s_v.append(vn); s_ki.append(kin)
            w_out = dsa_w_out
        o_all = jnp.concatenate([op, os_], axis=0)
        x = matmul_res_ln(o_all, w_out, j, x, ln1_g[i], ln1_b[i])
        x = moe_layer(x, moe_wg[i], moe_bg[i], moe_we[i], moe_be[i], moe_w1, moe_w3, moe_w2, i,
                      ln2_g[i], ln2_b[i])
    xp = x[:TP].reshape(B, L, D)
    xs = x[TP:].reshape(BS, LS, D)
    return (xp, xs, jnp.stack(p_s), jnp.stack(p_c), jnp.stack(p_k), jnp.stack(p_v), jnp.stack(p_ki),
            jnp.stack(s_s), jnp.stack(s_c), jnp.stack(s_k), jnp.stack(s_v), jnp.stack(s_ki))
```
